```python
import jax, jax.numpy as jnp
from jax import lax
import numpy as np

D_MODEL = 1024
BATCH = 16
SEQ = 2048
DEPTH = 4

HEAD_DIM = D_MODEL // 16
NSA_HEADS = 8
NSA_KV_GROUPS = 2
FOX_HEADS = 4
MOBA_HEADS = 4
MIX_WIDTH = (NSA_HEADS + FOX_HEADS + MOBA_HEADS) * HEAD_DIM
ROPE_DIM = HEAD_DIM // 4
ROPE_THETA = 500000.0
CMP_LEN = 32
CMP_STRIDE = 16
CMP_HIDDEN = 4 * HEAD_DIM
SLC_LEN = 64
SLC_TOPN = 8
WIN = 512
MOBA_BLOCK = 256
MOBA_TOPK = 3
QBLOCK = 128
GATHER_CHUNK = 32
D_FF = 256 * ((8 * D_MODEL // 3 + 255) // 256)
FOX_FGATE_BIAS = 3.0
MAX_POS_OFFSET = 4096
EPS = 1e-6

_NSA_KV = NSA_KV_GROUPS * HEAD_DIM
IN_SPLITS = (
    ("nsa_q", NSA_HEADS * HEAD_DIM),
    ("nsa_kc", _NSA_KV), ("nsa_vc", _NSA_KV),
    ("nsa_ks", _NSA_KV), ("nsa_vs", _NSA_KV),
    ("nsa_kw", _NSA_KV), ("nsa_vw", _NSA_KV),
    ("nsa_gate", 3 * NSA_HEADS),
    ("fox_q", FOX_HEADS * HEAD_DIM), ("fox_k", FOX_HEADS * HEAD_DIM),
    ("fox_v", FOX_HEADS * HEAD_DIM), ("fox_f", FOX_HEADS),
    ("moba_q", MOBA_HEADS * HEAD_DIM), ("moba_k", MOBA_HEADS * HEAD_DIM),
    ("moba_v", MOBA_HEADS * HEAD_DIM),
)
IN_COLS = sum(w for _, w in IN_SPLITS)

kernel_name = "hymba_nsa_fox_moba_macaron_adaln"


def rms_norm(x, g):
    xf = x.astype(jnp.float32)
    y = xf * lax.rsqrt(jnp.mean(xf * xf, axis=-1, keepdims=True) + EPS)
    return (y * g.astype(jnp.float32)).astype(x.dtype)


def ada_norm(x, g, shift, scale):
    return rms_norm(x, g) * (1 + scale[:, None, :]) + shift[:, None, :]


def swiglu(h, w13, w2):
    a, b = jnp.split(h @ w13, 2, axis=-1)
    return (jax.nn.silu(a) * b) @ w2


def masked_softmax(s, mask):
    s = jnp.where(mask, s.astype(jnp.float32), -jnp.inf)
    m = jnp.max(s, axis=-1, keepdims=True)
    m = jnp.where(jnp.isfinite(m), m, 0.0)
    e = jnp.exp(s - m)
    return e / jnp.maximum(jnp.sum(e, axis=-1, keepdims=True), 1e-30)


def rope_tables(positions):
    inv = ROPE_THETA ** (-jnp.arange(0, ROPE_DIM, 2, dtype=jnp.float32) / ROPE_DIM)
    ang = positions.astype(jnp.float32)[..., None] * inv
    return jnp.cos(ang), jnp.sin(ang)


def apply_rope(x, cos, sin):
    r, rest = x[..., :ROPE_DIM], x[..., ROPE_DIM:]
    r1, r2 = r[..., :ROPE_DIM // 2], r[..., ROPE_DIM // 2:]
    c, s = cos[:, :, None, :], sin[:, :, None, :]
    rot = jnp.concatenate([r1 * c - r2 * s, r2 * c + r1 * s], axis=-1).astype(x.dtype)
    return jnp.concatenate([rot, rest], axis=-1)


def split_blocks(a, axis, size):
    n = a.shape[axis] // size
    a = a.reshape(a.shape[:axis] + (n, size) + a.shape[axis + 1:])
    return jnp.moveaxis(a, axis, 0)


def merge_blocks(a, axis):
    a = jnp.moveaxis(a, 0, axis)
    return a.reshape(a.shape[:axis] + (a.shape[axis] * a.shape[axis + 1],) + a.shape[axis + 2:])


def nsa_mixer(q, kc, vc, ks, vs, kw, vw, gates, cmp_pos, cmp_w1, cmp_w2):
    B, S = q.shape[0], q.shape[1]
    G, R, D = NSA_KV_GROUPS, NSA_HEADS // NSA_KV_GROUPS, HEAD_DIM
    scale = D ** -0.5
    qg = q.reshape(B, S, G, R, D).transpose(0, 2, 3, 1, 4)
    t = np.arange(S)

    n_c = (S - CMP_LEN) // CMP_STRIDE + 1
    win_idx = np.arange(n_c)[:, None] * CMP_STRIDE + np.arange(CMP_LEN)[None, :]

    def compress(z, pos, w1, w2):
        zb = z[:, win_idx] + pos[None, None, :, None, :]
        zb = zb.transpose(0, 3, 1, 2, 4).reshape(B, G, n_c, CMP_LEN * D)
        return jax.nn.gelu(zb @ w1) @ w2

    kcmp = compress(kc, cmp_pos[0], cmp_w1[0], cmp_w2[0])
    vcmp = compress(vc, cmp_pos[1], cmp_w1[1], cmp_w2[1])
    cmp_end = np.arange(n_c) * CMP_STRIDE + CMP_LEN - 1
    cmp_mask = cmp_end[None, :] <= t[:, None]
    s_cmp = jnp.einsum('bgrsd,bgcd->bgrsc', qg, kcmp) * scale
    p_cmp = masked_softmax(s_cmp, cmp_mask)
    o_cmp = jnp.einsum('bgrsc,bgcd->bgrsd', p_cmp.astype(vcmp.dtype), vcmp)

    n_s = S // SLC_LEN
    c0 = np.arange(n_c)[:, None] * CMP_STRIDE
    j0 = np.arange(n_s)[None, :] * SLC_LEN
    overlap = ((c0 < j0 + SLC_LEN) & (c0 + CMP_LEN > j0)).astype(np.float32)
    imp = jnp.einsum('bgrsc,cj->bgsj', p_cmp, jnp.asarray(overlap))
    tb = (t // SLC_LEN)[:, None]
    jj = np.arange(n_s)[None, :]
    valid = jj <= tb
    forced = (jj == 0) | (jj == tb) | (jj == tb - 1)
    imp = jnp.where(valid, jnp.where(forced, jnp.inf, imp), -jnp.inf)
    n_top = min(SLC_TOPN, n_s)
    top_val, top_idx = lax.top_k(imp, n_top)
    top_ok = top_val > -jnp.inf

    ks_blk = ks.reshape(B, n_s, SLC_LEN, G, D).transpose(0, 3, 1, 2, 4)
    vs_blk = vs.reshape(B, n_s, SLC_LEN, G, D).transpose(0, 3, 1, 2, 4)
    C = GATHER_CHUNK
    bi = jnp.arange(B)[:, None, None]
    gi = jnp.arange(G)[None, :, None]
    k_sel = n_top * SLC_LEN

    def slc_chunk(args):
        qc, idx, ok, q0 = args
        flat = idx.reshape(B, G, C * n_top)
        kg = ks_blk[bi, gi, flat].reshape(B, G, C, k_sel, D)
        vg = vs_blk[bi, gi, flat].reshape(B, G, C, k_sel, D)
        kpos = (idx[..., None] * SLC_LEN + jnp.arange(SLC_LEN)).reshape(B, G, C, k_sel)
        qpos = q0 + jnp.arange(C)
        mask = jnp.repeat(ok, SLC_LEN, axis=-1) & (kpos <= qpos[:, None])
        s = jnp.einsum('bgrcd,bgckd->bgrck', qc, kg) * scale
        p = masked_softmax(s, mask[:, :, None])
        return jnp.einsum('bgrck,bgckd->bgrcd', p.astype(vg.dtype), vg)

    o_slc = lax.map(slc_chunk, (split_blocks(qg, 3, C), split_blocks(top_idx, 2, C),
                                split_blocks(top_ok, 2, C), jnp.arange(S // C) * C))
    o_slc = merge_blocks(o_slc, 3)

    kw_p = jnp.pad(kw.transpose(0, 2, 1, 3), ((0, 0), (0, 0), (WIN, 0), (0, 0)))
    vw_p = jnp.pad(vw.transpose(0, 2, 1, 3), ((0, 0), (0, 0), (WIN, 0), (0, 0)))
    span = WIN + QBLOCK

    def win_block(args):
        qb, q0 = args
        kb = lax.dynamic_slice_in_dim(kw_p, q0, span, axis=2)
        vb = lax.dynamic_slice_in_dim(vw_p, q0, span, axis=2)
        qpos = q0 + jnp.arange(QBLOCK)
        kpos = q0 - WIN + jnp.arange(span)
        d = qpos[:, None] - kpos[None, :]
        mask = (kpos[None, :] >= 0) & (d >= 0) & (d < WIN)
        s = jnp.einsum('bgrqd,bgkd->bgrqk', qb, kb) * scale
        p = masked_softmax(s, mask)
        return jnp.einsum('bgrqk,bgkd->bgrqd', p.astype(vb.dtype), vb)

    o_win = lax.map(win_block, (split_blocks(qg, 3, QBLOCK), jnp.arange(S // QBLOCK) * QBLOCK))
    o_win = merge_blocks(o_win, 3)

    g = gates.reshape(B, S, G, R, 3).transpose(0, 2, 3, 1, 4)
    o = g[..., 0:1] * o_cmp + g[..., 1:2] * o_slc + g[..., 2:3] * o_win
    return o.transpose(0, 3, 1, 2, 4).reshape(B, S, NSA_HEADS * D)


def fox_mixer(q, k, v, f_logit, f_bias):
    B, S, H, D = q.shape
    scale = D ** -0.5
    logf = jax.nn.log_sigmoid((f_logit + f_bias).astype(jnp.float32))
    F = jnp.cumsum(logf, axis=1).transpose(0, 2, 1)
    qt, kt, vt = (a.transpose(0, 2, 1, 3) for a in (q, k, v))
    kpos = jnp.arange(S)

    def blk(args):
        qb, Fq, q0 = args
        s = (jnp.einsum('bhqd,bhkd->bhqk', qb, kt).astype(jnp.float32) * scale
             + Fq[..., None] - F[:, :, None, :])
        qpos = q0 + jnp.arange(QBLOCK)
        p = masked_softmax(s, kpos[None, :] <= qpos[:, None])
        return jnp.einsum('bhqk,bhkd->bhqd', p.astype(vt.dtype), vt)

    o = lax.map(blk, (split_blocks(qt, 2, QBLOCK), split_blocks(F, 2, QBLOCK),
                      jnp.arange(S // QBLOCK) * QBLOCK))
    o = merge_blocks(o, 2)
    return o.transpose(0, 2, 1, 3).reshape(B, S, H * D)


def moba_mixer(q, k, v):
    B, S, H, D = q.shape
    scale = D ** -0.5
    qt, kt, vt = (a.transpose(0, 2, 1, 3) for a in (q, k, v))
    NB = -(-S // MOBA_BLOCK)
    Sp = NB * MOBA_BLOCK
    kp = jnp.pad(kt, ((0, 0), (0, 0), (0, Sp - S), (0, 0)))
    vp = jnp.pad(vt, ((0, 0), (0, 0), (0, Sp - S), (0, 0)))
    kb = kp.reshape(B, H, NB, MOBA_BLOCK, D)
    vb = vp.reshape(B, H, NB, MOBA_BLOCK, D)
    n_sel = min(MOBA_TOPK, NB - 1)
    C = GATHER_CHUNK
    t = np.arange(S)
    xs = (split_blocks(qt, 2, C), jnp.arange(S // C) * C)
    if n_sel > 0:
        kmean = jnp.mean(kb.astype(jnp.float32), axis=3)
        gate = jnp.einsum('bhsd,bhnd->bhsn', qt.astype(jnp.float32), kmean)
        past = np.arange(NB)[None, :] < (t // MOBA_BLOCK)[:, None]
        gate = jnp.where(past, gate, -jnp.inf)
        top_val, top_idx = lax.top_k(gate, n_sel)
        xs = xs + (split_blocks(top_idx, 2, C), split_blocks(top_val > -jnp.inf, 2, C))
    bi = jnp.arange(B)[:, None, None]
    hi = jnp.arange(H)[None, :, None]
    k_sel = n_sel * MOBA_BLOCK

    def chunk(args):
        qc, q0 = args[0], args[1]
        own = q0 // MOBA_BLOCK
        k_own = lax.dynamic_slice_in_dim(kp, own * MOBA_BLOCK, MOBA_BLOCK, axis=2)
        v_own = lax.dynamic_slice_in_dim(vp, own * MOBA_BLOCK, MOBA_BLOCK, axis=2)
        qpos = q0 + jnp.arange(C)
        own_pos = own * MOBA_BLOCK + jnp.arange(MOBA_BLOCK)
        s_own = jnp.einsum('bhcd,bhkd->bhck', qc, k_own) * scale
        m_own = jnp.broadcast_to(own_pos[None, :] <= qpos[:, None], s_own.shape)
        if n_sel == 0:
            p = masked_softmax(s_own, m_own)
            return jnp.einsum('bhck,bhkd->bhcd', p.astype(v_own.dtype), v_own)
        idx, ok = args[2], args[3]
        flat = idx.reshape(B, H, C * n_sel)
        kg = kb[bi, hi, flat].reshape(B, H, C, k_sel, D)
        vg = vb[bi, hi, flat].reshape(B, H, C, k_sel, D)
        s_sel = jnp.einsum('bhcd,bhckd->bhck', qc, kg) * scale
        m_sel = jnp.repeat(ok, MOBA_BLOCK, axis=-1)
        p = masked_softmax(jnp.concatenate([s_sel, s_own], axis=-1),
                           jnp.concatenate([m_sel, m_own], axis=-1))
        p = p.astype(vg.dtype)
        return (jnp.einsum('bhck,bhckd->bhcd', p[..., :k_sel], vg)
                + jnp.einsum('bhck,bhkd->bhcd', p[..., k_sel:], v_own))

    o = merge_blocks(lax.map(chunk, xs), 2)
    return o.transpose(0, 2, 1, 3).reshape(B, S, H * D)


def token_mixer(h, w_in, f_bias, cmp_pos, cmp_w1, cmp_w2, w_out, cos, sin):
    B, S, _ = h.shape
    proj = h @ w_in
    parts = {}
    off = 0
    for name, width in IN_SPLITS:
        parts[name] = proj[..., off:off + width]
        off += width

    def heads(a, n):
        return a.reshape(B, S, n, HEAD_DIM)

    o_nsa = nsa_mixer(
        apply_rope(heads(parts['nsa_q'], NSA_HEADS), cos, sin),
        apply_rope(heads(parts['nsa_kc'], NSA_KV_GROUPS), cos, sin), heads(parts['nsa_vc'], NSA_KV_GROUPS),
        apply_rope(heads(parts['nsa_ks'], NSA_KV_GROUPS), cos, sin), heads(parts['nsa_vs'], NSA_KV_GROUPS),
        apply_rope(heads(parts['nsa_kw'], NSA_KV_GROUPS), cos, sin), heads(parts['nsa_vw'], NSA_KV_GROUPS),
        jax.nn.sigmoid(parts['nsa_gate']).reshape(B, S, NSA_HEADS, 3),
        cmp_pos, cmp_w1, cmp_w2)
    o_fox = fox_mixer(heads(parts['fox_q'], FOX_HEADS), heads(parts['fox_k'], FOX_HEADS),
                      heads(parts['fox_v'], FOX_HEADS), parts['fox_f'], f_bias)
    o_moba = moba_mixer(apply_rope(heads(parts['moba_q'], MOBA_HEADS), cos, sin),
                        apply_rope(heads(parts['moba_k'], MOBA_HEADS), cos, sin),
                        heads(parts['moba_v'], MOBA_HEADS))
    return jnp.concatenate([o_nsa, o_fox, o_moba], axis=-1) @ w_out


def setup_inputs(seed: int = 0) -> dict:
    key = jax.random.key(seed)
    ks = jax.random.split(key, 16)
    D = D_MODEL

    def nrm(k, shape, s):
        return jax.random.normal(k, shape, jnp.float32) * s

    x = nrm(ks[0], (BATCH, SEQ, D), 1.0)
    c = nrm(ks[1], (BATCH, D), 1.0)
    offset = jax.random.randint(ks[2], (BATCH, 1), 0, MAX_POS_OFFSET, dtype=jnp.int32)
    positions = offset + jnp.arange(SEQ, dtype=jnp.int32)[None, :]
    return {
        "x": x,
        "c": c,
        "positions": positions,
        "norm_g": 1.0 + nrm(ks[3], (DEPTH, 3, D), 0.05),
        "w_ada": nrm(ks[4], (DEPTH, D, 9 * D), 0.5 * D ** -0.5),
        "b_ada": nrm(ks[5], (DEPTH, 9 * D), 0.01),
        "w_in": nrm(ks[6], (DEPTH, D, IN_COLS), D ** -0.5),
        "fox_fbias": FOX_FGATE_BIAS + nrm(ks[7], (DEPTH, FOX_HEADS), 0.1),
        "cmp_pos": nrm(ks[8], (DEPTH, 2, CMP_LEN, HEAD_DIM), 0.1),
        "cmp_w1": nrm(ks[9], (DEPTH, 2, CMP_LEN * HEAD_DIM, CMP_HIDDEN), (CMP_LEN * HEAD_DIM) ** -0.5),
        "cmp_w2": nrm(ks[10], (DEPTH, 2, CMP_HIDDEN, HEAD_DIM), CMP_HIDDEN ** -0.5),
        "w_out": nrm(ks[11], (DEPTH, MIX_WIDTH, D), MIX_WIDTH ** -0.5),
        "ffn_w13": nrm(ks[12], (DEPTH, 2, D, 2 * D_FF), D ** -0.5),
        "ffn_w2": nrm(ks[13], (DEPTH, 2, D_FF, D), D_FF ** -0.5),
        "final_g": 1.0 + nrm(ks[14], (D,), 0.05),
    }


def reference(x, c, positions, norm_g, w_ada, b_ada, w_in, fox_fbias, cmp_pos, cmp_w1, cmp_w2,
              w_out, ffn_w13, ffn_w2, final_g):
    cos, sin = rope_tables(positions)
    c_act = jax.nn.silu(c)
    B = x.shape[0]
    for l in range(DEPTH):
        mod = (c_act @ w_ada[l] + b_ada[l]).reshape(B, 3, 3, D_MODEL)
        h = ada_norm(x, norm_g[l, 0], mod[:, 0, 0], mod[:, 0, 1])
        x = x + 0.5 * mod[:, 0, 2][:, None, :] * swiglu(h, ffn_w13[l, 0], ffn_w2[l, 0])
        h = ada_norm(x, norm_g[l, 1], mod[:, 1, 0], mod[:, 1, 1])
        x = x + mod[:, 1, 2][:, None, :] * token_mixer(h, w_in[l], fox_fbias[l], cmp_pos[l], cmp_w1[l],
                                                        cmp_w2[l], w_out[l], cos, sin)
        h = ada_norm(x, norm_g[l, 2], mod[:, 2, 0], mod[:, 2, 1])
        x = x + 0.5 * mod[:, 2, 2][:, None, :] * swiglu(h, ffn_w13[l, 1], ffn_w2[l, 1])
    return rms_norm(x, final_g)
```

```python
import functools

import numpy as np
import jax
import jax.numpy as jnp
from jax import lax
from jax.experimental import pallas as pl
from jax.experimental.pallas import tpu as pltpu

D_MODEL = 1024
HEAD_DIM = 64
NSA_HEADS = 8
NSA_GROUPS = 2
NSA_REP = NSA_HEADS // NSA_GROUPS
FOX_HEADS = 4
MOBA_HEADS = 4
ROPE_DIM = HEAD_DIM // 4
ROPE_THETA = 500000.0
CMP_LEN = 32
CMP_STRIDE = 16
CMP_HIDDEN = 4 * HEAD_DIM
SLC_LEN = 64
SLC_TOPN = 8
WIN = 512
MOBA_BLOCK = 256
MOBA_TOPK = 3
D_FF = 2816
EPS = 1e-6
ATT_SCALE = HEAD_DIM ** -0.5

LANES = 128
NEG = -1e30
VMEM_LIMIT = 56 * 1024 * 1024

F32 = jnp.float32
BF16 = jnp.bfloat16
HIGHEST = lax.Precision.HIGHEST

_KV = NSA_GROUPS * HEAD_DIM
_OFF_NSA_Q = 0
_OFF_KC = _OFF_NSA_Q + NSA_HEADS * HEAD_DIM
_OFF_VC = _OFF_KC + _KV
_OFF_KS = _OFF_VC + _KV
_OFF_VS = _OFF_KS + _KV
_OFF_KW = _OFF_VS + _KV
_OFF_VW = _OFF_KW + _KV
_OFF_GATE = _OFF_VW + _KV
_OFF_FOX_Q = _OFF_GATE + 3 * NSA_HEADS
_OFF_FOX_K = _OFF_FOX_Q + FOX_HEADS * HEAD_DIM
_OFF_FOX_V = _OFF_FOX_K + FOX_HEADS * HEAD_DIM
_OFF_FOX_F = _OFF_FOX_V + FOX_HEADS * HEAD_DIM
_OFF_MOBA_Q = _OFF_FOX_F + FOX_HEADS
_OFF_MOBA_K = _OFF_MOBA_Q + MOBA_HEADS * HEAD_DIM
_OFF_MOBA_V = _OFF_MOBA_K + MOBA_HEADS * HEAD_DIM
IN_COLS = _OFF_MOBA_V + MOBA_HEADS * HEAD_DIM

N_HEAD_COLS = 2816
GATE_LANES = 3 * NSA_REP
PROJ_COLS = N_HEAD_COLS + 2 * LANES


def _dot(a, b):
    return jnp.dot(a, b, preferred_element_type=F32)


def _dot_nt(a, b, precision=None):
    return lax.dot_general(a, b, (((1,), (1,)), ((), ())), precision=precision,
                           preferred_element_type=F32)


def _iota(shape, dim):
    return lax.broadcasted_iota(jnp.int32, shape, dim)


def _params(*sem):
    return pltpu.CompilerParams(dimension_semantics=sem, vmem_limit_bytes=VMEM_LIMIT)


def _ada_norm(x, g, shift, scale):
    ms = jnp.mean(x * x, axis=-1, keepdims=True)
    y = x * lax.rsqrt(ms + EPS) * g
    return y * (1.0 + scale) + shift


def _ada_kernel(c_ref, w_ref, b_ref, o_ref):
    c = c_ref[...]
    ca = c * jax.nn.sigmoid(c)
    o_ref[0] = jnp.dot(ca, w_ref[0], precision=HIGHEST, preferred_element_type=F32) + b_ref[0]


def _ada_mod(c, w_ada, b_ada):
    depth, d, n = w_ada.shape
    b = c.shape[0]
    tn = 1152
    return pl.pallas_call(
        _ada_kernel,
        grid=(depth, n // tn),
        in_specs=[pl.BlockSpec((b, d), lambda l, j: (0, 0)),
                  pl.BlockSpec((1, d, tn), lambda l, j: (l, 0, j)),
                  pl.BlockSpec((1, 1, tn), lambda l, j: (l, 0, j))],
        out_specs=pl.BlockSpec((1, b, tn), lambda l, j: (l, 0, j)),
        out_shape=jax.ShapeDtypeStruct((depth, b, n), F32),
        compiler_params=_params("parallel", "parallel"),
        name="ada_mod",
    )(c, w_ada, b_ada.reshape(depth, 1, n))


def _ffn_kernel(x_ref, mod_ref, g_ref, wa_ref, wb_ref, w2_ref, o_ref, h_ref, acc_ref, *, sub, coef):
    j = pl.program_id(1)

    @pl.when(j == 0)
    def _():
        h = _ada_norm(x_ref[...], g_ref[...], mod_ref[0, 3 * sub:3 * sub + 1, :],
                      mod_ref[0, 3 * sub + 1:3 * sub + 2, :])
        h_ref[...] = h.astype(BF16)
        acc_ref[...] = jnp.zeros_like(acc_ref)

    h = h_ref[...]
    a = _dot(h, wa_ref[...])
    b = _dot(h, wb_ref[...])
    u = (a * jax.nn.sigmoid(a) * b).astype(BF16)
    acc_ref[...] += _dot(u, w2_ref[...])

    @pl.when(j == pl.num_programs(1) - 1)
    def _():
        gate = mod_ref[0, 3 * sub + 2:3 * sub + 3, :]
        o_ref[...] = x_ref[...] + (coef * gate) * acc_ref[...]


def _ffn(x2, mod, g, w13, w2, *, sub, seq, tm=1024, tf=256):
    n, d = x2.shape
    nf = D_FF // tf
    per_b = seq // tm
    return pl.pallas_call(
        functools.partial(_ffn_kernel, sub=sub, coef=0.5),
        grid=(n // tm, nf),
        in_specs=[pl.BlockSpec((tm, d), lambda i, j: (i, 0)),
                  pl.BlockSpec((1, 9, d), lambda i, j: (i // per_b, 0, 0)),
                  pl.BlockSpec((1, d), lambda i, j: (0, 0)),
                  pl.BlockSpec((d, tf), lambda i, j: (0, j)),
                  pl.BlockSpec((d, tf), lambda i, j: (0, nf + j)),
                  pl.BlockSpec((tf, d), lambda i, j: (j, 0))],
        out_specs=pl.BlockSpec((tm, d), lambda i, j: (i, 0)),
        out_shape=jax.ShapeDtypeStruct((n, d), F32),
        scratch_shapes=[pltpu.VMEM((tm, d), BF16), pltpu.VMEM((tm, d), F32)],
        compiler_params=_params("parallel", "arbitrary"),
        name="ffn",
    )(x2, mod, g, w13, w13, w2)


def _inproj_kernel(x_ref, mod_ref, g_ref, pos_ref, inv_ref, fb_ref, w_ref,
                   qn_ref, cmp_ref, nkv_ref, fox_ref, moba_ref, gm_ref):
    h = _ada_norm(x_ref[...], g_ref[...], mod_ref[0, 3:4, :], mod_ref[0, 4:5, :]).astype(BF16)
    tm = h.shape[0]

    ang = pos_ref[...].astype(F32) * inv_ref[...]
    cosl = jnp.cos(ang)
    sinl = jnp.sin(ang)
    lane = _iota((tm, LANES), 1) & (HEAD_DIM - 1)
    sin_lo = jnp.where(lane < ROPE_DIM // 2, -sinl, 0.0)
    sin_hi = jnp.where(lane >= ROPE_DIM // 2, sinl, 0.0)

    def rope(y):
        return (y * cosl + pltpu.roll(y, LANES - ROPE_DIM // 2, 1) * sin_lo
                + pltpu.roll(y, ROPE_DIM // 2, 1) * sin_hi)

    def group(c0, rotate, scale, out_ref, head0, dtype):
        y = _dot(h, w_ref[:, c0:c0 + 2 * LANES])
        for half in range(2):
            z = y[:, half * LANES:(half + 1) * LANES]
            if rotate:
                z = rope(z)
            if scale:
                z = z * ATT_SCALE
            z = z.astype(dtype)
            out_ref[0, head0 + 2 * half] = z[:, :HEAD_DIM]
            out_ref[0, head0 + 2 * half + 1] = z[:, HEAD_DIM:]

    group(0, True, True, qn_ref, 0, BF16)
    group(256, True, True, qn_ref, 4, BF16)
    y = _dot(h, w_ref[:, 512:768])
    zk = rope(y[:, :LANES])
    zv = y[:, LANES:]
    cmp_ref[0, 0] = zk[:, :HEAD_DIM]
    cmp_ref[0, 1] = zk[:, HEAD_DIM:]
    cmp_ref[0, 2] = zv[:, :HEAD_DIM]
    cmp_ref[0, 3] = zv[:, HEAD_DIM:]
    for idx, c0 in enumerate((768, 1024)):
        y = _dot(h, w_ref[:, c0:c0 + 256])
        zk = rope(y[:, :LANES]).astype(BF16)
        zv = y[:, LANES:].astype(BF16)
        nkv_ref[0, 4 * idx + 0] = zk[:, :HEAD_DIM]
        nkv_ref[0, 4 * idx + 1] = zk[:, HEAD_DIM:]
        nkv_ref[0, 4 * idx + 2] = zv[:, :HEAD_DIM]
        nkv_ref[0, 4 * idx + 3] = zv[:, HEAD_DIM:]
    group(1280, False, True, fox_ref, 0, BF16)
    group(1536, False, False, fox_ref, 4, BF16)
    group(1792, False, False, fox_ref, 8, BF16)
    group(2048, True, True, moba_ref, 0, BF16)
    group(2304, True, False, moba_ref, 4, BF16)
    group(2560, False, False, moba_ref, 8, BF16)
    y = _dot(h, w_ref[:, N_HEAD_COLS:PROJ_COLS])
    lane = _iota((tm, LANES), 1)
    m0 = y[:, :LANES]
    gm_ref[0, 0] = jnp.where(lane < GATE_LANES, jax.nn.sigmoid(m0),
                             jax.nn.log_sigmoid(m0 + fb_ref[...]))
    gm_ref[0, 1] = jax.nn.sigmoid(y[:, LANES:])


def _inproj(x2, mod, g, pos2, inv_lane, fb_lane, w_p, *, batch, seq, tm=512):
    n, d = x2.shape
    per_b = seq // tm

    def hm(nh, dtype):
        return (pl.BlockSpec((1, nh, tm, HEAD_DIM), lambda i: (i // per_b, 0, i % per_b, 0)),
                jax.ShapeDtypeStruct((batch, nh, seq, HEAD_DIM), dtype))

    specs = [hm(NSA_HEADS, BF16), hm(4, F32), hm(8, BF16), hm(12, BF16), hm(12, BF16),
             (pl.BlockSpec((1, 2, tm, LANES), lambda i: (i // per_b, 0, i % per_b, 0)),
              jax.ShapeDtypeStruct((batch, 2, seq, LANES), F32))]
    return pl.pallas_call(
        _inproj_kernel,
        grid=(n // tm,),
        in_specs=[pl.BlockSpec((tm, d), lambda i: (i, 0)),
                  pl.BlockSpec((1, 9, d), lambda i: (i // per_b, 0, 0)),
                  pl.BlockSpec((1, d), lambda i: (0, 0)),
                  pl.BlockSpec((tm, 1), lambda i: (i, 0)),
                  pl.BlockSpec((1, LANES), lambda i: (0, 0)),
                  pl.BlockSpec((1, LANES), lambda i: (0, 0)),
                  pl.BlockSpec((d, PROJ_COLS), lambda i: (0, 0))],
        out_specs=[s for s, _ in specs],
        out_shape=[s for _, s in specs],
        compiler_params=_params("parallel"),
        name="inproj",
    )(x2, mod, g, pos2, inv_lane, fb_lane, w_p)


def _compress_kernel(z_ref, pos_ref, w1_ref, w2_ref, o_ref):
    z = z_ref[0, 0]
    top = _dot((z + pos_ref[0, 0]).astype(BF16), w1_ref[0, 0])
    bot = _dot((z + pos_ref[0, 1]).astype(BF16), w1_ref[0, 1])
    nch = z.shape[0]
    hid = top + pltpu.roll(bot, nch - 1, 0)
    act = jax.nn.gelu(hid).astype(BF16)
    o_ref[0, 0] = _dot(act, w2_ref[0]).astype(o_ref.dtype)


def _compress(cmp_in, pos_l, w1_l, w2_l):
    b, four, nch, width = cmp_in.shape
    return pl.pallas_call(
        _compress_kernel,
        grid=(b, four),
        in_specs=[pl.BlockSpec((1, 1, nch, width), lambda i, n: (i, n, 0, 0)),
                  pl.BlockSpec((1, 2, 1, width), lambda i, n: (n // 2, 0, 0, 0)),
                  pl.BlockSpec((1, 2, width, CMP_HIDDEN), lambda i, n: (n // 2, 0, 0, 0)),
                  pl.BlockSpec((1, CMP_HIDDEN, HEAD_DIM), lambda i, n: (n // 2, 0, 0))],
        out_specs=pl.BlockSpec((1, 1, nch, HEAD_DIM), lambda i, n: (i, n, 0, 0)),
        out_shape=jax.ShapeDtypeStruct((b, four, nch, HEAD_DIM), BF16),
        compiler_params=_params("parallel", "parallel"),
        name="nsa_compress",
    )(cmp_in, pos_l, w1_l, w2_l)


def _online_update(carry, s, mask, v):
    m, l, acc = carry
    s = jnp.where(mask, s, NEG)
    m_new = jnp.maximum(m, jnp.max(s, axis=-1, keepdims=True))
    alpha = jnp.exp(m - m_new)
    p = jnp.where(mask, jnp.exp(s - m_new), 0.0)
    l = alpha * l + jnp.sum(p, axis=-1, keepdims=True)
    acc = alpha * acc + _dot(p.astype(BF16), v)
    return m_new, l, acc


def _online_init(rows):
    return (jnp.full((rows, 1), NEG, F32), jnp.zeros((rows, 1), F32), jnp.zeros((rows, HEAD_DIM), F32))


def _online_finish(carry):
    _, l, acc = carry
    return acc / jnp.maximum(l, 1e-30)


def _nsa_kernel(q_ref, kc_ref, vc_ref, ks_ref, vs_ref, kw_ref, vw_ref, gate_ref, ov_ref, ex_ref,
                o_ref, selx_ref, *, tq):
    i = pl.program_id(2)
    rep = NSA_REP
    q4 = jnp.concatenate([q_ref[0, r] for r in range(rep)], axis=0)
    t_row = i * tq + _iota((tq, 1), 0)
    t4 = jnp.concatenate([t_row] * rep, axis=0)

    n_c = kc_ref.shape[2]
    s = _dot_nt(q4, kc_ref[0, 0])
    cend = _iota((1, n_c), 1) * CMP_STRIDE + (CMP_LEN - 1)
    cm = cend <= t4
    s = jnp.where(cm, s, NEG)
    e = jnp.where(cm, jnp.exp(s - jnp.max(s, axis=-1, keepdims=True)), 0.0)
    p = e / jnp.maximum(jnp.sum(e, axis=-1, keepdims=True), 1e-30)
    o_cmp = _dot(p.astype(BF16), vc_ref[0, 0])

    psum = p[0:tq] + p[tq:2 * tq] + p[2 * tq:3 * tq] + p[3 * tq:4 * tq]
    imp = jnp.dot(psum, ov_ref[...], precision=HIGHEST, preferred_element_type=F32)
    n_s = ex_ref.shape[1] // SLC_LEN
    j = _iota((tq, LANES), 1)
    tb = lax.shift_right_logical(t_row, SLC_LEN.bit_length() - 1)
    valid = j <= tb
    forced = (j == 0) | (j == tb) | (j == tb - 1)
    val = jnp.where(valid, jnp.where(forced, jnp.inf, imp), -jnp.inf)
    rank = jnp.zeros((tq, LANES), jnp.int32)
    for jp in range(n_s):
        col = val[:, jp:jp + 1]
        ahead = (col > val) | ((col == val) & (j > jp))
        rank = rank + ahead.astype(jnp.int32)
    sel = jnp.where((rank < SLC_TOPN) & (val > -jnp.inf), 1.0, 0.0).astype(BF16)
    selx = _dot(sel, ex_ref[...])
    for r in range(rep):
        selx_ref[r * tq:(r + 1) * tq, :] = selx

    def slc_body(kt, carry):
        k0 = pl.multiple_of(kt * tq, tq)
        kpos = k0 + _iota((1, tq), 1)
        mask4 = (selx_ref[:, pl.ds(k0, tq)] > 0.5) & (kpos <= t4)
        sc = _dot_nt(q4, ks_ref[0, 0, pl.ds(k0, tq), :])
        return _online_update(carry, sc, mask4, vs_ref[0, 0, pl.ds(k0, tq), :])

    o_slc = _online_finish(lax.fori_loop(0, i + 1, slc_body, _online_init(rep * tq)))

    def win_body(kt, carry):
        k0 = pl.multiple_of(kt * tq, tq)
        kpos = k0 + _iota((1, tq), 1)
        dist = t4 - kpos
        mask4 = (dist >= 0) & (dist < WIN)
        sc = _dot_nt(q4, kw_ref[0, 0, pl.ds(k0, tq), :])
        return _online_update(carry, sc, mask4, vw_ref[0, 0, pl.ds(k0, tq), :])

    first = jnp.maximum(i - WIN // tq, 0)
    o_win = _online_finish(lax.fori_loop(first, i + 1, win_body, _online_init(rep * tq)))

    gates = gate_ref[0, 0]
    outs = []
    for r in range(rep):
        rows = slice(r * tq, (r + 1) * tq)
        outs.append(gates[:, 3 * r:3 * r + 1] * o_cmp[rows]
                    + gates[:, 3 * r + 1:3 * r + 2] * o_slc[rows]
                    + gates[:, 3 * r + 2:3 * r + 3] * o_win[rows])
    o_ref[0] = jnp.concatenate(outs, axis=1).astype(o_ref.dtype)


def _nsa(qn, cmp_kv, nkv, gm, ov, ex, *, tq=128):
    b, _, seq, _ = qn.shape
    n_c = cmp_kv.shape[2]
    kv_spec = lambda base: pl.BlockSpec((1, 1, seq, HEAD_DIM), lambda bi, g, i: (bi, base + g, 0, 0))
    cmp_spec = lambda base: pl.BlockSpec((1, 1, n_c, HEAD_DIM), lambda bi, g, i: (bi, base + g, 0, 0))
    return pl.pallas_call(
        functools.partial(_nsa_kernel, tq=tq),
        grid=(b, NSA_GROUPS, seq // tq),
        in_specs=[pl.BlockSpec((1, NSA_REP, tq, HEAD_DIM), lambda bi, g, i: (bi, g, i, 0)),
                  cmp_spec(0), cmp_spec(2),
                  kv_spec(0), kv_spec(2), kv_spec(4), kv_spec(6),
                  pl.BlockSpec((1, 1, tq, LANES), lambda bi, g, i: (bi, g, i, 0)),
                  pl.BlockSpec((n_c, LANES), lambda bi, g, i: (0, 0)),
                  pl.BlockSpec((LANES, seq), lambda bi, g, i: (0, 0))],
        out_specs=pl.BlockSpec((1, tq, NSA_REP * HEAD_DIM), lambda bi, g, i: (bi, i, g)),
        out_shape=jax.ShapeDtypeStruct((b, seq, NSA_HEADS * HEAD_DIM), BF16),
        scratch_shapes=[pltpu.VMEM((NSA_REP * tq, seq), F32)],
        compiler_params=_params("parallel", "parallel", "arbitrary"),
        name="nsa",
    )(qn, cmp_kv, cmp_kv, nkv, nkv, nkv, nkv, gm, ov, ex)


def _decay_kernel(lf_ref, tri_ref, col_ref, row_ref):
    seq = lf_ref.shape[2]
    blk = tri_ref.shape[0]
    carry = jnp.zeros((1, LANES), F32)
    for c in range(seq // blk):
        rows = slice(c * blk, (c + 1) * blk)
        cs = jnp.dot(tri_ref[...], lf_ref[0, 0, rows, :], precision=HIGHEST,
                     preferred_element_type=F32) + carry
        col_ref[0, rows, :] = cs
        carry = cs[blk - 1:blk, :]
    row_ref[0] = col_ref[0].T[8:16, :]


def _decay(gm, tri):
    b, _, seq, _ = gm.shape
    blk = tri.shape[0]
    return pl.pallas_call(
        _decay_kernel,
        grid=(b,),
        in_specs=[pl.BlockSpec((1, 1, seq, LANES), lambda i: (i, 0, 0, 0)),
                  pl.BlockSpec((blk, blk), lambda i: (0, 0))],
        out_specs=[pl.BlockSpec((1, seq, LANES), lambda i: (i, 0, 0)),
                   pl.BlockSpec((1, 8, seq), lambda i: (i, 0, 0))],
        out_shape=[jax.ShapeDtypeStruct((b, seq, LANES), F32),
                   jax.ShapeDtypeStruct((b, 8, seq), F32)],
        compiler_params=_params("parallel"),
        name="fox_decay",
    )(gm, tri)


def _fox_kernel(q_ref, k_ref, v_ref, fcol_ref, frow_ref, o_ref, *, tq):
    i = pl.program_id(1)
    t_row = i * tq + _iota((tq, 1), 0)
    outs = []
    for h in range(FOX_HEADS):
        q = q_ref[0, h]
        fq = fcol_ref[0, :, GATE_LANES + h:GATE_LANES + h + 1]

        def body(kt, carry, h=h, q=q, fq=fq):
            k0 = pl.multiple_of(kt * tq, tq)
            kpos = k0 + _iota((1, tq), 1)
            fk = frow_ref[0, 4 + h:5 + h, pl.ds(k0, tq)]
            sc = _dot_nt(q, k_ref[0, h, pl.ds(k0, tq), :]) + (fq - fk)
            return _online_update(carry, sc, kpos <= t_row, v_ref[0, h, pl.ds(k0, tq), :])

        outs.append(_online_finish(lax.fori_loop(0, i + 1, body, _online_init(tq))))
    o_ref[0] = jnp.concatenate(outs, axis=1).astype(o_ref.dtype)


def _fox(fox, fcol, frow, *, tq=256):
    b, _, seq, _ = fox.shape
    nh = FOX_HEADS
    return pl.pallas_call(
        functools.partial(_fox_kernel, tq=tq),
        grid=(b, seq // tq),
        in_specs=[pl.BlockSpec((1, nh, tq, HEAD_DIM), lambda bi, i: (bi, 0, i, 0)),
                  pl.BlockSpec((1, nh, seq, HEAD_DIM), lambda bi, i: (bi, 1, 0, 0)),
                  pl.BlockSpec((1, nh, seq, HEAD_DIM), lambda bi, i: (bi, 2, 0, 0)),
                  pl.BlockSpec((1, tq, LANES), lambda bi, i: (bi, i, 0)),
                  pl.BlockSpec((1, 8, seq), lambda bi, i: (bi, 0, 0))],
        out_specs=pl.BlockSpec((1, tq, nh * HEAD_DIM), lambda bi, i: (bi, i, 0)),
        out_shape=jax.ShapeDtypeStruct((b, seq, nh * HEAD_DIM), BF16),
        compiler_params=_params("parallel", "arbitrary"),
        name="fox",
    )(fox, fox, fox, fcol, frow)


def _moba_kernel(q_ref, k_ref, v_ref, avg_ref, o_ref):
    tq = MOBA_BLOCK
    i = pl.program_id(1)
    t_row = i * tq + _iota((tq, 1), 0)
    nb = k_ref.shape[2] // MOBA_BLOCK
    lane = _iota((tq, LANES), 1)
    outs = []
    for h in range(MOBA_HEADS):
        q = q_ref[0, h]
        kmean = _dot(avg_ref[...], k_ref[0, h])
        gate = _dot_nt(q.astype(F32), kmean, precision=HIGHEST)
        val = jnp.where(lane < i, gate, -jnp.inf)
        rank = jnp.zeros((tq, LANES), jnp.int32)
        for jp in range(nb):
            col = val[:, jp:jp + 1]
            ahead = (col > val) | ((col == val) & (lane > jp))
            rank = rank + ahead.astype(jnp.int32)
        sel = jnp.where((rank < MOBA_TOPK) & (lane < i), 1.0, 0.0)

        def body(kt, carry, h=h, q=q, sel=sel):
            k0 = pl.multiple_of(kt * tq, tq)
            picked = jnp.sum(jnp.where(lane == kt, sel, 0.0), axis=-1, keepdims=True) > 0.5
            mask = jnp.broadcast_to(picked, (tq, tq))
            sc = _dot_nt(q, k_ref[0, h, pl.ds(k0, tq), :])
            return _online_update(carry, sc, mask, v_ref[0, h, pl.ds(k0, tq), :])

        carry = lax.fori_loop(0, i, body, _online_init(tq))
        k0 = pl.multiple_of(i * tq, tq)
        kpos = k0 + _iota((1, tq), 1)
        sc = _dot_nt(q, k_ref[0, h, pl.ds(k0, tq), :])
        carry = _online_update(carry, sc, kpos <= t_row, v_ref[0, h, pl.ds(k0, tq), :])
        outs.append(_online_finish(carry))
    o_ref[0] = jnp.concatenate(outs, axis=1).astype(o_ref.dtype)


def _moba(moba, avg):
    b, _, seq, _ = moba.shape
    nh = MOBA_HEADS
    tq = MOBA_BLOCK
    return pl.pallas_call(
        _moba_kernel,
        grid=(b, seq // tq),
        in_specs=[pl.BlockSpec((1, nh, tq, HEAD_DIM), lambda bi, i: (bi, 0, i, 0)),
                  pl.BlockSpec((1, nh, seq, HEAD_DIM), lambda bi, i: (bi, 1, 0, 0)),
                  pl.BlockSpec((1, nh, seq, HEAD_DIM), lambda bi, i: (bi, 2, 0, 0)),
                  pl.BlockSpec((LANES, seq), lambda bi, i: (0, 0))],
        out_specs=pl.BlockSpec((1, tq, nh * HEAD_DIM), lambda bi, i: (bi, i, 0)),
        out_shape=jax.ShapeDtypeStruct((b, seq, nh * HEAD_DIM), BF16),
        compiler_params=_params("parallel", "arbitrary"),
        name="moba",
    )(moba, moba, moba, avg)


def _outproj_kernel(x_ref, mod_ref, on_ref, of_ref, om_ref, w_ref, o_ref):
    n_w = NSA_HEADS * HEAD_DIM
    f_w = FOX_HEADS * HEAD_DIM
    acc = _dot(on_ref[...], w_ref[0:n_w, :])
    acc += _dot(of_ref[...], w_ref[n_w:n_w + f_w, :])
    acc += _dot(om_ref[...], w_ref[n_w + f_w:, :])
    o_ref[...] = x_ref[...] + mod_ref[0, 5:6, :] * acc


def _outproj(x2, mod, o_nsa, o_fox, o_moba, w_out, *, seq, tm=512):
    n, d = x2.shape
    per_b = seq // tm
    row = lambda width: pl.BlockSpec((tm, width), lambda i: (i, 0))
    return pl.pallas_call(
        _outproj_kernel,
        grid=(n // tm,),
        in_specs=[row(d), pl.BlockSpec((1, 9, d), lambda i: (i // per_b, 0, 0)),
                  row(o_nsa.shape[1]), row(o_fox.shape[1]), row(o_moba.shape[1]),
                  pl.BlockSpec((d, d), lambda i: (0, 0))],
        out_specs=row(d),
        out_shape=jax.ShapeDtypeStruct((n, d), F32),
        compiler_params=_params("parallel"),
        name="outproj",
    )(x2, mod, o_nsa, o_fox, o_moba, w_out)


def _final_kernel(x_ref, g_ref, o_ref):
    x = x_ref[...]
    ms = jnp.mean(x * x, axis=-1, keepdims=True)
    o_ref[...] = x * lax.rsqrt(ms + EPS) * g_ref[...]


def _final_norm(x2, g, tm=1024):
    n, d = x2.shape
    return pl.pallas_call(
        _final_kernel,
        grid=(n // tm,),
        in_specs=[pl.BlockSpec((tm, d), lambda i: (i, 0)), pl.BlockSpec((1, d), lambda i: (0, 0))],
        out_specs=pl.BlockSpec((tm, d), lambda i: (i, 0)),
        out_shape=jax.ShapeDtypeStruct((n, d), F32),
        compiler_params=_params("parallel"),
        name="final_norm",
    )(x2, g)


def _pack_w_in(w_in):
    depth, d, _ = w_in.shape

    def cols(a, b):
        return w_in[:, :, a:b]

    zeros = lambda n: jnp.zeros((depth, d, n), w_in.dtype)
    g_half = 3 * NSA_REP
    parts = [cols(_OFF_NSA_Q, _OFF_GATE),
             cols(_OFF_FOX_Q, _OFF_FOX_F),
             cols(_OFF_MOBA_Q, IN_COLS),
             cols(_OFF_GATE, _OFF_GATE + g_half), cols(_OFF_FOX_F, _OFF_FOX_F + FOX_HEADS),
             zeros(LANES - g_half - FOX_HEADS),
             cols(_OFF_GATE + g_half, _OFF_GATE + 2 * g_half), zeros(LANES - g_half)]
    return jnp.concatenate(parts, axis=-1).astype(BF16)


def _constants(seq):
    n_c = seq // CMP_STRIDE
    n_s = seq // SLC_LEN
    c0 = np.arange(n_c)[:, None] * CMP_STRIDE
    j0 = np.arange(LANES)[None, :] * SLC_LEN
    real = (np.arange(n_c)[:, None] < (seq - CMP_LEN) // CMP_STRIDE + 1) & (np.arange(LANES)[None, :] < n_s)
    ov = ((c0 < j0 + SLC_LEN) & (c0 + CMP_LEN > j0) & real).astype(np.float32)
    ex = (np.arange(LANES)[:, None] == (np.arange(seq)[None, :] // SLC_LEN)).astype(np.float32)
    avg = (np.arange(LANES)[:, None] == (np.arange(seq)[None, :] // MOBA_BLOCK)).astype(np.float32) / MOBA_BLOCK
    tri = np.tril(np.ones((256, 256), np.float32))
    return (jnp.asarray(ov), jnp.asarray(ex, dtype=BF16), jnp.asarray(avg, dtype=BF16), jnp.asarray(tri))


def kernel(x, c, positions, norm_g, w_ada, b_ada, w_in, fox_fbias, cmp_pos, cmp_w1, cmp_w2, w_out,
           ffn_w13, ffn_w2, final_g):
    batch, seq, d = x.shape
    depth = w_ada.shape[0]
    assert d == D_MODEL and seq % MOBA_BLOCK == 0 and seq % 1024 == 0

    w_in_p = _pack_w_in(w_in)
    w13 = ffn_w13.astype(BF16)
    w2 = ffn_w2.astype(BF16)
    w_o = w_out.astype(BF16)
    half = CMP_LEN * HEAD_DIM // 2
    cw1 = cmp_w1.astype(BF16).reshape(depth, 2, 2, half, CMP_HIDDEN)
    cw2 = cmp_w2.astype(BF16)
    cpos = cmp_pos.reshape(depth, 2, 2, 1, half)
    ov, ex, avg, tri = _constants(seq)

    inv = ROPE_THETA ** (-jnp.arange(0, ROPE_DIM, 2, dtype=F32) / ROPE_DIM)
    lane = np.arange(LANES) % HEAD_DIM
    inv_lane = jnp.where(jnp.asarray(lane < ROPE_DIM), inv[jnp.asarray(lane % (ROPE_DIM // 2))], 0.0)
    inv_lane = inv_lane.reshape(1, LANES).astype(F32)
    fb_lane = jnp.zeros((depth, 1, LANES), F32).at[:, 0, GATE_LANES:GATE_LANES + FOX_HEADS].set(fox_fbias)

    mod_all = _ada_mod(c, w_ada, b_ada).reshape(depth, batch, 9, d)
    pos2 = positions.reshape(batch * seq, 1)
    x2 = x.reshape(batch * seq, d)
    for l in range(depth):
        mod = mod_all[l]
        x2 = _ffn(x2, mod, norm_g[l, 0:1], w13[l, 0], w2[l, 0], sub=0, seq=seq)
        qn, cmp_in, nkv, fox, moba, gm = _inproj(x2, mod, norm_g[l, 1:2], pos2, inv_lane, fb_lane[l],
                                                 w_in_p[l], batch=batch, seq=seq)
        cmp_kv = _compress(cmp_in.reshape(batch, 4, seq // CMP_STRIDE, CMP_STRIDE * HEAD_DIM),
                           cpos[l], cw1[l], cw2[l])
        o_nsa = _nsa(qn, cmp_kv, nkv, gm, ov, ex)
        fcol, frow = _decay(gm, tri)
        o_fox = _fox(fox, fcol, frow)
        o_moba = _moba(moba, avg)
        x2 = _outproj(x2, mod, o_nsa.reshape(batch * seq, -1), o_fox.reshape(batch * seq, -1),
                      o_moba.reshape(batch * seq, -1), w_o[l], seq=seq)
        x2 = _ffn(x2, mod, norm_g[l, 2:3], w13[l, 1], w2[l, 1], sub=2, seq=seq)
    return _final_norm(x2, final_g.reshape(1, d)).reshape(batch, seq, d)
```

```python
import functools

import numpy as np
import jax
import jax.numpy as jnp
from jax import lax
from jax.experimental import pallas as pl
from jax.experimental.pallas import tpu as pltpu

D_MODEL = 1024
HEAD_DIM = 64
NSA_HEADS = 8
NSA_GROUPS = 2
NSA_REP = NSA_HEADS // NSA_GROUPS
FOX_HEADS = 4
MOBA_HEADS = 4
ROPE_DIM = HEAD_DIM // 4
ROPE_THETA = 500000.0
CMP_LEN = 32
CMP_STRIDE = 16
CMP_HIDDEN = 4 * HEAD_DIM
SLC_LEN = 64
SLC_TOPN = 8
WIN = 512
MOBA_BLOCK = 256
MOBA_TOPK = 3
D_FF = 2816
EPS = 1e-6
ATT_SCALE = HEAD_DIM ** -0.5

FOX_TILE = 512
LANES = 128
NEG = -1e30
VMEM_LIMIT = 56 * 1024 * 1024

F32 = jnp.float32
BF16 = jnp.bfloat16
HIGHEST = lax.Precision.HIGHEST

_KV = NSA_GROUPS * HEAD_DIM
_OFF_NSA_Q = 0
_OFF_KC = _OFF_NSA_Q + NSA_HEADS * HEAD_DIM
_OFF_VC = _OFF_KC + _KV
_OFF_KS = _OFF_VC + _KV
_OFF_VS = _OFF_KS + _KV
_OFF_KW = _OFF_VS + _KV
_OFF_VW = _OFF_KW + _KV
_OFF_GATE = _OFF_VW + _KV
_OFF_FOX_Q = _OFF_GATE + 3 * NSA_HEADS
_OFF_FOX_K = _OFF_FOX_Q + FOX_HEADS * HEAD_DIM
_OFF_FOX_V = _OFF_FOX_K + FOX_HEADS * HEAD_DIM
_OFF_FOX_F = _OFF_FOX_V + FOX_HEADS * HEAD_DIM
_OFF_MOBA_Q = _OFF_FOX_F + FOX_HEADS
_OFF_MOBA_K = _OFF_MOBA_Q + MOBA_HEADS * HEAD_DIM
_OFF_MOBA_V = _OFF_MOBA_K + MOBA_HEADS * HEAD_DIM
IN_COLS = _OFF_MOBA_V + MOBA_HEADS * HEAD_DIM

N_HEAD_COLS = 2816
GATE_LANES = 3 * NSA_REP
PROJ_COLS = N_HEAD_COLS + 2 * LANES


def _dot(a, b):
    return jnp.dot(a, b, preferred_element_type=F32)


def _dot_nt(a, b, precision=None):
    return lax.dot_general(a, b, (((1,), (1,)), ((), ())), precision=precision,
                           preferred_element_type=F32)


def _iota(shape, dim):
    return lax.broadcasted_iota(jnp.int32, shape, dim)


def _params(*sem):
    return pltpu.CompilerParams(dimension_semantics=sem, vmem_limit_bytes=VMEM_LIMIT)


def _ada_norm(x, g, shift, scale):
    ms = jnp.mean(x * x, axis=-1, keepdims=True)
    y = x * lax.rsqrt(ms + EPS) * g
    return y * (1.0 + scale) + shift


def _ada_kernel(c_ref, w_ref, b_ref, o_ref):
    c = c_ref[...]
    ca = c * jax.nn.sigmoid(c)
    o_ref[0] = jnp.dot(ca, w_ref[0], precision=HIGHEST, preferred_element_type=F32) + b_ref[0]


def _ada_mod(c, w_ada, b_ada):
    depth, d, n = w_ada.shape
    b = c.shape[0]
    tn = 1152
    return pl.pallas_call(
        _ada_kernel,
        grid=(depth, n // tn),
        in_specs=[pl.BlockSpec((b, d), lambda l, j: (0, 0)),
                  pl.BlockSpec((1, d, tn), lambda l, j: (l, 0, j)),
                  pl.BlockSpec((1, 1, tn), lambda l, j: (l, 0, j))],
        out_specs=pl.BlockSpec((1, b, tn), lambda l, j: (l, 0, j)),
        out_shape=jax.ShapeDtypeStruct((depth, b, n), F32),
        compiler_params=_params("parallel", "parallel"),
        name="ada_mod",
    )(c, w_ada, b_ada.reshape(depth, 1, n))


def _ffn_kernel(x_ref, mod_ref, g_ref, wa_ref, wb_ref, w2_ref, o_ref, h_ref, acc_ref, *, sub, coef):
    j = pl.program_id(1)

    @pl.when(j == 0)
    def _():
        h = _ada_norm(x_ref[...], g_ref[...], mod_ref[0, 3 * sub:3 * sub + 1, :],
                      mod_ref[0, 3 * sub + 1:3 * sub + 2, :])
        h_ref[...] = h.astype(BF16)
        acc_ref[...] = jnp.zeros_like(acc_ref)

    h = h_ref[...]
    a = _dot(h, wa_ref[...])
    b = _dot(h, wb_ref[...])
    u = (a * jax.nn.sigmoid(a) * b).astype(BF16)
    acc_ref[...] += _dot(u, w2_ref[...])

    @pl.when(j == pl.num_programs(1) - 1)
    def _():
        gate = mod_ref[0, 3 * sub + 2:3 * sub + 3, :]
        o_ref[...] = x_ref[...] + (coef * gate) * acc_ref[...]


def _ffn(x2, mod, g, w13, w2, *, sub, seq, tm=1024, tf=256):
    n, d = x2.shape
    nf = D_FF // tf
    per_b = seq // tm
    return pl.pallas_call(
        functools.partial(_ffn_kernel, sub=sub, coef=0.5),
        grid=(n // tm, nf),
        in_specs=[pl.BlockSpec((tm, d), lambda i, j: (i, 0)),
                  pl.BlockSpec((1, 9, d), lambda i, j: (i // per_b, 0, 0)),
                  pl.BlockSpec((1, d), lambda i, j: (0, 0)),
                  pl.BlockSpec((d, tf), lambda i, j: (0, j)),
                  pl.BlockSpec((d, tf), lambda i, j: (0, nf + j)),
                  pl.BlockSpec((tf, d), lambda i, j: (j, 0))],
        out_specs=pl.BlockSpec((tm, d), lambda i, j: (i, 0)),
        out_shape=jax.ShapeDtypeStruct((n, d), F32),
        scratch_shapes=[pltpu.VMEM((tm, d), BF16), pltpu.VMEM((tm, d), F32)],
        compiler_params=_params("parallel", "arbitrary"),
        name="ffn",
    )(x2, mod, g, w13, w13, w2)


def _inproj_kernel(x_ref, mod_ref, g_ref, pos_ref, inv_ref, fb_ref, w_ref,
                   qn_ref, cmp_ref, nk_ref, nv_ref, fqk_ref, fv_ref, mqk_ref, mv_ref, gm_ref):
    h = _ada_norm(x_ref[...], g_ref[...], mod_ref[0, 3:4, :], mod_ref[0, 4:5, :]).astype(BF16)
    tm = h.shape[0]

    ang = pos_ref[...].astype(F32) * inv_ref[...]
    cosl = jnp.cos(ang)
    sinl = jnp.sin(ang)
    lane = _iota((tm, LANES), 1)
    hl = lane & (HEAD_DIM - 1)
    sin_lo = jnp.where(hl < ROPE_DIM // 2, -sinl, 0.0)
    sin_hi = jnp.where(hl >= ROPE_DIM // 2, sinl, 0.0)
    one_hot = jnp.where(lane == HEAD_DIM, 1.0, 0.0)

    def rope(y):
        return (y * cosl + pltpu.roll(y, LANES - ROPE_DIM // 2, 1) * sin_lo
                + pltpu.roll(y, ROPE_DIM // 2, 1) * sin_hi)

    def put_heads(z, out_ref, head0, dtype):
        z = z.astype(dtype)
        out_ref[0, head0] = z[:, :HEAD_DIM]
        out_ref[0, head0 + 1] = z[:, HEAD_DIM:]

    def put_values(z, out_ref, head0):
        out_ref[0, head0] = jnp.where(lane < HEAD_DIM, z, one_hot).astype(BF16)
        out_ref[0, head0 + 1] = jnp.where(lane < HEAD_DIM, pltpu.roll(z, HEAD_DIM, 1), one_hot).astype(BF16)

    def slab(c0):
        y = _dot(h, w_ref[:, c0:c0 + 2 * LANES])
        return y[:, :LANES], y[:, LANES:]

    for s_idx in range(2):
        a, b = slab(256 * s_idx)
        put_heads(rope(a) * ATT_SCALE, qn_ref, 4 * s_idx, BF16)
        put_heads(rope(b) * ATT_SCALE, qn_ref, 4 * s_idx + 2, BF16)
    a, b = slab(512)
    put_heads(rope(a), cmp_ref, 0, F32)
    put_heads(b, cmp_ref, 2, F32)
    for idx, c0 in enumerate((768, 1024)):
        a, b = slab(c0)
        put_heads(rope(a), nk_ref, 2 * idx, BF16)
        put_values(b, nv_ref, 2 * idx)
    a, b = slab(1280)
    put_heads(a * ATT_SCALE, fqk_ref, 0, BF16)
    put_heads(b * ATT_SCALE, fqk_ref, 2, BF16)
    a, b = slab(1536)
    put_heads(a, fqk_ref, 4, BF16)
    put_heads(b, fqk_ref, 6, BF16)
    a, b = slab(1792)
    put_values(a, fv_ref, 0)
    put_values(b, fv_ref, 2)
    a, b = slab(2048)
    put_heads(rope(a) * ATT_SCALE, mqk_ref, 0, BF16)
    put_heads(rope(b) * ATT_SCALE, mqk_ref, 2, BF16)
    a, b = slab(2304)
    put_heads(rope(a), mqk_ref, 4, BF16)
    put_heads(rope(b), mqk_ref, 6, BF16)
    a, b = slab(2560)
    put_values(a, mv_ref, 0)
    put_values(b, mv_ref, 2)
    a, b = slab(N_HEAD_COLS)
    gm_ref[0, 0] = jnp.where(lane < GATE_LANES, jax.nn.sigmoid(a), jax.nn.log_sigmoid(a + fb_ref[...]))
    gm_ref[0, 1] = jax.nn.sigmoid(b)


def _inproj(x2, mod, g, pos2, inv_lane, fb_lane, w_p, *, batch, seq, tm=512):
    n, d = x2.shape
    per_b = seq // tm

    def hm(nh, dtype, width=HEAD_DIM):
        return (pl.BlockSpec((1, nh, tm, width), lambda i: (i // per_b, 0, i % per_b, 0)),
                jax.ShapeDtypeStruct((batch, nh, seq, width), dtype))

    specs = [hm(NSA_HEADS, BF16), hm(4, F32), hm(4, BF16), hm(4, BF16, LANES),
             hm(8, BF16), hm(4, BF16, LANES), hm(8, BF16), hm(4, BF16, LANES), hm(2, F32, LANES)]
    return pl.pallas_call(
        _inproj_kernel,
        grid=(n // tm,),
        in_specs=[pl.BlockSpec((tm, d), lambda i: (i, 0)),
                  pl.BlockSpec((1, 9, d), lambda i: (i // per_b, 0, 0)),
                  pl.BlockSpec((1, d), lambda i: (0, 0)),
                  pl.BlockSpec((tm, 1), lambda i: (i, 0)),
                  pl.BlockSpec((1, LANES), lambda i: (0, 0)),
                  pl.BlockSpec((1, LANES), lambda i: (0, 0)),
                  pl.BlockSpec((d, PROJ_COLS), lambda i: (0, 0))],
        out_specs=[s for s, _ in specs],
        out_shape=[s for _, s in specs],
        compiler_params=_params("parallel"),
        name="inproj",
    )(x2, mod, g, pos2, inv_lane, fb_lane, w_p)


def _compress_kernel(z_ref, pos_ref, w1_ref, w2_ref, o_ref):
    z = z_ref[0, 0]
    top = _dot((z + pos_ref[0, 0]).astype(BF16), w1_ref[0, 0])
    bot = _dot((z + pos_ref[0, 1]).astype(BF16), w1_ref[0, 1])
    nch = z.shape[0]
    hid = top + pltpu.roll(bot, nch - 1, 0)
    act = jax.nn.gelu(hid).astype(BF16)
    o_ref[0, 0] = _dot(act, w2_ref[0]).astype(o_ref.dtype)


def _compress(cmp_in, pos_l, w1_l, w2_l):
    b, four, nch, width = cmp_in.shape
    return pl.pallas_call(
        _compress_kernel,
        grid=(b, four),
        in_specs=[pl.BlockSpec((1, 1, nch, width), lambda i, n: (i, n, 0, 0)),
                  pl.BlockSpec((1, 2, 1, width), lambda i, n: (n // 2, 0, 0, 0)),
                  pl.BlockSpec((1, 2, width, CMP_HIDDEN), lambda i, n: (n // 2, 0, 0, 0)),
                  pl.BlockSpec((1, CMP_HIDDEN, HEAD_DIM), lambda i, n: (n // 2, 0, 0))],
        out_specs=pl.BlockSpec((1, 1, nch, HEAD_DIM), lambda i, n: (i, n, 0, 0)),
        out_shape=jax.ShapeDtypeStruct((b, four, nch, HEAD_DIM), BF16),
        compiler_params=_params("parallel", "parallel"),
        name="nsa_compress",
    )(cmp_in, pos_l, w1_l, w2_l)


def _chunk_update(carry, s, v_aug):
    m, acc = carry
    m_new = jnp.maximum(m, jnp.max(s, axis=-1, keepdims=True))
    p = jnp.exp(s - m_new).astype(BF16)
    acc = jnp.exp(m - m_new) * acc + _dot(p, v_aug)
    return m_new, acc


def _chunk_init(rows):
    return jnp.full((rows, 1), NEG, F32), jnp.zeros((rows, LANES), F32)


def _chunk_finish(carry):
    _, acc = carry
    return acc[:, :HEAD_DIM] / jnp.maximum(acc[:, HEAD_DIM:HEAD_DIM + 1], 1e-30)


def _add_rows(s, bias, rep):
    tq = bias.shape[0]
    return jnp.concatenate([s[r * tq:(r + 1) * tq] + bias for r in range(rep)], axis=0)


def _nsa_kernel(q_ref, kc_ref, vc_ref, ks_ref, vs_ref, kw_ref, vw_ref, gate_ref, ov_ref, ex_ref,
                o_ref, bias_ref, *, tq, tk):
    i = pl.program_id(2)
    rep = NSA_REP
    q4 = jnp.concatenate([q_ref[0, r] for r in range(rep)], axis=0)
    t_row = i * tq + _iota((tq, 1), 0)
    t4 = jnp.concatenate([t_row] * rep, axis=0)

    n_c = kc_ref.shape[2]
    s = _dot_nt(q4, kc_ref[0, 0])
    cend = _iota((1, n_c), 1) * CMP_STRIDE + (CMP_LEN - 1)
    cm = cend <= t4
    s = jnp.where(cm, s, NEG)
    e = jnp.where(cm, jnp.exp(s - jnp.max(s, axis=-1, keepdims=True)), 0.0)
    p = e / jnp.maximum(jnp.sum(e, axis=-1, keepdims=True), 1e-30)
    o_cmp = _dot(p.astype(BF16), vc_ref[0, 0])

    psum = p[0:tq] + p[tq:2 * tq] + p[2 * tq:3 * tq] + p[3 * tq:4 * tq]
    imp = jnp.dot(psum, ov_ref[...], precision=HIGHEST, preferred_element_type=F32)
    n_s = ex_ref.shape[1] // SLC_LEN
    j = _iota((tq, LANES), 1)
    tb = lax.shift_right_logical(t_row, SLC_LEN.bit_length() - 1)
    valid = j <= tb
    forced = (j == 0) | (j == tb) | (j == tb - 1)
    val = jnp.where(valid, jnp.where(forced, jnp.inf, imp), -jnp.inf)
    v_t = val.T[:n_s]
    j_t = _iota((n_s, tq), 0)
    rank = jnp.zeros((n_s, tq), jnp.int32)
    for jp in range(n_s):
        row = v_t[jp:jp + 1, :]
        ahead = (row > v_t) | ((row == v_t) & (j_t > jp))
        rank = rank + ahead.astype(jnp.int32)
    neg_t = jnp.where((rank < SLC_TOPN) & (v_t > -jnp.inf), 0.0, NEG)
    neg = jnp.concatenate([neg_t, jnp.zeros((LANES - n_s, tq), F32)], axis=0).T
    bias_ref[...] = _dot(neg.astype(BF16), ex_ref[...])
    n_chunks = i // (tk // tq) + 1
    k_last = pl.multiple_of((n_chunks - 1) * tk, tk)
    kpos = k_last + _iota((1, tk), 1)
    bias_ref[:, pl.ds(k_last, tk)] += jnp.where(kpos <= t_row, 0.0, NEG)

    def slc_body(c, carry):
        k0 = pl.multiple_of(c * tk, tk)
        sc = _dot_nt(q4, ks_ref[0, 0, pl.ds(k0, tk), :])
        sc = _add_rows(sc, bias_ref[:, pl.ds(k0, tk)], rep)
        return _chunk_update(carry, sc, vs_ref[0, 0, pl.ds(k0, tk), :])

    o_slc = _chunk_finish(lax.fori_loop(0, n_chunks, slc_body, _chunk_init(rep * tq)))

    span = WIN + tq
    start = pl.multiple_of(jnp.maximum(i * tq - WIN, 0), tq)
    dist = t_row - (start + _iota((1, span), 1))
    band = jnp.where((dist >= 0) & (dist < WIN), 0.0, NEG)
    sw = _add_rows(_dot_nt(q4, kw_ref[0, 0, pl.ds(start, span), :]), band, rep)
    pw = jnp.exp(sw - jnp.max(sw, axis=-1, keepdims=True)).astype(BF16)
    aw = _dot(pw, vw_ref[0, 0, pl.ds(start, span), :])
    o_win = aw[:, :HEAD_DIM] / jnp.maximum(aw[:, HEAD_DIM:HEAD_DIM + 1], 1e-30)

    gates = gate_ref[0, 0]
    outs = []
    for r in range(rep):
        rows = slice(r * tq, (r + 1) * tq)
        outs.append(gates[:, 3 * r:3 * r + 1] * o_cmp[rows]
                    + gates[:, 3 * r + 1:3 * r + 2] * o_slc[rows]
                    + gates[:, 3 * r + 2:3 * r + 3] * o_win[rows])
    o_ref[0] = jnp.concatenate(outs, axis=1).astype(o_ref.dtype)


def _nsa(qn, cmp_kv, nk, nv, gm, ov, ex, *, tq=128, tk=512):
    b, _, seq, _ = qn.shape
    n_c = cmp_kv.shape[2]
    k_spec = lambda base: pl.BlockSpec((1, 1, seq, HEAD_DIM), lambda bi, g, i: (bi, base + g, 0, 0))
    v_spec = lambda base: pl.BlockSpec((1, 1, seq, LANES), lambda bi, g, i: (bi, base + g, 0, 0))
    cmp_spec = lambda base: pl.BlockSpec((1, 1, n_c, HEAD_DIM), lambda bi, g, i: (bi, base + g, 0, 0))
    return pl.pallas_call(
        functools.partial(_nsa_kernel, tq=tq, tk=tk),
        grid=(b, NSA_GROUPS, seq // tq),
        in_specs=[pl.BlockSpec((1, NSA_REP, tq, HEAD_DIM), lambda bi, g, i: (bi, g, i, 0)),
                  cmp_spec(0), cmp_spec(2),
                  k_spec(0), v_spec(0), k_spec(2), v_spec(2),
                  pl.BlockSpec((1, 1, tq, LANES), lambda bi, g, i: (bi, g, i, 0)),
                  pl.BlockSpec((n_c, LANES), lambda bi, g, i: (0, 0)),
                  pl.BlockSpec((LANES, seq), lambda bi, g, i: (0, 0))],
        out_specs=pl.BlockSpec((1, tq, NSA_REP * HEAD_DIM), lambda bi, g, i: (bi, i, g)),
        out_shape=jax.ShapeDtypeStruct((b, seq, NSA_HEADS * HEAD_DIM), BF16),
        scratch_shapes=[pltpu.VMEM((tq, seq), F32)],
        compiler_params=_params("parallel", "parallel", "arbitrary"),
        name="nsa",
    )(qn, cmp_kv, cmp_kv, nk, nv, nk, nv, gm, ov, ex)


def _decay_kernel(lf_ref, tri_ref, col_ref, row_ref):
    seq = lf_ref.shape[2]
    blk = tri_ref.shape[0]
    carry = jnp.zeros((1, LANES), F32)
    for c in range(seq // blk):
        rows = slice(c * blk, (c + 1) * blk)
        cs = jnp.dot(tri_ref[...], lf_ref[0, 0, rows, :], precision=HIGHEST,
                     preferred_element_type=F32) + carry
        col_ref[0, rows, :] = cs
        carry = cs[blk - 1:blk, :]
    row_ref[0] = col_ref[0].T[8:16, :]


def _decay(gm, tri):
    b, _, seq, _ = gm.shape
    blk = tri.shape[0]
    return pl.pallas_call(
        _decay_kernel,
        grid=(b,),
        in_specs=[pl.BlockSpec((1, 1, seq, LANES), lambda i: (i, 0, 0, 0)),
                  pl.BlockSpec((blk, blk), lambda i: (0, 0))],
        out_specs=[pl.BlockSpec((1, seq, LANES), lambda i: (i, 0, 0)),
                   pl.BlockSpec((1, 8, seq), lambda i: (i, 0, 0))],
        out_shape=[jax.ShapeDtypeStruct((b, seq, LANES), F32),
                   jax.ShapeDtypeStruct((b, 8, seq), F32)],
        compiler_params=_params("parallel"),
        name="fox_decay",
    )(gm, tri)


def _fox_kernel(q_ref, k_ref, v_ref, fcol_ref, frow_ref, tri_ref, o_ref, *, tq):
    i = pl.program_id(1)
    outs = []
    for h in range(FOX_HEADS):
        q = q_ref[0, h]
        fq = fcol_ref[0, :, GATE_LANES + h:GATE_LANES + h + 1]

        def scores(c, h=h, q=q, fq=fq):
            k0 = pl.multiple_of(c * tq, tq)
            fk = frow_ref[0, 4 + h:5 + h, pl.ds(k0, tq)]
            sc = _dot_nt(q, k_ref[0, h, pl.ds(k0, tq), :]) + (fq - fk)
            return sc, v_ref[0, h, pl.ds(k0, tq), :]

        def body(c, carry, scores=scores):
            sc, v = scores(c)
            return _chunk_update(carry, sc, v)

        carry = lax.fori_loop(0, i, body, _chunk_init(tq))
        sc, v = scores(i)
        outs.append(_chunk_finish(_chunk_update(carry, sc + tri_ref[...], v)))
    o_ref[0] = jnp.concatenate(outs, axis=1).astype(o_ref.dtype)


def _fox(fqk, fv, fcol, frow, tri, *, tq=FOX_TILE):
    b, _, seq, _ = fqk.shape
    nh = FOX_HEADS
    return pl.pallas_call(
        functools.partial(_fox_kernel, tq=tq),
        grid=(b, seq // tq),
        in_specs=[pl.BlockSpec((1, nh, tq, HEAD_DIM), lambda bi, i: (bi, 0, i, 0)),
                  pl.BlockSpec((1, nh, seq, HEAD_DIM), lambda bi, i: (bi, 1, 0, 0)),
                  pl.BlockSpec((1, nh, seq, LANES), lambda bi, i: (bi, 0, 0, 0)),
                  pl.BlockSpec((1, tq, LANES), lambda bi, i: (bi, i, 0)),
                  pl.BlockSpec((1, 8, seq), lambda bi, i: (bi, 0, 0)),
                  pl.BlockSpec((tq, tq), lambda bi, i: (0, 0))],
        out_specs=pl.BlockSpec((1, tq, nh * HEAD_DIM), lambda bi, i: (bi, i, 0)),
        out_shape=jax.ShapeDtypeStruct((b, seq, nh * HEAD_DIM), BF16),
        compiler_params=_params("parallel", "arbitrary"),
        name="fox",
    )(fqk, fqk, fv, fcol, frow, tri)


def _moba_kernel(q_ref, k_ref, v_ref, avg_ref, eb_ref, tri_ref, o_ref, bias_ref, *, tk):
    tq = MOBA_BLOCK
    i = pl.program_id(1)
    nb = k_ref.shape[2] // MOBA_BLOCK
    lane = _iota((tq, LANES), 1)
    own = pl.multiple_of(i * tq, tq)
    outs = []
    for h in range(MOBA_HEADS):
        q = q_ref[0, h]
        kmean = _dot(avg_ref[...], k_ref[0, h])
        gate = _dot_nt(q.astype(F32), kmean, precision=HIGHEST)
        val = jnp.where(lane < i, gate, -jnp.inf)
        rank = jnp.zeros((tq, LANES), jnp.int32)
        for jp in range(nb):
            col = val[:, jp:jp + 1]
            ahead = (col > val) | ((col == val) & (lane > jp))
            rank = rank + ahead.astype(jnp.int32)
        keep = ((rank < MOBA_TOPK) & (lane < i)) | (lane == i)
        neg = jnp.where(keep, 0.0, NEG).astype(BF16)
        bias_ref[...] = _dot(neg, eb_ref[...])
        bias_ref[:, pl.ds(own, tq)] = tri_ref[...]

        def body(c, carry, h=h, q=q):
            k0 = pl.multiple_of(c * tk, tk)
            sc = _dot_nt(q, k_ref[0, h, pl.ds(k0, tk), :]) + bias_ref[:, pl.ds(k0, tk)]
            return _chunk_update(carry, sc, v_ref[0, h, pl.ds(k0, tk), :])

        n_chunks = i // (tk // tq) + 1
        outs.append(_chunk_finish(lax.fori_loop(0, n_chunks, body, _chunk_init(tq))))
    o_ref[0] = jnp.concatenate(outs, axis=1).astype(o_ref.dtype)


def _moba(mqk, mv, avg, eb, tri, *, tk=512):
    b, _, seq, _ = mqk.shape
    nh = MOBA_HEADS
    tq = MOBA_BLOCK
    return pl.pallas_call(
        functools.partial(_moba_kernel, tk=tk),
        grid=(b, seq // tq),
        in_specs=[pl.BlockSpec((1, nh, tq, HEAD_DIM), lambda bi, i: (bi, 0, i, 0)),
                  pl.BlockSpec((1, nh, seq, HEAD_DIM), lambda bi, i: (bi, 1, 0, 0)),
                  pl.BlockSpec((1, nh, seq, LANES), lambda bi, i: (bi, 0, 0, 0)),
                  pl.BlockSpec((LANES, seq), lambda bi, i: (0, 0)),
                  pl.BlockSpec((LANES, seq), lambda bi, i: (0, 0)),
                  pl.BlockSpec((tq, tq), lambda bi, i: (0, 0))],
        out_specs=pl.BlockSpec((1, tq, nh * HEAD_DIM), lambda bi, i: (bi, i, 0)),
        out_shape=jax.ShapeDtypeStruct((b, seq, nh * HEAD_DIM), BF16),
        scratch_shapes=[pltpu.VMEM((tq, seq), F32)],
        compiler_params=_params("parallel", "arbitrary"),
        name="moba",
    )(mqk, mqk, mv, avg, eb, tri)


def _outproj_kernel(x_ref, mod_ref, on_ref, of_ref, om_ref, w_ref, o_ref):
    n_w = NSA_HEADS * HEAD_DIM
    f_w = FOX_HEADS * HEAD_DIM
    acc = _dot(on_ref[...], w_ref[0:n_w, :])
    acc += _dot(of_ref[...], w_ref[n_w:n_w + f_w, :])
    acc += _dot(om_ref[...], w_ref[n_w + f_w:, :])
    o_ref[...] = x_ref[...] + mod_ref[0, 5:6, :] * acc


def _outproj(x2, mod, o_nsa, o_fox, o_moba, w_out, *, seq, tm=512):
    n, d = x2.shape
    per_b = seq // tm
    row = lambda width: pl.BlockSpec((tm, width), lambda i: (i, 0))
    return pl.pallas_call(
        _outproj_kernel,
        grid=(n // tm,),
        in_specs=[row(d), pl.BlockSpec((1, 9, d), lambda i: (i // per_b, 0, 0)),
                  row(o_nsa.shape[1]), row(o_fox.shape[1]), row(o_moba.shape[1]),
                  pl.BlockSpec((d, d), lambda i: (0, 0))],
        out_specs=row(d),
        out_shape=jax.ShapeDtypeStruct((n, d), F32),
        compiler_params=_params("parallel"),
        name="outproj",
    )(x2, mod, o_nsa, o_fox, o_moba, w_out)


def _final_kernel(x_ref, g_ref, o_ref):
    x = x_ref[...]
    ms = jnp.mean(x * x, axis=-1, keepdims=True)
    o_ref[...] = x * lax.rsqrt(ms + EPS) * g_ref[...]


def _final_norm(x2, g, tm=1024):
    n, d = x2.shape
    return pl.pallas_call(
        _final_kernel,
        grid=(n // tm,),
        in_specs=[pl.BlockSpec((tm, d), lambda i: (i, 0)), pl.BlockSpec((1, d), lambda i: (0, 0))],
        out_specs=pl.BlockSpec((tm, d), lambda i: (i, 0)),
        out_shape=jax.ShapeDtypeStruct((n, d), F32),
        compiler_params=_params("parallel"),
        name="final_norm",
    )(x2, g)


def _pack_w_in(w_in):
    depth, d, _ = w_in.shape

    def cols(a, b):
        return w_in[:, :, a:b]

    zeros = lambda n: jnp.zeros((depth, d, n), w_in.dtype)
    g_half = 3 * NSA_REP
    parts = [cols(_OFF_NSA_Q, _OFF_GATE),
             cols(_OFF_FOX_Q, _OFF_FOX_F),
             cols(_OFF_MOBA_Q, IN_COLS),
             cols(_OFF_GATE, _OFF_GATE + g_half), cols(_OFF_FOX_F, _OFF_FOX_F + FOX_HEADS),
             zeros(LANES - g_half - FOX_HEADS),
             cols(_OFF_GATE + g_half, _OFF_GATE + 2 * g_half), zeros(LANES - g_half)]
    return jnp.concatenate(parts, axis=-1).astype(BF16)


def _constants(seq):
    n_c = seq // CMP_STRIDE
    n_s = seq // SLC_LEN
    c0 = np.arange(n_c)[:, None] * CMP_STRIDE
    j0 = np.arange(LANES)[None, :] * SLC_LEN
    real = (np.arange(n_c)[:, None] < (seq - CMP_LEN) // CMP_STRIDE + 1) & (np.arange(LANES)[None, :] < n_s)
    ov = ((c0 < j0 + SLC_LEN) & (c0 + CMP_LEN > j0) & real).astype(np.float32)
    ex = (np.arange(LANES)[:, None] == (np.arange(seq)[None, :] // SLC_LEN)).astype(np.float32)
    eb = (np.arange(LANES)[:, None] == (np.arange(seq)[None, :] // MOBA_BLOCK)).astype(np.float32)
    avg = eb / MOBA_BLOCK
    tri = np.tril(np.ones((256, 256), np.float32))

    def causal_bias(n):
        return np.where(np.arange(n)[None, :] <= np.arange(n)[:, None], 0.0, NEG).astype(np.float32)

    return dict(ov=jnp.asarray(ov), ex=jnp.asarray(ex, dtype=BF16), avg=jnp.asarray(avg, dtype=BF16),
                eb=jnp.asarray(eb, dtype=BF16), tri=jnp.asarray(tri),
                causal_fox=jnp.asarray(causal_bias(FOX_TILE)), causal_moba=jnp.asarray(causal_bias(MOBA_BLOCK)))


def kernel(x, c, positions, norm_g, w_ada, b_ada, w_in, fox_fbias, cmp_pos, cmp_w1, cmp_w2, w_out,
           ffn_w13, ffn_w2, final_g):
    batch, seq, d = x.shape
    depth = w_ada.shape[0]
    assert d == D_MODEL and seq % MOBA_BLOCK == 0 and seq % 1024 == 0

    w_in_p = _pack_w_in(w_in)
    w13 = ffn_w13.astype(BF16)
    w2 = ffn_w2.astype(BF16)
    w_o = w_out.astype(BF16)
    half = CMP_LEN * HEAD_DIM // 2
    cw1 = cmp_w1.astype(BF16).reshape(depth, 2, 2, half, CMP_HIDDEN)
    cw2 = cmp_w2.astype(BF16)
    cpos = cmp_pos.reshape(depth, 2, 2, 1, half)
    cst = _constants(seq)

    inv = ROPE_THETA ** (-jnp.arange(0, ROPE_DIM, 2, dtype=F32) / ROPE_DIM)
    lane = np.arange(LANES) % HEAD_DIM
    inv_lane = jnp.where(jnp.asarray(lane < ROPE_DIM), inv[jnp.asarray(lane % (ROPE_DIM // 2))], 0.0)
    inv_lane = inv_lane.reshape(1, LANES).astype(F32)
    fb_lane = jnp.zeros((depth, 1, LANES), F32).at[:, 0, GATE_LANES:GATE_LANES + FOX_HEADS].set(fox_fbias)

    mod_all = _ada_mod(c, w_ada, b_ada).reshape(depth, batch, 9, d)
    pos2 = positions.reshape(batch * seq, 1)
    x2 = x.reshape(batch * seq, d)
    for l in range(depth):
        mod = mod_all[l]
        x2 = _ffn(x2, mod, norm_g[l, 0:1], w13[l, 0], w2[l, 0], sub=0, seq=seq)
        qn, cmp_in, nk, nv, fqk, fv, mqk, mv, gm = _inproj(
            x2, mod, norm_g[l, 1:2], pos2, inv_lane, fb_lane[l], w_in_p[l], batch=batch, seq=seq)
        cmp_kv = _compress(cmp_in.reshape(batch, 4, seq // CMP_STRIDE, CMP_STRIDE * HEAD_DIM),
                           cpos[l], cw1[l], cw2[l])
        o_nsa = _nsa(qn, cmp_kv, nk, nv, gm, cst["ov"], cst["ex"])
        fcol, frow = _decay(gm, cst["tri"])
        o_fox = _fox(fqk, fv, fcol, frow, cst["causal_fox"])
        o_moba = _moba(mqk, mv, cst["avg"], cst["eb"], cst["causal_moba"])
        x2 = _outproj(x2, mod, o_nsa.reshape(batch * seq, -1), o_fox.reshape(batch * seq, -1),
                      o_moba.reshape(batch * seq, -1), w_o[l], seq=seq)
        x2 = _ffn(x2, mod, norm_g[l, 2:3], w13[l, 1], w2[l, 1], sub=2, seq=seq)
    return _final_norm(x2, final_g.reshape(1, d)).reshape(batch, seq, d)
```

```python
import functools

import numpy as np
import jax
import jax.numpy as jnp
from jax import lax
from jax.experimental import pallas as pl
from jax.experimental.pallas import tpu as pltpu

D_MODEL = 1024
HEAD_DIM = 64
NSA_HEADS = 8
NSA_GROUPS = 2
NSA_REP = NSA_HEADS // NSA_GROUPS
FOX_HEADS = 4
MOBA_HEADS = 4
ROPE_DIM = HEAD_DIM // 4
ROPE_THETA = 500000.0
CMP_LEN = 32
CMP_STRIDE = 16
CMP_HIDDEN = 4 * HEAD_DIM
SLC_LEN = 64
SLC_TOPN = 8
WIN = 512
MOBA_BLOCK = 256
MOBA_TOPK = 3
D_FF = 2816
EPS = 1e-6
ATT_SCALE = HEAD_DIM ** -0.5

FOX_TILE = 512
LANES = 128
NEG = -1e30
VMEM_LIMIT = 56 * 1024 * 1024

F32 = jnp.float32
BF16 = jnp.bfloat16
HIGHEST = lax.Precision.HIGHEST

_KV = NSA_GROUPS * HEAD_DIM
_OFF_NSA_Q = 0
_OFF_KC = _OFF_NSA_Q + NSA_HEADS * HEAD_DIM
_OFF_VC = _OFF_KC + _KV
_OFF_KS = _OFF_VC + _KV
_OFF_VS = _OFF_KS + _KV
_OFF_KW = _OFF_VS + _KV
_OFF_VW = _OFF_KW + _KV
_OFF_GATE = _OFF_VW + _KV
_OFF_FOX_Q = _OFF_GATE + 3 * NSA_HEADS
_OFF_FOX_K = _OFF_FOX_Q + FOX_HEADS * HEAD_DIM
_OFF_FOX_V = _OFF_FOX_K + FOX_HEADS * HEAD_DIM
_OFF_FOX_F = _OFF_FOX_V + FOX_HEADS * HEAD_DIM
_OFF_MOBA_Q = _OFF_FOX_F + FOX_HEADS
_OFF_MOBA_K = _OFF_MOBA_Q + MOBA_HEADS * HEAD_DIM
_OFF_MOBA_V = _OFF_MOBA_K + MOBA_HEADS * HEAD_DIM
IN_COLS = _OFF_MOBA_V + MOBA_HEADS * HEAD_DIM

N_HEAD_COLS = 2816
GATE_LANES = 3 * NSA_REP
PROJ_COLS = N_HEAD_COLS + 2 * LANES


def _dot(a, b):
    return jnp.dot(a, b, preferred_element_type=F32)


def _dot_nt(a, b, precision=None):
    return lax.dot_general(a, b, (((1,), (1,)), ((), ())), precision=precision,
                           preferred_element_type=F32)


def _iota(shape, dim):
    return lax.broadcasted_iota(jnp.int32, shape, dim)


def _params(*sem, flags=None):
    return pltpu.CompilerParams(dimension_semantics=sem, vmem_limit_bytes=VMEM_LIMIT, flags=flags)


def _ada_norm(x, g, shift, scale):
    ms = jnp.mean(x * x, axis=-1, keepdims=True)
    y = x * lax.rsqrt(ms + EPS) * g
    return y * (1.0 + scale) + shift


def _ada_kernel(c_ref, w_ref, b_ref, o_ref):
    c = c_ref[...]
    ca = c * jax.nn.sigmoid(c)
    o_ref[0] = jnp.dot(ca, w_ref[0], precision=HIGHEST, preferred_element_type=F32) + b_ref[0]


def _ada_mod(c, w_ada, b_ada):
    depth, d, n = w_ada.shape
    b = c.shape[0]
    tn = 1152
    return pl.pallas_call(
        _ada_kernel,
        grid=(depth, n // tn),
        in_specs=[pl.BlockSpec((b, d), lambda l, j: (0, 0)),
                  pl.BlockSpec((1, d, tn), lambda l, j: (l, 0, j)),
                  pl.BlockSpec((1, 1, tn), lambda l, j: (l, 0, j))],
        out_specs=pl.BlockSpec((1, b, tn), lambda l, j: (l, 0, j)),
        out_shape=jax.ShapeDtypeStruct((depth, b, n), F32),
        compiler_params=_params("parallel", "parallel"),
        name="ada_mod",
    )(c, w_ada, b_ada.reshape(depth, 1, n))


def _ffn_kernel(x_ref, mod_ref, g_ref, wa_ref, wb_ref, w2_ref, o_ref, h_ref, acc_ref, *, sub, coef):
    j = pl.program_id(1)

    @pl.when(j == 0)
    def _():
        h = _ada_norm(x_ref[...], g_ref[...], mod_ref[0, 3 * sub:3 * sub + 1, :],
                      mod_ref[0, 3 * sub + 1:3 * sub + 2, :])
        h_ref[...] = h.astype(BF16)
        acc_ref[...] = jnp.zeros_like(acc_ref)

    h = h_ref[...]
    a = _dot(h, wa_ref[...])
    b = _dot(h, wb_ref[...])
    u = (a * jax.nn.sigmoid(a) * b).astype(BF16)
    acc_ref[...] += _dot(u, w2_ref[...])

    @pl.when(j == pl.num_programs(1) - 1)
    def _():
        gate = mod_ref[0, 3 * sub + 2:3 * sub + 3, :]
        o_ref[...] = x_ref[...] + (coef * gate) * acc_ref[...]


def _ffn(x2, mod, g, w13, w2, *, sub, seq, tm=1024, tf=256):
    n, d = x2.shape
    nf = D_FF // tf
    per_b = seq // tm
    return pl.pallas_call(
        functools.partial(_ffn_kernel, sub=sub, coef=0.5),
        grid=(n // tm, nf),
        in_specs=[pl.BlockSpec((tm, d), lambda i, j: (i, 0)),
                  pl.BlockSpec((1, 9, d), lambda i, j: (i // per_b, 0, 0)),
                  pl.BlockSpec((1, d), lambda i, j: (0, 0)),
                  pl.BlockSpec((d, tf), lambda i, j: (0, j)),
                  pl.BlockSpec((d, tf), lambda i, j: (0, nf + j)),
                  pl.BlockSpec((tf, d), lambda i, j: (j, 0))],
        out_specs=pl.BlockSpec((tm, d), lambda i, j: (i, 0)),
        out_shape=jax.ShapeDtypeStruct((n, d), F32),
        scratch_shapes=[pltpu.VMEM((tm, d), BF16), pltpu.VMEM((tm, d), F32)],
        compiler_params=_params("parallel", "arbitrary"),
        name="ffn",
    )(x2, mod, g, w13, w13, w2)


def _inproj_kernel(x_ref, mod_ref, g_ref, pos_ref, inv_ref, fb_ref, w_ref,
                   qn_ref, cmp_ref, nks_ref, nkw_ref, nv_ref, fqk_ref, fv_ref, mq_ref, mk_ref, mv_ref, gm_ref,
                   *, per_b):
    h = _ada_norm(x_ref[...], g_ref[...], mod_ref[0, 3:4, :], mod_ref[0, 4:5, :]).astype(BF16)
    tm = h.shape[0]
    seq_pos = (pl.program_id(0) % per_b) * tm + _iota((tm, LANES), 0)

    ang = pos_ref[...].astype(F32) * inv_ref[...]
    cosl = jnp.cos(ang)
    sinl = jnp.sin(ang)
    lane = _iota((tm, LANES), 1)
    hl = lane & (HEAD_DIM - 1)
    sin_lo = jnp.where(hl < ROPE_DIM // 2, -sinl, 0.0)
    sin_hi = jnp.where(hl >= ROPE_DIM // 2, sinl, 0.0)
    one_hot = jnp.where(lane == HEAD_DIM, 1.0, 0.0)

    def rope(y):
        return (y * cosl + pltpu.roll(y, LANES - ROPE_DIM // 2, 1) * sin_lo
                + pltpu.roll(y, ROPE_DIM // 2, 1) * sin_hi)

    def put_heads(z, out_ref, head0, dtype):
        z = z.astype(dtype)
        out_ref[0, head0] = z[:, :HEAD_DIM]
        out_ref[0, head0 + 1] = z[:, HEAD_DIM:]

    def put_wide(z, out_ref, head0, tail):
        out_ref[0, head0] = jnp.where(lane < HEAD_DIM, z, tail).astype(BF16)
        out_ref[0, head0 + 1] = jnp.where(lane < HEAD_DIM, pltpu.roll(z, HEAD_DIM, 1), tail).astype(BF16)

    def put_values(z, out_ref, head0):
        put_wide(z, out_ref, head0, one_hot)

    def put_keys(z, out_ref, head0, block_len):
        block = lax.shift_right_logical(seq_pos, block_len.bit_length() - 1)
        put_wide(z, out_ref, head0, jnp.where(lane - HEAD_DIM == block, 1.0, 0.0))

    def slab(c0):
        y = _dot(h, w_ref[:, c0:c0 + 2 * LANES])
        return y[:, :LANES], y[:, LANES:]

    for s_idx in range(2):
        a, b = slab(256 * s_idx)
        put_heads(rope(a) * ATT_SCALE, qn_ref, 4 * s_idx, BF16)
        put_heads(rope(b) * ATT_SCALE, qn_ref, 4 * s_idx + 2, BF16)
    a, b = slab(512)
    put_heads(rope(a), cmp_ref, 0, F32)
    put_heads(b, cmp_ref, 2, F32)
    a, b = slab(768)
    put_keys(rope(a), nks_ref, 0, SLC_LEN)
    put_values(b, nv_ref, 0)
    a, b = slab(1024)
    put_heads(rope(a), nkw_ref, 0, BF16)
    put_values(b, nv_ref, 2)
    a, b = slab(1280)
    put_heads(a * ATT_SCALE, fqk_ref, 0, BF16)
    put_heads(b * ATT_SCALE, fqk_ref, 2, BF16)
    a, b = slab(1536)
    put_heads(a, fqk_ref, 4, BF16)
    put_heads(b, fqk_ref, 6, BF16)
    a, b = slab(1792)
    put_values(a, fv_ref, 0)
    put_values(b, fv_ref, 2)
    a, b = slab(2048)
    put_heads(rope(a) * ATT_SCALE, mq_ref, 0, BF16)
    put_heads(rope(b) * ATT_SCALE, mq_ref, 2, BF16)
    a, b = slab(2304)
    put_keys(rope(a), mk_ref, 0, MOBA_BLOCK)
    put_keys(rope(b), mk_ref, 2, MOBA_BLOCK)
    a, b = slab(2560)
    put_values(a, mv_ref, 0)
    put_values(b, mv_ref, 2)
    a, b = slab(N_HEAD_COLS)
    gm_ref[0, 0] = jnp.where(lane < GATE_LANES, jax.nn.sigmoid(a), jax.nn.log_sigmoid(a + fb_ref[...]))
    gm_ref[0, 1] = jax.nn.sigmoid(b)


def _inproj(x2, mod, g, pos2, inv_lane, fb_lane, w_p, *, batch, seq, tm=512):
    n, d = x2.shape
    per_b = seq // tm

    def hm(nh, dtype, width=HEAD_DIM):
        return (pl.BlockSpec((1, nh, tm, width), lambda i: (i // per_b, 0, i % per_b, 0)),
                jax.ShapeDtypeStruct((batch, nh, seq, width), dtype))

    specs = [hm(NSA_HEADS, BF16), hm(4, F32), hm(2, BF16, LANES), hm(2, BF16), hm(4, BF16, LANES),
             hm(8, BF16), hm(4, BF16, LANES),
             hm(4, BF16), hm(4, BF16, LANES), hm(4, BF16, LANES), hm(2, F32, LANES)]
    return pl.pallas_call(
        functools.partial(_inproj_kernel, per_b=per_b),
        grid=(n // tm,),
        in_specs=[pl.BlockSpec((tm, d), lambda i: (i, 0)),
                  pl.BlockSpec((1, 9, d), lambda i: (i // per_b, 0, 0)),
                  pl.BlockSpec((1, d), lambda i: (0, 0)),
                  pl.BlockSpec((tm, 1), lambda i: (i, 0)),
                  pl.BlockSpec((1, LANES), lambda i: (0, 0)),
                  pl.BlockSpec((1, LANES), lambda i: (0, 0)),
                  pl.BlockSpec((d, PROJ_COLS), lambda i: (0, 0))],
        out_specs=[s for s, _ in specs],
        out_shape=[s for _, s in specs],
        compiler_params=_params("parallel"),
        name="inproj",
    )(x2, mod, g, pos2, inv_lane, fb_lane, w_p)


def _compress_kernel(z_ref, pos_ref, w1_ref, w2_ref, o_ref):
    z = z_ref[0, 0]
    top = _dot((z + pos_ref[0, 0]).astype(BF16), w1_ref[0, 0])
    bot = _dot((z + pos_ref[0, 1]).astype(BF16), w1_ref[0, 1])
    nch = z.shape[0]
    hid = top + pltpu.roll(bot, nch - 1, 0)
    act = jax.nn.gelu(hid).astype(BF16)
    o_ref[0, 0] = _dot(act, w2_ref[0]).astype(o_ref.dtype)


def _compress(cmp_in, pos_l, w1_l, w2_l):
    b, four, nch, width = cmp_in.shape
    return pl.pallas_call(
        _compress_kernel,
        grid=(b, four),
        in_specs=[pl.BlockSpec((1, 1, nch, width), lambda i, n: (i, n, 0, 0)),
                  pl.BlockSpec((1, 2, 1, width), lambda i, n: (n // 2, 0, 0, 0)),
                  pl.BlockSpec((1, 2, width, CMP_HIDDEN), lambda i, n: (n // 2, 0, 0, 0)),
                  pl.BlockSpec((1, CMP_HIDDEN, HEAD_DIM), lambda i, n: (n // 2, 0, 0))],
        out_specs=pl.BlockSpec((1, 1, nch, HEAD_DIM), lambda i, n: (i, n, 0, 0)),
        out_shape=jax.ShapeDtypeStruct((b, four, nch, HEAD_DIM), BF16),
        compiler_params=_params("parallel", "parallel"),
        name="nsa_compress",
    )(cmp_in, pos_l, w1_l, w2_l)


def _chunk_update(carry, s, v_aug):
    m, acc = carry
    m_new = jnp.maximum(m, jnp.max(s, axis=-1, keepdims=True))
    p = jnp.exp(s - m_new).astype(BF16)
    acc = jnp.exp(m - m_new) * acc + _dot(p, v_aug)
    return m_new, acc


def _chunk_init(rows):
    return jnp.full((rows, 1), NEG, F32), jnp.zeros((rows, LANES), F32)


def _chunk_finish(carry):
    _, acc = carry
    return acc[:, :HEAD_DIM] / jnp.maximum(acc[:, HEAD_DIM:HEAD_DIM + 1], 1e-30)


def _add_rows(s, bias, rep):
    tq = bias.shape[0]
    return jnp.concatenate([s[r * tq:(r + 1) * tq] + bias for r in range(rep)], axis=0)


def _nsa_kernel(q_ref, kc_ref, vc_ref, ks_ref, vs_ref, kw_ref, vw_ref, gate_ref, ov_ref, o_ref, *, tq, tk):
    i = pl.program_id(2)
    rep = NSA_REP
    q4 = jnp.concatenate([q_ref[0, r] for r in range(rep)], axis=0)
    t_row = i * tq + _iota((tq, 1), 0)
    t4 = jnp.concatenate([t_row] * rep, axis=0)

    n_c = kc_ref.shape[2]
    s = _dot_nt(q4, kc_ref[0, 0])
    cend = _iota((1, n_c), 1) * CMP_STRIDE + (CMP_LEN - 1)
    cm = cend <= t4
    s = jnp.where(cm, s, NEG)
    e = jnp.where(cm, jnp.exp(s - jnp.max(s, axis=-1, keepdims=True)), 0.0)
    p = e / jnp.maximum(jnp.sum(e, axis=-1, keepdims=True), 1e-30)
    o_cmp = _dot(p.astype(BF16), vc_ref[0, 0])

    psum = p[0:tq] + p[tq:2 * tq] + p[2 * tq:3 * tq] + p[3 * tq:4 * tq]
    imp = jnp.dot(psum, ov_ref[...], precision=HIGHEST, preferred_element_type=F32)
    n_s = ks_ref.shape[2] // SLC_LEN
    j = _iota((tq, LANES), 1)
    tb = lax.shift_right_logical(t_row, SLC_LEN.bit_length() - 1)
    valid = j <= tb
    forced = (j == 0) | (j == tb) | (j == tb - 1)
    val = jnp.where(valid, jnp.where(forced, jnp.inf, imp), -jnp.inf)
    v_t = val.T[:n_s]
    j_t = _iota((n_s, tq), 0)
    rank = jnp.zeros((n_s, tq), jnp.int32)
    for jp in range(n_s):
        row = v_t[jp:jp + 1, :]
        ahead = (row > v_t) | ((row == v_t) & (j_t > jp))
        rank = rank + ahead.astype(jnp.int32)
    neg_t = jnp.where((rank < SLC_TOPN) & (v_t > -jnp.inf), 0.0, NEG)
    neg = jnp.concatenate([neg_t, jnp.zeros((LANES - n_s, tq), F32)], axis=0).T[:, :HEAD_DIM]
    q4a = jnp.concatenate([jnp.concatenate([q_ref[0, r].astype(F32), neg], axis=1) for r in range(rep)],
                          axis=0).astype(BF16)

    def slc_scores(c):
        k0 = pl.multiple_of(c * tk, tk)
        return _dot_nt(q4a, ks_ref[0, 0, pl.ds(k0, tk), :]), vs_ref[0, 0, pl.ds(k0, tk), :], k0

    def slc_body(c, carry):
        sc, v, _ = slc_scores(c)
        return _chunk_update(carry, sc, v)

    span = WIN + tq
    start = pl.multiple_of(jnp.maximum(i * tq - WIN, 0), tq)
    dist = t_row - (start + _iota((1, span), 1))
    band = jnp.where((dist >= 0) & (dist < WIN), 0.0, NEG)
    sw = _add_rows(_dot_nt(q4, kw_ref[0, 0, pl.ds(start, span), :]), band, rep)
    pw = jnp.exp(sw - jnp.max(sw, axis=-1, keepdims=True)).astype(BF16)
    aw = _dot(pw, vw_ref[0, 0, pl.ds(start, span), :])
    o_win = aw[:, :HEAD_DIM] / jnp.maximum(aw[:, HEAD_DIM:HEAD_DIM + 1], 1e-30)

    last = i // (tk // tq)
    carry = lax.fori_loop(0, last, slc_body, _chunk_init(rep * tq))
    sc, v, k0 = slc_scores(last)
    causal = jnp.where(k0 + _iota((1, tk), 1) <= t_row, 0.0, NEG)
    o_slc = _chunk_finish(_chunk_update(carry, _add_rows(sc, causal, rep), v))

    gates = gate_ref[0, 0]
    outs = []
    for r in range(rep):
        rows = slice(r * tq, (r + 1) * tq)
        outs.append(gates[:, 3 * r:3 * r + 1] * o_cmp[rows]
                    + gates[:, 3 * r + 1:3 * r + 2] * o_slc[rows]
                    + gates[:, 3 * r + 2:3 * r + 3] * o_win[rows])
    o_ref[0] = jnp.concatenate(outs, axis=1).astype(o_ref.dtype)


def _nsa(qn, cmp_kv, nks, nkw, nv, gm, ov, *, tq=128, tk=512):
    b, _, seq, _ = qn.shape
    n_c = cmp_kv.shape[2]
    wide = lambda base: pl.BlockSpec((1, 1, seq, LANES), lambda bi, g, i: (bi, base + g, 0, 0))
    cmp_spec = lambda base: pl.BlockSpec((1, 1, n_c, HEAD_DIM), lambda bi, g, i: (bi, base + g, 0, 0))
    return pl.pallas_call(
        functools.partial(_nsa_kernel, tq=tq, tk=tk),
        grid=(b, NSA_GROUPS, seq // tq),
        in_specs=[pl.BlockSpec((1, NSA_REP, tq, HEAD_DIM), lambda bi, g, i: (bi, g, i, 0)),
                  cmp_spec(0), cmp_spec(2),
                  wide(0), wide(0),
                  pl.BlockSpec((1, 1, seq, HEAD_DIM), lambda bi, g, i: (bi, g, 0, 0)), wide(2),
                  pl.BlockSpec((1, 1, tq, LANES), lambda bi, g, i: (bi, g, i, 0)),
                  pl.BlockSpec((n_c, LANES), lambda bi, g, i: (0, 0))],
        out_specs=pl.BlockSpec((1, tq, NSA_REP * HEAD_DIM), lambda bi, g, i: (bi, i, g)),
        out_shape=jax.ShapeDtypeStruct((b, seq, NSA_HEADS * HEAD_DIM), BF16),
        compiler_params=_params("parallel", "parallel", "arbitrary"),
        name="nsa",
    )(qn, cmp_kv, cmp_kv, nks, nv, nkw, nv, gm, ov)


def _decay_kernel(lf_ref, tri_ref, col_ref, row_ref):
    seq = lf_ref.shape[2]
    blk = tri_ref.shape[0]
    carry = jnp.zeros((1, LANES), F32)
    for c in range(seq // blk):
        rows = slice(c * blk, (c + 1) * blk)
        cs = jnp.dot(tri_ref[...], lf_ref[0, 0, rows, :], precision=HIGHEST,
                     preferred_element_type=F32) + carry
        col_ref[0, rows, :] = cs
        carry = cs[blk - 1:blk, :]
    row_ref[0] = col_ref[0].T[8:16, :]


def _decay(gm, tri):
    b, _, seq, _ = gm.shape
    blk = tri.shape[0]
    return pl.pallas_call(
        _decay_kernel,
        grid=(b,),
        in_specs=[pl.BlockSpec((1, 1, seq, LANES), lambda i: (i, 0, 0, 0)),
                  pl.BlockSpec((blk, blk), lambda i: (0, 0))],
        out_specs=[pl.BlockSpec((1, seq, LANES), lambda i: (i, 0, 0)),
                   pl.BlockSpec((1, 8, seq), lambda i: (i, 0, 0))],
        out_shape=[jax.ShapeDtypeStruct((b, seq, LANES), F32),
                   jax.ShapeDtypeStruct((b, 8, seq), F32)],
        compiler_params=_params("parallel"),
        name="fox_decay",
    )(gm, tri)


def _fox_kernel(q_ref, k_ref, v_ref, fcol_ref, frow_ref, tri_ref, o_ref, *, tq):
    i = pl.program_id(1)
    heads = range(FOX_HEADS)
    qs = [q_ref[0, h] for h in heads]
    fqs = [fcol_ref[0, :, GATE_LANES + h:GATE_LANES + h + 1] for h in heads]

    def step(c, carries, diag):
        k0 = pl.multiple_of(c * tq, tq)
        new = []
        for h in heads:
            fk = frow_ref[0, 4 + h:5 + h, pl.ds(k0, tq)]
            sc = _dot_nt(qs[h], k_ref[0, h, pl.ds(k0, tq), :]) + (fqs[h] - fk)
            if diag:
                sc = sc + tri_ref[...]
            new.append(_chunk_update(carries[h], sc, v_ref[0, h, pl.ds(k0, tq), :]))
        return tuple(new)

    carries = step(i, tuple(_chunk_init(tq) for _ in heads), True)
    carries = lax.fori_loop(0, i, functools.partial(step, diag=False), carries)
    o_ref[0] = jnp.concatenate([_chunk_finish(cr) for cr in carries], axis=1).astype(o_ref.dtype)


def _fox(fqk, fv, fcol, frow, tri, *, tq=FOX_TILE):
    b, _, seq, _ = fqk.shape
    nh = FOX_HEADS
    return pl.pallas_call(
        functools.partial(_fox_kernel, tq=tq),
        grid=(b, seq // tq),
        in_specs=[pl.BlockSpec((1, nh, tq, HEAD_DIM), lambda bi, i: (bi, 0, i, 0)),
                  pl.BlockSpec((1, nh, seq, HEAD_DIM), lambda bi, i: (bi, 1, 0, 0)),
                  pl.BlockSpec((1, nh, seq, LANES), lambda bi, i: (bi, 0, 0, 0)),
                  pl.BlockSpec((1, tq, LANES), lambda bi, i: (bi, i, 0)),
                  pl.BlockSpec((1, 8, seq), lambda bi, i: (bi, 0, 0)),
                  pl.BlockSpec((tq, tq), lambda bi, i: (0, 0))],
        out_specs=pl.BlockSpec((1, tq, nh * HEAD_DIM), lambda bi, i: (bi, i, 0)),
        out_shape=jax.ShapeDtypeStruct((b, seq, nh * HEAD_DIM), BF16),
        compiler_params=_params("parallel", "arbitrary"),
        name="fox",
    )(fqk, fqk, fv, fcol, frow, tri)


def _moba_kernel(q_ref, k_ref, v_ref, avg_ref, tri_ref, o_ref, km_ref, *, tk):
    tq = MOBA_BLOCK
    i = pl.program_id(1)
    nb = avg_ref.shape[0]

    @pl.when(i == 0)
    def _():
        for h in range(MOBA_HEADS):
            km_ref[h] = _dot(avg_ref[...], k_ref[0, h])

    blk = _iota((nb, tq), 0)
    qas = []
    for h in range(MOBA_HEADS):
        q = q_ref[0, h]
        gate = _dot_nt(km_ref[h][:, :HEAD_DIM], q.astype(F32), precision=HIGHEST)
        val = jnp.where(blk < i, gate, -jnp.inf)
        rank = jnp.zeros((nb, tq), jnp.int32)
        for jp in range(nb):
            row = val[jp:jp + 1, :]
            ahead = (row > val) | ((row == val) & (blk > jp))
            rank = rank + ahead.astype(jnp.int32)
        keep = ((rank < MOBA_TOPK) & (blk < i)) | (blk == i)
        neg_t = jnp.where(keep, 0.0, NEG)
        neg = jnp.concatenate([neg_t, jnp.zeros((LANES - nb, tq), F32)], axis=0).T[:, :HEAD_DIM]
        qas.append(jnp.concatenate([q.astype(F32), neg], axis=1).astype(BF16))

    def step(c, carries, own):
        k0 = pl.multiple_of(c * tk, tk)
        new = []
        for h in range(MOBA_HEADS):
            sc = _dot_nt(qas[h], k_ref[0, h, pl.ds(k0, tk), :])
            if own:
                sc = sc + tri_ref[i % (tk // tq)]
            new.append(_chunk_update(carries[h], sc, v_ref[0, h, pl.ds(k0, tk), :]))
        return tuple(new)

    last = i // (tk // tq)
    carries = step(last, tuple(_chunk_init(tq) for _ in range(MOBA_HEADS)), True)
    carries = lax.fori_loop(0, last, functools.partial(step, own=False), carries)
    o_ref[0] = jnp.concatenate([_chunk_finish(cr) for cr in carries], axis=1).astype(o_ref.dtype)


def _moba(mq, mk, mv, avg, tri, *, tk=512):
    b, _, seq, _ = mq.shape
    nh = MOBA_HEADS
    tq = MOBA_BLOCK
    nb = avg.shape[0]
    return pl.pallas_call(
        functools.partial(_moba_kernel, tk=tk),
        grid=(b, seq // tq),
        in_specs=[pl.BlockSpec((1, nh, tq, HEAD_DIM), lambda bi, i: (bi, 0, i, 0)),
                  pl.BlockSpec((1, nh, seq, LANES), lambda bi, i: (bi, 0, 0, 0)),
                  pl.BlockSpec((1, nh, seq, LANES), lambda bi, i: (bi, 0, 0, 0)),
                  pl.BlockSpec((nb, seq), lambda bi, i: (0, 0)),
                  pl.BlockSpec((tk // tq, tq, tk), lambda bi, i: (0, 0, 0))],
        out_specs=pl.BlockSpec((1, tq, nh * HEAD_DIM), lambda bi, i: (bi, i, 0)),
        out_shape=jax.ShapeDtypeStruct((b, seq, nh * HEAD_DIM), BF16),
        scratch_shapes=[pltpu.VMEM((nh, nb, LANES), F32)],
        compiler_params=_params("parallel", "arbitrary"),
        name="moba",
    )(mq, mk, mv, avg, tri)


def _outproj_kernel(x_ref, mod_ref, on_ref, of_ref, om_ref, w_ref, o_ref):
    n_w = NSA_HEADS * HEAD_DIM
    f_w = FOX_HEADS * HEAD_DIM
    acc = _dot(on_ref[...], w_ref[0:n_w, :])
    acc += _dot(of_ref[...], w_ref[n_w:n_w + f_w, :])
    acc += _dot(om_ref[...], w_ref[n_w + f_w:, :])
    o_ref[...] = x_ref[...] + mod_ref[0, 5:6, :] * acc


def _outproj(x2, mod, o_nsa, o_fox, o_moba, w_out, *, seq, tm=512):
    n, d = x2.shape
    per_b = seq // tm
    row = lambda width: pl.BlockSpec((tm, width), lambda i: (i, 0))
    return pl.pallas_call(
        _outproj_kernel,
        grid=(n // tm,),
        in_specs=[row(d), pl.BlockSpec((1, 9, d), lambda i: (i // per_b, 0, 0)),
                  row(o_nsa.shape[1]), row(o_fox.shape[1]), row(o_moba.shape[1]),
                  pl.BlockSpec((d, d), lambda i: (0, 0))],
        out_specs=row(d),
        out_shape=jax.ShapeDtypeStruct((n, d), F32),
        compiler_params=_params("parallel"),
        name="outproj",
    )(x2, mod, o_nsa, o_fox, o_moba, w_out)


def _final_kernel(x_ref, g_ref, o_ref):
    x = x_ref[...]
    ms = jnp.mean(x * x, axis=-1, keepdims=True)
    o_ref[...] = x * lax.rsqrt(ms + EPS) * g_ref[...]


def _final_norm(x2, g, tm=1024):
    n, d = x2.shape
    return pl.pallas_call(
        _final_kernel,
        grid=(n // tm,),
        in_specs=[pl.BlockSpec((tm, d), lambda i: (i, 0)), pl.BlockSpec((1, d), lambda i: (0, 0))],
        out_specs=pl.BlockSpec((tm, d), lambda i: (i, 0)),
        out_shape=jax.ShapeDtypeStruct((n, d), F32),
        compiler_params=_params("parallel"),
        name="final_norm",
    )(x2, g)


def _pack_w_in(w_in):
    depth, d, _ = w_in.shape

    def cols(a, b):
        return w_in[:, :, a:b]

    zeros = lambda n: jnp.zeros((depth, d, n), w_in.dtype)
    g_half = 3 * NSA_REP
    parts = [cols(_OFF_NSA_Q, _OFF_GATE),
             cols(_OFF_FOX_Q, _OFF_FOX_F),
             cols(_OFF_MOBA_Q, IN_COLS),
             cols(_OFF_GATE, _OFF_GATE + g_half), cols(_OFF_FOX_F, _OFF_FOX_F + FOX_HEADS),
             zeros(LANES - g_half - FOX_HEADS),
             cols(_OFF_GATE + g_half, _OFF_GATE + 2 * g_half), zeros(LANES - g_half)]
    return jnp.concatenate(parts, axis=-1).astype(BF16)


def _constants(seq):
    n_c = seq // CMP_STRIDE
    n_s = seq // SLC_LEN
    c0 = np.arange(n_c)[:, None] * CMP_STRIDE
    j0 = np.arange(LANES)[None, :] * SLC_LEN
    real = (np.arange(n_c)[:, None] < (seq - CMP_LEN) // CMP_STRIDE + 1) & (np.arange(LANES)[None, :] < n_s)
    ov = ((c0 < j0 + SLC_LEN) & (c0 + CMP_LEN > j0) & real).astype(np.float32)
    nb = seq // MOBA_BLOCK
    avg = (np.arange(nb)[:, None] == (np.arange(seq)[None, :] // MOBA_BLOCK)).astype(np.float32) / MOBA_BLOCK
    tri = np.tril(np.ones((256, 256), np.float32))

    def causal_bias(n):
        return np.where(np.arange(n)[None, :] <= np.arange(n)[:, None], 0.0, NEG).astype(np.float32)

    own = causal_bias(MOBA_BLOCK)
    causal_moba = np.stack([np.concatenate([own, np.full_like(own, NEG)], axis=1),
                            np.concatenate([np.zeros_like(own), own], axis=1)])
    return dict(ov=jnp.asarray(ov), avg=jnp.asarray(avg, dtype=BF16), tri=jnp.asarray(tri),
                causal_fox=jnp.asarray(causal_bias(FOX_TILE)), causal_moba=jnp.asarray(causal_moba))


def kernel(x, c, positions, norm_g, w_ada, b_ada, w_in, fox_fbias, cmp_pos, cmp_w1, cmp_w2, w_out,
           ffn_w13, ffn_w2, final_g):
    batch, seq, d = x.shape
    depth = w_ada.shape[0]
    assert d == D_MODEL and seq % MOBA_BLOCK == 0 and seq % 1024 == 0

    w_in_p = _pack_w_in(w_in)
    w13 = ffn_w13.astype(BF16)
    w2 = ffn_w2.astype(BF16)
    w_o = w_out.astype(BF16)
    half = CMP_LEN * HEAD_DIM // 2
    cw1 = cmp_w1.astype(BF16).reshape(depth, 2, 2, half, CMP_HIDDEN)
    cw2 = cmp_w2.astype(BF16)
    cpos = cmp_pos.reshape(depth, 2, 2, 1, half)
    cst = _constants(seq)

    inv = ROPE_THETA ** (-jnp.arange(0, ROPE_DIM, 2, dtype=F32) / ROPE_DIM)
    lane = np.arange(LANES) % HEAD_DIM
    inv_lane = jnp.where(jnp.asarray(lane < ROPE_DIM), inv[jnp.asarray(lane % (ROPE_DIM // 2))], 0.0)
    inv_lane = inv_lane.reshape(1, LANES).astype(F32)
    fb_lane = jnp.zeros((depth, 1, LANES), F32).at[:, 0, GATE_LANES:GATE_LANES + FOX_HEADS].set(fox_fbias)

    mod_all = _ada_mod(c, w_ada, b_ada).reshape(depth, batch, 9, d)
    pos2 = positions.reshape(batch * seq, 1)
    x2 = x.reshape(batch * seq, d)
    for l in range(depth):
        mod = mod_all[l]
        x2 = _ffn(x2, mod, norm_g[l, 0:1], w13[l, 0], w2[l, 0], sub=0, seq=seq)
        qn, cmp_in, nks, nkw, nv, fqk, fv, mq, mk, mv, gm = _inproj(
            x2, mod, norm_g[l, 1:2], pos2, inv_lane, fb_lane[l], w_in_p[l], batch=batch, seq=seq)
        cmp_kv = _compress(cmp_in.reshape(batch, 4, seq // CMP_STRIDE, CMP_STRIDE * HEAD_DIM),
                           cpos[l], cw1[l], cw2[l])
        o_nsa = _nsa(qn, cmp_kv, nks, nkw, nv, gm, cst["ov"])
        fcol, frow = _decay(gm, cst["tri"])
        o_fox = _fox(fqk, fv, fcol, frow, cst["causal_fox"])
        o_moba = _moba(mq, mk, mv, cst["avg"], cst["causal_moba"])
        x2 = _outproj(x2, mod, o_nsa.reshape(batch * seq, -1), o_fox.reshape(batch * seq, -1),
                      o_moba.reshape(batch * seq, -1), w_o[l], seq=seq)
        x2 = _ffn(x2, mod, norm_g[l, 2:3], w13[l, 1], w2[l, 1], sub=2, seq=seq, tm=512, tf=1408)
    return _final_norm(x2, final_g.reshape(1, d)).reshape(batch, seq, d)
```

```python
import functools

import numpy as np
import jax
import jax.numpy as jnp
from jax import lax
from jax.experimental import pallas as pl
from jax.experimental.pallas import tpu as pltpu

D_MODEL = 1024
HEAD_DIM = 64
NSA_HEADS = 8
NSA_GROUPS = 2
NSA_REP = NSA_HEADS // NSA_GROUPS
FOX_HEADS = 4
MOBA_HEADS = 4
ROPE_DIM = HEAD_DIM // 4
ROPE_THETA = 500000.0
CMP_LEN = 32
CMP_STRIDE = 16
CMP_HIDDEN = 4 * HEAD_DIM
SLC_LEN = 64
SLC_TOPN = 8
WIN = 512
MOBA_BLOCK = 256
MOBA_TOPK = 3
D_FF = 2816
EPS = 1e-6
ATT_SCALE = HEAD_DIM ** -0.5

FOX_TILE = 512
MOBA_TILE = 2 * MOBA_BLOCK
LANES = 128
NEG = -1e30
VMEM_LIMIT = 56 * 1024 * 1024

F32 = jnp.float32
BF16 = jnp.bfloat16
HIGHEST = lax.Precision.HIGHEST

_KV = NSA_GROUPS * HEAD_DIM
_OFF_NSA_Q = 0
_OFF_KC = _OFF_NSA_Q + NSA_HEADS * HEAD_DIM
_OFF_VC = _OFF_KC + _KV
_OFF_KS = _OFF_VC + _KV
_OFF_VS = _OFF_KS + _KV
_OFF_KW = _OFF_VS + _KV
_OFF_VW = _OFF_KW + _KV
_OFF_GATE = _OFF_VW + _KV
_OFF_FOX_Q = _OFF_GATE + 3 * NSA_HEADS
_OFF_FOX_K = _OFF_FOX_Q + FOX_HEADS * HEAD_DIM
_OFF_FOX_V = _OFF_FOX_K + FOX_HEADS * HEAD_DIM
_OFF_FOX_F = _OFF_FOX_V + FOX_HEADS * HEAD_DIM
_OFF_MOBA_Q = _OFF_FOX_F + FOX_HEADS
_OFF_MOBA_K = _OFF_MOBA_Q + MOBA_HEADS * HEAD_DIM
_OFF_MOBA_V = _OFF_MOBA_K + MOBA_HEADS * HEAD_DIM
IN_COLS = _OFF_MOBA_V + MOBA_HEADS * HEAD_DIM

N_HEAD_COLS = 2816
GATE_LANES = 3 * NSA_REP
PROJ_COLS = N_HEAD_COLS + 2 * LANES


def _dot(a, b):
    return jnp.dot(a, b, preferred_element_type=F32)


def _dot_nt(a, b, precision=None):
    return lax.dot_general(a, b, (((1,), (1,)), ((), ())), precision=precision,
                           preferred_element_type=F32)


def _iota(shape, dim):
    return lax.broadcasted_iota(jnp.int32, shape, dim)


def _params(*sem, flags=None):
    return pltpu.CompilerParams(dimension_semantics=sem, vmem_limit_bytes=VMEM_LIMIT, flags=flags)


def _ada_norm(x, g, shift, scale):
    ms = jnp.mean(x * x, axis=-1, keepdims=True)
    y = x * lax.rsqrt(ms + EPS) * g
    return y * (1.0 + scale) + shift


def _ada_kernel(c_ref, w_ref, b_ref, o_ref):
    c = c_ref[...]
    ca = c * jax.nn.sigmoid(c)
    o_ref[0] = jnp.dot(ca, w_ref[0], precision=HIGHEST, preferred_element_type=F32) + b_ref[0]


def _ada_mod(c, w_ada, b_ada):
    depth, d, n = w_ada.shape
    b = c.shape[0]
    tn = 1152
    return pl.pallas_call(
        _ada_kernel,
        grid=(depth, n // tn),
        in_specs=[pl.BlockSpec((b, d), lambda l, j: (0, 0)),
                  pl.BlockSpec((1, d, tn), lambda l, j: (l, 0, j)),
                  pl.BlockSpec((1, 1, tn), lambda l, j: (l, 0, j))],
        out_specs=pl.BlockSpec((1, b, tn), lambda l, j: (l, 0, j)),
        out_shape=jax.ShapeDtypeStruct((depth, b, n), F32),
        compiler_params=_params("parallel", "parallel"),
        name="ada_mod",
    )(c, w_ada, b_ada.reshape(depth, 1, n))


def _ffn_kernel(x_ref, mod_ref, g_ref, fg_ref, wa_ref, wb_ref, w2_ref, o_ref, h_ref, *, sub, coef, final):
    j = pl.program_id(1)

    @pl.when(j == 0)
    def _():
        h = _ada_norm(x_ref[...], g_ref[...], mod_ref[0, 3 * sub:3 * sub + 1, :],
                      mod_ref[0, 3 * sub + 1:3 * sub + 2, :])
        h_ref[...] = h.astype(BF16)

    h = h_ref[...]
    a = _dot(h, wa_ref[...])
    b = _dot(h, wb_ref[...])
    u = (a * jax.nn.sigmoid(a) * b).astype(BF16)
    part = _dot(u, w2_ref[...])

    @pl.when(j == 0)
    def _():
        o_ref[...] = part

    @pl.when(j == 1)
    def _():
        gate = mod_ref[0, 3 * sub + 2:3 * sub + 3, :]
        y = x_ref[...] + (coef * gate) * (o_ref[...] + part)
        if final:
            y = y * lax.rsqrt(jnp.mean(y * y, axis=-1, keepdims=True) + EPS) * fg_ref[...]
        o_ref[...] = y


def _ffn(x2, mod, g, final_g, w13, w2, *, sub, seq, final, tm=1024):
    n, d = x2.shape
    nf = 2
    tf = D_FF // nf
    per_b = seq // tm
    return pl.pallas_call(
        functools.partial(_ffn_kernel, sub=sub, coef=0.5, final=final),
        grid=(n // tm, nf),
        in_specs=[pl.BlockSpec((tm, d), lambda i, j: (i, 0)),
                  pl.BlockSpec((1, 9, d), lambda i, j: (i // per_b, 0, 0)),
                  pl.BlockSpec((1, d), lambda i, j: (0, 0)),
                  pl.BlockSpec((1, d), lambda i, j: (0, 0)),
                  pl.BlockSpec((d, tf), lambda i, j: (0, j)),
                  pl.BlockSpec((d, tf), lambda i, j: (0, nf + j)),
                  pl.BlockSpec((tf, d), lambda i, j: (j, 0))],
        out_specs=pl.BlockSpec((tm, d), lambda i, j: (i, 0)),
        out_shape=jax.ShapeDtypeStruct((n, d), F32),
        scratch_shapes=[pltpu.VMEM((tm, d), BF16)],
        compiler_params=_params("parallel", "arbitrary"),
        name="ffn",
    )(x2, mod, g, final_g, w13, w13, w2)


def _inproj_kernel(x_ref, mod_ref, g_ref, pos_ref, inv_ref, fb_ref, w_ref,
                   qn_ref, cmp_ref, nks_ref, nkw_ref, nv_ref, fqk_ref, fv_ref, mq_ref, mk_ref, mv_ref, gm_ref,
                   *, per_b):
    h = _ada_norm(x_ref[...], g_ref[...], mod_ref[0, 3:4, :], mod_ref[0, 4:5, :]).astype(BF16)
    tm = h.shape[0]
    seq_pos = (pl.program_id(0) % per_b) * tm + _iota((tm, LANES), 0)

    ang = pos_ref[...].astype(F32) * inv_ref[...]
    cosl = jnp.cos(ang)
    sinl = jnp.sin(ang)
    lane = _iota((tm, LANES), 1)
    hl = lane & (HEAD_DIM - 1)
    sin_lo = jnp.where(hl < ROPE_DIM // 2, -sinl, 0.0)
    sin_hi = jnp.where(hl >= ROPE_DIM // 2, sinl, 0.0)
    one_hot = jnp.where(lane == HEAD_DIM, 1.0, 0.0)

    def rope(y):
        return (y * cosl + pltpu.roll(y, LANES - ROPE_DIM // 2, 1) * sin_lo
                + pltpu.roll(y, ROPE_DIM // 2, 1) * sin_hi)

    def put_heads(z, out_ref, head0, dtype):
        z = z.astype(dtype)
        out_ref[0, head0] = z[:, :HEAD_DIM]
        out_ref[0, head0 + 1] = z[:, HEAD_DIM:]

    def put_wide(z, out_ref, head0, tail):
        out_ref[0, head0] = jnp.where(lane < HEAD_DIM, z, tail).astype(BF16)
        out_ref[0, head0 + 1] = jnp.where(lane < HEAD_DIM, pltpu.roll(z, HEAD_DIM, 1), tail).astype(BF16)

    def put_values(z, out_ref, head0):
        put_wide(z, out_ref, head0, one_hot)

    def put_keys(z, out_ref, head0, block_len):
        block = lax.shift_right_logical(seq_pos, block_len.bit_length() - 1)
        put_wide(z, out_ref, head0, jnp.where(lane - HEAD_DIM == block, 1.0, 0.0))

    def slab(c0):
        y = _dot(h, w_ref[:, c0:c0 + 2 * LANES])
        return y[:, :LANES], y[:, LANES:]

    for s_idx in range(2):
        a, b = slab(256 * s_idx)
        put_heads(rope(a) * ATT_SCALE, qn_ref, 4 * s_idx, BF16)
        put_heads(rope(b) * ATT_SCALE, qn_ref, 4 * s_idx + 2, BF16)
    a, b = slab(512)
    put_heads(rope(a), cmp_ref, 0, F32)
    put_heads(b, cmp_ref, 2, F32)
    a, b = slab(768)
    put_keys(rope(a), nks_ref, 0, SLC_LEN)
    put_values(b, nv_ref, 0)
    a, b = slab(1024)
    put_heads(rope(a), nkw_ref, 0, BF16)
    put_values(b, nv_ref, 2)
    a, b = slab(1280)
    put_heads(a * ATT_SCALE, fqk_ref, 0, BF16)
    put_heads(b * ATT_SCALE, fqk_ref, 2, BF16)
    a, b = slab(1536)
    put_heads(a, fqk_ref, 4, BF16)
    put_heads(b, fqk_ref, 6, BF16)
    a, b = slab(1792)
    put_values(a, fv_ref, 0)
    put_values(b, fv_ref, 2)
    a, b = slab(2048)
    put_heads(rope(a) * ATT_SCALE, mq_ref, 0, BF16)
    put_heads(rope(b) * ATT_SCALE, mq_ref, 2, BF16)
    a, b = slab(2304)
    put_keys(rope(a), mk_ref, 0, MOBA_BLOCK)
    put_keys(rope(b), mk_ref, 2, MOBA_BLOCK)
    a, b = slab(2560)
    put_values(a, mv_ref, 0)
    put_values(b, mv_ref, 2)
    a, b = slab(N_HEAD_COLS)
    gm_ref[0, 0] = jnp.where(lane < GATE_LANES, jax.nn.sigmoid(a), jax.nn.log_sigmoid(a + fb_ref[...]))
    gm_ref[0, 1] = jax.nn.sigmoid(b)


def _inproj(x2, mod, g, pos2, inv_lane, fb_lane, w_p, *, batch, seq, tm=512):
    n, d = x2.shape
    per_b = seq // tm

    def hm(nh, dtype, width=HEAD_DIM):
        return (pl.BlockSpec((1, nh, tm, width), lambda i: (i // per_b, 0, i % per_b, 0)),
                jax.ShapeDtypeStruct((batch, nh, seq, width), dtype))

    specs = [hm(NSA_HEADS, BF16), hm(4, F32), hm(2, BF16, LANES), hm(2, BF16), hm(4, BF16, LANES),
             hm(8, BF16), hm(4, BF16, LANES),
             hm(4, BF16), hm(4, BF16, LANES), hm(4, BF16, LANES), hm(2, F32, LANES)]
    return pl.pallas_call(
        functools.partial(_inproj_kernel, per_b=per_b),
        grid=(n // tm,),
        in_specs=[pl.BlockSpec((tm, d), lambda i: (i, 0)),
                  pl.BlockSpec((1, 9, d), lambda i: (i // per_b, 0, 0)),
                  pl.BlockSpec((1, d), lambda i: (0, 0)),
                  pl.BlockSpec((tm, 1), lambda i: (i, 0)),
                  pl.BlockSpec((1, LANES), lambda i: (0, 0)),
                  pl.BlockSpec((1, LANES), lambda i: (0, 0)),
                  pl.BlockSpec((d, PROJ_COLS), lambda i: (0, 0))],
        out_specs=[s for s, _ in specs],
        out_shape=[s for _, s in specs],
        compiler_params=_params("parallel"),
        name="inproj",
    )(x2, mod, g, pos2, inv_lane, fb_lane, w_p)


def _compress_kernel(z_ref, pos_ref, w1_ref, w2_ref, o_ref):
    z = z_ref[0, 0]
    top = _dot((z + pos_ref[0, 0]).astype(BF16), w1_ref[0, 0])
    bot = _dot((z + pos_ref[0, 1]).astype(BF16), w1_ref[0, 1])
    nch = z.shape[0]
    hid = top + pltpu.roll(bot, nch - 1, 0)
    act = jax.nn.gelu(hid).astype(BF16)
    o_ref[0, 0] = _dot(act, w2_ref[0]).astype(o_ref.dtype)


def _compress(cmp_in, pos_l, w1_l, w2_l):
    b, four, nch, width = cmp_in.shape
    return pl.pallas_call(
        _compress_kernel,
        grid=(b, four),
        in_specs=[pl.BlockSpec((1, 1, nch, width), lambda i, n: (i, n, 0, 0)),
                  pl.BlockSpec((1, 2, 1, width), lambda i, n: (n // 2, 0, 0, 0)),
                  pl.BlockSpec((1, 2, width, CMP_HIDDEN), lambda i, n: (n // 2, 0, 0, 0)),
                  pl.BlockSpec((1, CMP_HIDDEN, HEAD_DIM), lambda i, n: (n // 2, 0, 0))],
        out_specs=pl.BlockSpec((1, 1, nch, HEAD_DIM), lambda i, n: (i, n, 0, 0)),
        out_shape=jax.ShapeDtypeStruct((b, four, nch, HEAD_DIM), BF16),
        compiler_params=_params("parallel", "parallel"),
        name="nsa_compress",
    )(cmp_in, pos_l, w1_l, w2_l)


def _chunk_update(carry, s, v_aug):
    m, acc = carry
    m_new = jnp.maximum(m, jnp.max(s, axis=-1, keepdims=True))
    p = jnp.exp(s - m_new).astype(BF16)
    acc = jnp.exp(m - m_new) * acc + _dot(p, v_aug)
    return m_new, acc


def _chunk_init(rows):
    return jnp.full((rows, 1), NEG, F32), jnp.zeros((rows, LANES), F32)


def _chunk_finish(carry):
    _, acc = carry
    return acc[:, :HEAD_DIM] / jnp.maximum(acc[:, HEAD_DIM:HEAD_DIM + 1], 1e-30)


def _add_rows(s, bias, rep):
    tq = bias.shape[0]
    return jnp.concatenate([s[r * tq:(r + 1) * tq] + bias for r in range(rep)], axis=0)


def _nsa_kernel(q_ref, kc_ref, vc_ref, ks_ref, vs_ref, kw_ref, vw_ref, gate_ref, ov_ref, o_ref, *, tq, tk):
    i = pl.program_id(1)
    rep = NSA_REP
    groups = range(NSA_GROUPS)
    t_row = i * tq + _iota((tq, 1), 0)
    t4 = jnp.concatenate([t_row] * rep, axis=0)
    n_c = kc_ref.shape[2]
    n_s = ks_ref.shape[2] // SLC_LEN
    cend = _iota((1, n_c), 1) * CMP_STRIDE + (CMP_LEN - 1)
    cm = cend <= t4
    j = _iota((tq, LANES), 1)
    tb = lax.shift_right_logical(t_row, SLC_LEN.bit_length() - 1)
    valid = j <= tb
    forced = (j == 0) | (j == tb) | (j == tb - 1)
    j_t = _iota((n_s, tq), 0)
    span = WIN + tq
    start = pl.multiple_of(jnp.maximum(i * tq - WIN, 0), tq)
    dist = t_row - (start + _iota((1, span), 1))
    band = jnp.where((dist >= 0) & (dist < WIN), 0.0, NEG)

    o_cmp, o_win, q4a = [], [], []
    for g in groups:
        q4 = jnp.concatenate([q_ref[0, rep * g + r] for r in range(rep)], axis=0)
        s = jnp.where(cm, _dot_nt(q4, kc_ref[0, g]), NEG)
        e = jnp.where(cm, jnp.exp(s - jnp.max(s, axis=-1, keepdims=True)), 0.0)
        p = e / jnp.maximum(jnp.sum(e, axis=-1, keepdims=True), 1e-30)
        o_cmp.append(_dot(p.astype(BF16), vc_ref[0, g]))
        psum = p[0:tq] + p[tq:2 * tq] + p[2 * tq:3 * tq] + p[3 * tq:4 * tq]
        imp = jnp.dot(psum, ov_ref[...], precision=HIGHEST, preferred_element_type=F32)
        val = jnp.where(valid, jnp.where(forced, jnp.inf, imp), -jnp.inf)
        v_t = val.T[:n_s]
        rank = jnp.zeros((n_s, tq), jnp.int32)
        for jp in range(n_s):
            row = v_t[jp:jp + 1, :]
            ahead = (row > v_t) | ((row == v_t) & (j_t > jp))
            rank = rank + ahead.astype(jnp.int32)
        neg_t = jnp.where((rank < SLC_TOPN) & (v_t > -jnp.inf), 0.0, NEG)
        neg = jnp.concatenate([neg_t, jnp.zeros((LANES - n_s, tq), F32)], axis=0).T[:, :HEAD_DIM]
        q4a.append(jnp.concatenate(
            [jnp.concatenate([q_ref[0, rep * g + r].astype(F32), neg], axis=1) for r in range(rep)],
            axis=0).astype(BF16))
        sw = _add_rows(_dot_nt(q4, kw_ref[0, g, pl.ds(start, span), :]), band, rep)
        pw = jnp.exp(sw - jnp.max(sw, axis=-1, keepdims=True)).astype(BF16)
        aw = _dot(pw, vw_ref[0, g, pl.ds(start, span), :])
        o_win.append(aw[:, :HEAD_DIM] / jnp.maximum(aw[:, HEAD_DIM:HEAD_DIM + 1], 1e-30))

    def slc_step(c, carries, diag):
        k0 = pl.multiple_of(c * tk, tk)
        new = []
        for g in groups:
            sc = _dot_nt(q4a[g], ks_ref[0, g, pl.ds(k0, tk), :])
            if diag:
                sc = _add_rows(sc, jnp.where(k0 + _iota((1, tk), 1) <= t_row, 0.0, NEG), rep)
            new.append(_chunk_update(carries[g], sc, vs_ref[0, g, pl.ds(k0, tk), :]))
        return tuple(new)

    last = i // (tk // tq)
    carries = lax.fori_loop(0, last, functools.partial(slc_step, diag=False),
                            tuple(_chunk_init(rep * tq) for _ in groups))
    carries = slc_step(last, carries, True)

    outs = []
    for g in groups:
        o_slc = _chunk_finish(carries[g])
        gates = gate_ref[0, g]
        for r in range(rep):
            rows = slice(r * tq, (r + 1) * tq)
            outs.append(gates[:, 3 * r:3 * r + 1] * o_cmp[g][rows]
                        + gates[:, 3 * r + 1:3 * r + 2] * o_slc[rows]
                        + gates[:, 3 * r + 2:3 * r + 3] * o_win[g][rows])
    o_ref[0] = jnp.concatenate(outs, axis=1).astype(o_ref.dtype)


def _nsa(qn, cmp_kv, nks, nkw, nv, gm, ov, *, tq=128, tk=512):
    b, _, seq, _ = qn.shape
    n_c = cmp_kv.shape[2]
    ng = NSA_GROUPS
    wide = lambda part: pl.BlockSpec((1, ng, seq, LANES), lambda bi, i: (bi, part, 0, 0))
    cmp_spec = lambda part: pl.BlockSpec((1, ng, n_c, HEAD_DIM), lambda bi, i: (bi, part, 0, 0))
    return pl.pallas_call(
        functools.partial(_nsa_kernel, tq=tq, tk=tk),
        grid=(b, seq // tq),
        in_specs=[pl.BlockSpec((1, NSA_HEADS, tq, HEAD_DIM), lambda bi, i: (bi, 0, i, 0)),
                  cmp_spec(0), cmp_spec(1),
                  wide(0), wide(0),
                  pl.BlockSpec((1, ng, seq, HEAD_DIM), lambda bi, i: (bi, 0, 0, 0)), wide(1),
                  pl.BlockSpec((1, ng, tq, LANES), lambda bi, i: (bi, 0, i, 0)),
                  pl.BlockSpec((n_c, LANES), lambda bi, i: (0, 0))],
        out_specs=pl.BlockSpec((1, tq, NSA_HEADS * HEAD_DIM), lambda bi, i: (bi, i, 0)),
        out_shape=jax.ShapeDtypeStruct((b, seq, NSA_HEADS * HEAD_DIM), BF16),
        compiler_params=_params("parallel", "arbitrary"),
        name="nsa",
    )(qn, cmp_kv, cmp_kv, nks, nv, nkw, nv, gm, ov)


def _decay_kernel(lf_ref, tri_ref, col_ref, row_ref):
    seq = lf_ref.shape[2]
    blk = tri_ref.shape[0]
    carry = jnp.zeros((1, LANES), F32)
    for c in range(seq // blk):
        rows = slice(c * blk, (c + 1) * blk)
        cs = jnp.dot(tri_ref[...], lf_ref[0, 0, rows, :], precision=HIGHEST,
                     preferred_element_type=F32) + carry
        col_ref[0, rows, :] = cs
        carry = cs[blk - 1:blk, :]
    row_ref[0] = col_ref[0].T[8:16, :]


def _decay(gm, tri):
    b, _, seq, _ = gm.shape
    blk = tri.shape[0]
    return pl.pallas_call(
        _decay_kernel,
        grid=(b,),
        in_specs=[pl.BlockSpec((1, 1, seq, LANES), lambda i: (i, 0, 0, 0)),
                  pl.BlockSpec((blk, blk), lambda i: (0, 0))],
        out_specs=[pl.BlockSpec((1, seq, LANES), lambda i: (i, 0, 0)),
                   pl.BlockSpec((1, 8, seq), lambda i: (i, 0, 0))],
        out_shape=[jax.ShapeDtypeStruct((b, seq, LANES), F32),
                   jax.ShapeDtypeStruct((b, 8, seq), F32)],
        compiler_params=_params("parallel"),
        name="fox_decay",
    )(gm, tri)


def _fox_kernel(q_ref, k_ref, v_ref, fcol_ref, frow_ref, tri_ref, o_ref, *, tq):
    i = pl.program_id(1)
    heads = range(FOX_HEADS)
    qs = [q_ref[0, h] for h in heads]
    fqs = [fcol_ref[0, :, GATE_LANES + h:GATE_LANES + h + 1] for h in heads]

    def step(c, carries, diag):
        k0 = pl.multiple_of(c * tq, tq)
        new = []
        for h in heads:
            fk = frow_ref[0, 4 + h:5 + h, pl.ds(k0, tq)]
            sc = _dot_nt(qs[h], k_ref[0, h, pl.ds(k0, tq), :]) + (fqs[h] - fk)
            if diag:
                sc = sc + tri_ref[...]
            new.append(_chunk_update(carries[h], sc, v_ref[0, h, pl.ds(k0, tq), :]))
        return tuple(new)

    carries = step(i, tuple(_chunk_init(tq) for _ in heads), True)
    carries = lax.fori_loop(0, i, functools.partial(step, diag=False), carries)
    o_ref[0] = jnp.concatenate([_chunk_finish(cr) for cr in carries], axis=1).astype(o_ref.dtype)


def _fox(fqk, fv, fcol, frow, tri, *, tq=FOX_TILE):
    b, _, seq, _ = fqk.shape
    nh = FOX_HEADS
    return pl.pallas_call(
        functools.partial(_fox_kernel, tq=tq),
        grid=(b, seq // tq),
        in_specs=[pl.BlockSpec((1, nh, tq, HEAD_DIM), lambda bi, i: (bi, 0, i, 0)),
                  pl.BlockSpec((1, nh, seq, HEAD_DIM), lambda bi, i: (bi, 1, 0, 0)),
                  pl.BlockSpec((1, nh, seq, LANES), lambda bi, i: (bi, 0, 0, 0)),
                  pl.BlockSpec((1, tq, LANES), lambda bi, i: (bi, i, 0)),
                  pl.BlockSpec((1, 8, seq), lambda bi, i: (bi, 0, 0)),
                  pl.BlockSpec((tq, tq), lambda bi, i: (0, 0))],
        out_specs=pl.BlockSpec((1, tq, nh * HEAD_DIM), lambda bi, i: (bi, i, 0)),
        out_shape=jax.ShapeDtypeStruct((b, seq, nh * HEAD_DIM), BF16),
        compiler_params=_params("parallel", "arbitrary"),
        name="fox",
    )(fqk, fqk, fv, fcol, frow, tri)


def _moba_kernel(q_ref, k_ref, v_ref, avg_ref, tri_ref, o_ref, km_ref, *, tq):
    i = pl.program_id(1)
    nb = avg_ref.shape[0]

    @pl.when(i == 0)
    def _():
        for h in range(MOBA_HEADS):
            km_ref[h] = _dot(avg_ref[...], k_ref[0, h])

    blk = _iota((nb, tq), 0)
    own = i * (tq // MOBA_BLOCK) + lax.shift_right_logical(_iota((nb, tq), 1), MOBA_BLOCK.bit_length() - 1)
    qas = []
    for h in range(MOBA_HEADS):
        q = q_ref[0, h]
        gate = _dot_nt(km_ref[h][:, :HEAD_DIM], q.astype(F32), precision=HIGHEST)
        val = jnp.where(blk < own, gate, -jnp.inf)
        rank = jnp.zeros((nb, tq), jnp.int32)
        for jp in range(nb):
            row = val[jp:jp + 1, :]
            ahead = (row > val) | ((row == val) & (blk > jp))
            rank = rank + ahead.astype(jnp.int32)
        keep = ((rank < MOBA_TOPK) & (blk < own)) | (blk == own)
        neg_t = jnp.where(keep, 0.0, NEG)
        neg = jnp.concatenate([neg_t, jnp.zeros((LANES - nb, tq), F32)], axis=0).T[:, :HEAD_DIM]
        qas.append(jnp.concatenate([q.astype(F32), neg], axis=1).astype(BF16))

    def step(c, carries, diag):
        k0 = pl.multiple_of(c * tq, tq)
        new = []
        for h in range(MOBA_HEADS):
            sc = _dot_nt(qas[h], k_ref[0, h, pl.ds(k0, tq), :])
            if diag:
                sc = sc + tri_ref[...]
            new.append(_chunk_update(carries[h], sc, v_ref[0, h, pl.ds(k0, tq), :]))
        return tuple(new)

    carries = step(i, tuple(_chunk_init(tq) for _ in range(MOBA_HEADS)), True)
    carries = lax.fori_loop(0, i, functools.partial(step, diag=False), carries)
    o_ref[0] = jnp.concatenate([_chunk_finish(cr) for cr in carries], axis=1).astype(o_ref.dtype)


def _moba(mq, mk, mv, avg, tri, *, tq=MOBA_TILE):
    b, _, seq, _ = mq.shape
    nh = MOBA_HEADS
    nb = avg.shape[0]
    return pl.pallas_call(
        functools.partial(_moba_kernel, tq=tq),
        grid=(b, seq // tq),
        in_specs=[pl.BlockSpec((1, nh, tq, HEAD_DIM), lambda bi, i: (bi, 0, i, 0)),
                  pl.BlockSpec((1, nh, seq, LANES), lambda bi, i: (bi, 0, 0, 0)),
                  pl.BlockSpec((1, nh, seq, LANES), lambda bi, i: (bi, 0, 0, 0)),
                  pl.BlockSpec((nb, seq), lambda bi, i: (0, 0)),
                  pl.BlockSpec((tq, tq), lambda bi, i: (0, 0))],
        out_specs=pl.BlockSpec((1, tq, nh * HEAD_DIM), lambda bi, i: (bi, i, 0)),
        out_shape=jax.ShapeDtypeStruct((b, seq, nh * HEAD_DIM), BF16),
        scratch_shapes=[pltpu.VMEM((nh, nb, LANES), F32)],
        compiler_params=_params("parallel", "arbitrary"),
        name="moba",
    )(mq, mk, mv, avg, tri)


def _outproj_kernel(x_ref, mod_ref, on_ref, of_ref, om_ref, w_ref, o_ref):
    n_w = NSA_HEADS * HEAD_DIM
    f_w = FOX_HEADS * HEAD_DIM
    acc = _dot(on_ref[...], w_ref[0:n_w, :])
    acc += _dot(of_ref[...], w_ref[n_w:n_w + f_w, :])
    acc += _dot(om_ref[...], w_ref[n_w + f_w:, :])
    o_ref[...] = x_ref[...] + mod_ref[0, 5:6, :] * acc


def _outproj(x2, mod, o_nsa, o_fox, o_moba, w_out, *, seq, tm=512):
    n, d = x2.shape
    per_b = seq // tm
    row = lambda width: pl.BlockSpec((tm, width), lambda i: (i, 0))
    return pl.pallas_call(
        _outproj_kernel,
        grid=(n // tm,),
        in_specs=[row(d), pl.BlockSpec((1, 9, d), lambda i: (i // per_b, 0, 0)),
                  row(o_nsa.shape[1]), row(o_fox.shape[1]), row(o_moba.shape[1]),
                  pl.BlockSpec((d, d), lambda i: (0, 0))],
        out_specs=row(d),
        out_shape=jax.ShapeDtypeStruct((n, d), F32),
        compiler_params=_params("parallel"),
        name="outproj",
    )(x2, mod, o_nsa, o_fox, o_moba, w_out)


def _pack_w_in(w_in):
    depth, d, _ = w_in.shape

    def cols(a, b):
        return w_in[:, :, a:b]

    zeros = lambda n: jnp.zeros((depth, d, n), w_in.dtype)
    g_half = 3 * NSA_REP
    parts = [cols(_OFF_NSA_Q, _OFF_GATE),
             cols(_OFF_FOX_Q, _OFF_FOX_F),
             cols(_OFF_MOBA_Q, IN_COLS),
             cols(_OFF_GATE, _OFF_GATE + g_half), cols(_OFF_FOX_F, _OFF_FOX_F + FOX_HEADS),
             zeros(LANES - g_half - FOX_HEADS),
             cols(_OFF_GATE + g_half, _OFF_GATE + 2 * g_half), zeros(LANES - g_half)]
    return jnp.concatenate(parts, axis=-1).astype(BF16)


def _constants(seq):
    n_c = seq // CMP_STRIDE
    n_s = seq // SLC_LEN
    c0 = np.arange(n_c)[:, None] * CMP_STRIDE
    j0 = np.arange(LANES)[None, :] * SLC_LEN
    real = (np.arange(n_c)[:, None] < (seq - CMP_LEN) // CMP_STRIDE + 1) & (np.arange(LANES)[None, :] < n_s)
    ov = ((c0 < j0 + SLC_LEN) & (c0 + CMP_LEN > j0) & real).astype(np.float32)
    nb = seq // MOBA_BLOCK
    avg = (np.arange(nb)[:, None] == (np.arange(seq)[None, :] // MOBA_BLOCK)).astype(np.float32) / MOBA_BLOCK
    tri = np.tril(np.ones((256, 256), np.float32))

    def causal_bias(n):
        return np.where(np.arange(n)[None, :] <= np.arange(n)[:, None], 0.0, NEG).astype(np.float32)

    r = np.arange(MOBA_TILE)
    same = (r[:, None] // MOBA_BLOCK) == (r[None, :] // MOBA_BLOCK)
    allowed = np.where(same, r[None, :] <= r[:, None], r[None, :] < r[:, None])
    causal_moba = np.where(allowed, 0.0, NEG).astype(np.float32)
    return dict(ov=jnp.asarray(ov), avg=jnp.asarray(avg, dtype=BF16), tri=jnp.asarray(tri),
                causal_fox=jnp.asarray(causal_bias(FOX_TILE)), causal_moba=jnp.asarray(causal_moba))


def kernel(x, c, positions, norm_g, w_ada, b_ada, w_in, fox_fbias, cmp_pos, cmp_w1, cmp_w2, w_out,
           ffn_w13, ffn_w2, final_g):
    batch, seq, d = x.shape
    depth = w_ada.shape[0]
    assert d == D_MODEL and seq % MOBA_BLOCK == 0 and seq % 1024 == 0

    w_in_p = _pack_w_in(w_in)
    w13 = ffn_w13.astype(BF16)
    w2 = ffn_w2.astype(BF16)
    w_o = w_out.astype(BF16)
    half = CMP_LEN * HEAD_DIM // 2
    cw1 = cmp_w1.astype(BF16).reshape(depth, 2, 2, half, CMP_HIDDEN)
    cw2 = cmp_w2.astype(BF16)
    cpos = cmp_pos.reshape(depth, 2, 2, 1, half)
    cst = _constants(seq)

    inv = ROPE_THETA ** (-jnp.arange(0, ROPE_DIM, 2, dtype=F32) / ROPE_DIM)
    lane = np.arange(LANES) % HEAD_DIM
    inv_lane = jnp.where(jnp.asarray(lane < ROPE_DIM), inv[jnp.asarray(lane % (ROPE_DIM // 2))], 0.0)
    inv_lane = inv_lane.reshape(1, LANES).astype(F32)
    fb_lane = jnp.zeros((depth, 1, LANES), F32).at[:, 0, GATE_LANES:GATE_LANES + FOX_HEADS].set(fox_fbias)

    mod_all = _ada_mod(c, w_ada, b_ada).reshape(depth, batch, 9, d)
    pos2 = positions.reshape(batch * seq, 1)
    x2 = x.reshape(batch * seq, d)
    fg = final_g.reshape(1, d)
    for l in range(depth):
        mod = mod_all[l]
        x2 = _ffn(x2, mod, norm_g[l, 0:1], fg, w13[l, 0], w2[l, 0], sub=0, seq=seq, final=False)
        qn, cmp_in, nks, nkw, nv, fqk, fv, mq, mk, mv, gm = _inproj(
            x2, mod, norm_g[l, 1:2], pos2, inv_lane, fb_lane[l], w_in_p[l], batch=batch, seq=seq)
        cmp_kv = _compress(cmp_in.reshape(batch, 4, seq // CMP_STRIDE, CMP_STRIDE * HEAD_DIM),
                           cpos[l], cw1[l], cw2[l])
        o_nsa = _nsa(qn, cmp_kv, nks, nkw, nv, gm, cst["ov"])
        fcol, frow = _decay(gm, cst["tri"])
        o_fox = _fox(fqk, fv, fcol, frow, cst["causal_fox"])
        o_moba = _moba(mq, mk, mv, cst["avg"], cst["causal_moba"])
        x2 = _outproj(x2, mod, o_nsa.reshape(batch * seq, -1), o_fox.reshape(batch * seq, -1),
                      o_moba.reshape(batch * seq, -1), w_o[l], seq=seq)
        x2 = _ffn(x2, mod, norm_g[l, 2:3], fg, w13[l, 1], w2[l, 1], sub=2, seq=seq,
                  final=(l == depth - 1), tm=512)
    return x2.reshape(batch, seq, d)
```

```python
import functools

import numpy as np
import jax
import jax.numpy as jnp
from jax import lax
from jax.experimental import pallas as pl
from jax.experimental.pallas import tpu as pltpu

D_MODEL = 1024
HEAD_DIM = 64
NSA_HEADS = 8
NSA_GROUPS = 2
NSA_REP = NSA_HEADS // NSA_GROUPS
FOX_HEADS = 4
MOBA_HEADS = 4
ROPE_DIM = HEAD_DIM // 4
ROPE_THETA = 500000.0
CMP_LEN = 32
CMP_STRIDE = 16
CMP_HIDDEN = 4 * HEAD_DIM
SLC_LEN = 64
SLC_TOPN = 8
WIN = 512
MOBA_BLOCK = 256
MOBA_TOPK = 3
D_FF = 2816
EPS = 1e-6
ATT_SCALE = HEAD_DIM ** -0.5

FOX_TILE = 512
MOBA_TILE = 2 * MOBA_BLOCK
LANES = 128
NEG = -1e30
VMEM_LIMIT = 56 * 1024 * 1024

F32 = jnp.float32
BF16 = jnp.bfloat16
HIGHEST = lax.Precision.HIGHEST

_KV = NSA_GROUPS * HEAD_DIM
_OFF_NSA_Q = 0
_OFF_KC = _OFF_NSA_Q + NSA_HEADS * HEAD_DIM
_OFF_VC = _OFF_KC + _KV
_OFF_KS = _OFF_VC + _KV
_OFF_VS = _OFF_KS + _KV
_OFF_KW = _OFF_VS + _KV
_OFF_VW = _OFF_KW + _KV
_OFF_GATE = _OFF_VW + _KV
_OFF_FOX_Q = _OFF_GATE + 3 * NSA_HEADS
_OFF_FOX_K = _OFF_FOX_Q + FOX_HEADS * HEAD_DIM
_OFF_FOX_V = _OFF_FOX_K + FOX_HEADS * HEAD_DIM
_OFF_FOX_F = _OFF_FOX_V + FOX_HEADS * HEAD_DIM
_OFF_MOBA_Q = _OFF_FOX_F + FOX_HEADS
_OFF_MOBA_K = _OFF_MOBA_Q + MOBA_HEADS * HEAD_DIM
_OFF_MOBA_V = _OFF_MOBA_K + MOBA_HEADS * HEAD_DIM
IN_COLS = _OFF_MOBA_V + MOBA_HEADS * HEAD_DIM

N_HEAD_COLS = 2816
GATE_LANES = 3 * NSA_REP
PROJ_COLS = N_HEAD_COLS + 2 * LANES


def _dot(a, b):
    return jnp.dot(a, b, preferred_element_type=F32)


def _dot_nt(a, b, precision=None):
    return lax.dot_general(a, b, (((1,), (1,)), ((), ())), precision=precision,
                           preferred_element_type=F32)


def _iota(shape, dim):
    return lax.broadcasted_iota(jnp.int32, shape, dim)


def _params(*sem, flags=None):
    return pltpu.CompilerParams(dimension_semantics=sem, vmem_limit_bytes=VMEM_LIMIT, flags=flags)


def _ada_norm(x, g, shift, scale):
    ms = jnp.mean(x * x, axis=-1, keepdims=True)
    y = x * lax.rsqrt(ms + EPS) * g
    return y * (1.0 + scale) + shift


def _ada_kernel(c_ref, w_ref, b_ref, o_ref):
    c = c_ref[...]
    ca = c * jax.nn.sigmoid(c)
    o_ref[0] = jnp.dot(ca, w_ref[0], precision=HIGHEST, preferred_element_type=F32) + b_ref[0]


def _ada_mod(c, w_ada, b_ada):
    depth, d, n = w_ada.shape
    b = c.shape[0]
    tn = 1152
    return pl.pallas_call(
        _ada_kernel,
        grid=(depth, n // tn),
        in_specs=[pl.BlockSpec((b, d), lambda l, j: (0, 0)),
                  pl.BlockSpec((1, d, tn), lambda l, j: (l, 0, j)),
                  pl.BlockSpec((1, 1, tn), lambda l, j: (l, 0, j))],
        out_specs=pl.BlockSpec((1, b, tn), lambda l, j: (l, 0, j)),
        out_shape=jax.ShapeDtypeStruct((depth, b, n), F32),
        compiler_params=_params("parallel", "parallel"),
        name="ada_mod",
    )(c, w_ada, b_ada.reshape(depth, 1, n))


def _ffn_kernel(x_ref, mod_ref, g_ref, fg_ref, wa_ref, wb_ref, w2_ref, o_ref, h_ref, *, sub, coef, final):
    j = pl.program_id(1)

    @pl.when(j == 0)
    def _():
        h = _ada_norm(x_ref[...], g_ref[...], mod_ref[0, 3 * sub:3 * sub + 1, :],
                      mod_ref[0, 3 * sub + 1:3 * sub + 2, :])
        h_ref[...] = h.astype(BF16)

    h = h_ref[...]
    a = _dot(h, wa_ref[...])
    b = _dot(h, wb_ref[...])
    u = (a * jax.nn.sigmoid(a) * b).astype(BF16)
    part = _dot(u, w2_ref[...])

    @pl.when(j == 0)
    def _():
        o_ref[...] = part

    @pl.when(j == 1)
    def _():
        gate = mod_ref[0, 3 * sub + 2:3 * sub + 3, :]
        y = x_ref[...] + (coef * gate) * (o_ref[...] + part)
        if final:
            y = y * lax.rsqrt(jnp.mean(y * y, axis=-1, keepdims=True) + EPS) * fg_ref[...]
        o_ref[...] = y


def _ffn_res_kernel(x_ref, mod_ref, g_ref, fg_ref, w13_ref, w2_ref, o_ref, *, sub, coef, final, splits):
    x = x_ref[...]
    h = _ada_norm(x, g_ref[...], mod_ref[0, 3 * sub:3 * sub + 1, :],
                  mod_ref[0, 3 * sub + 1:3 * sub + 2, :]).astype(BF16)
    acc = None
    for lo, hi in splits:
        a = _dot(h, w13_ref[:, lo:hi])
        b = _dot(h, w13_ref[:, D_FF + lo:D_FF + hi])
        u = (a * jax.nn.sigmoid(a) * b).astype(BF16)
        part = _dot(u, w2_ref[lo:hi, :])
        acc = part if acc is None else acc + part
    y = x + (coef * mod_ref[0, 3 * sub + 2:3 * sub + 3, :]) * acc
    if final:
        y = y * lax.rsqrt(jnp.mean(y * y, axis=-1, keepdims=True) + EPS) * fg_ref[...]
    o_ref[...] = y


def _ffn_res(x2, mod, g, final_g, w13, w2, *, sub, seq, final, tm, chunk):
    n, d = x2.shape
    per_b = seq // tm
    splits = tuple((lo, min(lo + chunk, D_FF)) for lo in range(0, D_FF, chunk))
    resident = lambda shape: pl.BlockSpec(shape, lambda i: (0, 0), pipeline_mode=pl.Buffered(1))
    return pl.pallas_call(
        functools.partial(_ffn_res_kernel, sub=sub, coef=0.5, final=final, splits=splits),
        grid=(n // tm,),
        in_specs=[pl.BlockSpec((tm, d), lambda i: (i, 0)),
                  pl.BlockSpec((1, 9, d), lambda i: (i // per_b, 0, 0)),
                  pl.BlockSpec((1, d), lambda i: (0, 0)),
                  pl.BlockSpec((1, d), lambda i: (0, 0)),
                  resident((d, 2 * D_FF)), resident((D_FF, d))],
        out_specs=pl.BlockSpec((tm, d), lambda i: (i, 0)),
        out_shape=jax.ShapeDtypeStruct((n, d), F32),
        compiler_params=_params("parallel"),
        name="ffn_res",
    )(x2, mod, g, final_g, w13, w2)


def _ffn(x2, mod, g, final_g, w13, w2, *, sub, seq, final, tm=1024):
    n, d = x2.shape
    nf = 2
    tf = D_FF // nf
    per_b = seq // tm
    return pl.pallas_call(
        functools.partial(_ffn_kernel, sub=sub, coef=0.5, final=final),
        grid=(n // tm, nf),
        in_specs=[pl.BlockSpec((tm, d), lambda i, j: (i, 0)),
                  pl.BlockSpec((1, 9, d), lambda i, j: (i // per_b, 0, 0)),
                  pl.BlockSpec((1, d), lambda i, j: (0, 0)),
                  pl.BlockSpec((1, d), lambda i, j: (0, 0)),
                  pl.BlockSpec((d, tf), lambda i, j: (0, j)),
                  pl.BlockSpec((d, tf), lambda i, j: (0, nf + j)),
                  pl.BlockSpec((tf, d), lambda i, j: (j, 0))],
        out_specs=pl.BlockSpec((tm, d), lambda i, j: (i, 0)),
        out_shape=jax.ShapeDtypeStruct((n, d), F32),
        scratch_shapes=[pltpu.VMEM((tm, d), BF16)],
        compiler_params=_params("parallel", "arbitrary"),
        name="ffn",
    )(x2, mod, g, final_g, w13, w13, w2)


def _inproj_kernel(x_ref, mod_ref, g_ref, pos_ref, inv_ref, fb_ref, w_ref,
                   qn_ref, cmp_ref, nks_ref, nkw_ref, nv_ref, fqk_ref, fv_ref, mq_ref, mk_ref, mv_ref, gm_ref,
                   *, per_b):
    h = _ada_norm(x_ref[...], g_ref[...], mod_ref[0, 3:4, :], mod_ref[0, 4:5, :]).astype(BF16)
    tm = h.shape[0]
    seq_pos = (pl.program_id(0) % per_b) * tm + _iota((tm, LANES), 0)

    ang = pos_ref[...].astype(F32) * inv_ref[...]
    cosl = jnp.cos(ang)
    sinl = jnp.sin(ang)
    lane = _iota((tm, LANES), 1)
    hl = lane & (HEAD_DIM - 1)
    sin_lo = jnp.where(hl < ROPE_DIM // 2, -sinl, 0.0)
    sin_hi = jnp.where(hl >= ROPE_DIM // 2, sinl, 0.0)
    one_hot = jnp.where(lane == HEAD_DIM, 1.0, 0.0)

    def rope(y):
        return (y * cosl + pltpu.roll(y, LANES - ROPE_DIM // 2, 1) * sin_lo
                + pltpu.roll(y, ROPE_DIM // 2, 1) * sin_hi)

    def put_heads(z, out_ref, head0, dtype):
        z = z.astype(dtype)
        out_ref[0, head0] = z[:, :HEAD_DIM]
        out_ref[0, head0 + 1] = z[:, HEAD_DIM:]

    def put_wide(z, out_ref, head0, tail):
        out_ref[0, head0] = jnp.where(lane < HEAD_DIM, z, tail).astype(BF16)
        out_ref[0, head0 + 1] = jnp.where(lane < HEAD_DIM, pltpu.roll(z, HEAD_DIM, 1), tail).astype(BF16)

    def put_values(z, out_ref, head0):
        put_wide(z, out_ref, head0, one_hot)

    def put_keys(z, out_ref, head0, block_len):
        block = lax.shift_right_logical(seq_pos, block_len.bit_length() - 1)
        put_wide(z, out_ref, head0, jnp.where(lane - HEAD_DIM == block, 1.0, 0.0))

    def slab(c0):
        y = _dot(h, w_ref[:, c0:c0 + 2 * LANES])
        return y[:, :LANES], y[:, LANES:]

    for s_idx in range(2):
        a, b = slab(256 * s_idx)
        put_heads(rope(a) * ATT_SCALE, qn_ref, 4 * s_idx, BF16)
        put_heads(rope(b) * ATT_SCALE, qn_ref, 4 * s_idx + 2, BF16)
    a, b = slab(512)
    put_heads(rope(a), cmp_ref, 0, F32)
    put_heads(b, cmp_ref, 2, F32)
    a, b = slab(768)
    put_keys(rope(a), nks_ref, 0, SLC_LEN)
    put_values(b, nv_ref, 0)
    a, b = slab(1024)
    put_heads(rope(a), nkw_ref, 0, BF16)
    put_values(b, nv_ref, 2)
    a, b = slab(1280)
    put_heads(a * ATT_SCALE, fqk_ref, 0, BF16)
    put_heads(b * ATT_SCALE, fqk_ref, 2, BF16)
    a, b = slab(1536)
    put_heads(a, fqk_ref, 4, BF16)
    put_heads(b, fqk_ref, 6, BF16)
    a, b = slab(1792)
    put_values(a, fv_ref, 0)
    put_values(b, fv_ref, 2)
    a, b = slab(2048)
    put_heads(rope(a) * ATT_SCALE, mq_ref, 0, BF16)
    put_heads(rope(b) * ATT_SCALE, mq_ref, 2, BF16)
    a, b = slab(2304)
    put_keys(rope(a), mk_ref, 0, MOBA_BLOCK)
    put_keys(rope(b), mk_ref, 2, MOBA_BLOCK)
    a, b = slab(2560)
    put_values(a, mv_ref, 0)
    put_values(b, mv_ref, 2)
    a, b = slab(N_HEAD_COLS)
    gm_ref[0, 0] = jnp.where(lane < GATE_LANES, jax.nn.sigmoid(a), jax.nn.log_sigmoid(a + fb_ref[...]))
    gm_ref[0, 1] = jax.nn.sigmoid(b)


def _inproj(x2, mod, g, pos2, inv_lane, fb_lane, w_p, *, batch, seq, tm=512):
    n, d = x2.shape
    per_b = seq // tm

    def hm(nh, dtype, width=HEAD_DIM):
        return (pl.BlockSpec((1, nh, tm, width), lambda i: (i // per_b, 0, i % per_b, 0)),
                jax.ShapeDtypeStruct((batch, nh, seq, width), dtype))

    specs = [hm(NSA_HEADS, BF16), hm(4, F32), hm(2, BF16, LANES), hm(2, BF16), hm(4, BF16, LANES),
             hm(8, BF16), hm(4, BF16, LANES),
             hm(4, BF16), hm(4, BF16, LANES), hm(4, BF16, LANES), hm(2, F32, LANES)]
    return pl.pallas_call(
        functools.partial(_inproj_kernel, per_b=per_b),
        grid=(n // tm,),
        in_specs=[pl.BlockSpec((tm, d), lambda i: (i, 0)),
                  pl.BlockSpec((1, 9, d), lambda i: (i // per_b, 0, 0)),
                  pl.BlockSpec((1, d), lambda i: (0, 0)),
                  pl.BlockSpec((tm, 1), lambda i: (i, 0)),
                  pl.BlockSpec((1, LANES), lambda i: (0, 0)),
                  pl.BlockSpec((1, LANES), lambda i: (0, 0)),
                  pl.BlockSpec((d, PROJ_COLS), lambda i: (0, 0))],
        out_specs=[s for s, _ in specs],
        out_shape=[s for _, s in specs],
        compiler_params=_params("parallel"),
        name="inproj",
    )(x2, mod, g, pos2, inv_lane, fb_lane, w_p)


def _compress_kernel(z_ref, pos_ref, w1_ref, w2_ref, o_ref):
    z = z_ref[0, 0]
    top = _dot((z + pos_ref[0, 0]).astype(BF16), w1_ref[0, 0])
    bot = _dot((z + pos_ref[0, 1]).astype(BF16), w1_ref[0, 1])
    nch = z.shape[0]
    hid = top + pltpu.roll(bot, nch - 1, 0)
    act = jax.nn.gelu(hid).astype(BF16)
    o_ref[0, 0] = _dot(act, w2_ref[0]).astype(o_ref.dtype)


def _compress(cmp_in, pos_l, w1_l, w2_l):
    b, four, nch, width = cmp_in.shape
    return pl.pallas_call(
        _compress_kernel,
        grid=(b, four),
        in_specs=[pl.BlockSpec((1, 1, nch, width), lambda i, n: (i, n, 0, 0)),
                  pl.BlockSpec((1, 2, 1, width), lambda i, n: (n // 2, 0, 0, 0)),
                  pl.BlockSpec((1, 2, width, CMP_HIDDEN), lambda i, n: (n // 2, 0, 0, 0)),
                  pl.BlockSpec((1, CMP_HIDDEN, HEAD_DIM), lambda i, n: (n // 2, 0, 0))],
        out_specs=pl.BlockSpec((1, 1, nch, HEAD_DIM), lambda i, n: (i, n, 0, 0)),
        out_shape=jax.ShapeDtypeStruct((b, four, nch, HEAD_DIM), BF16),
        compiler_params=_params("parallel", "parallel"),
        name="nsa_compress",
    )(cmp_in, pos_l, w1_l, w2_l)


def _chunk_update(carry, s, v_aug):
    m, acc = carry
    m_new = jnp.maximum(m, jnp.max(s, axis=-1, keepdims=True))
    p = jnp.exp(s - m_new).astype(BF16)
    acc = jnp.exp(m - m_new) * acc + _dot(p, v_aug)
    return m_new, acc


def _chunk_init(rows):
    return jnp.full((rows, 1), NEG, F32), jnp.zeros((rows, LANES), F32)


def _chunk_finish(carry):
    _, acc = carry
    return acc[:, :HEAD_DIM] / jnp.maximum(acc[:, HEAD_DIM:HEAD_DIM + 1], 1e-30)


def _add_rows(s, bias, rep):
    tq = bias.shape[0]
    return jnp.concatenate([s[r * tq:(r + 1) * tq] + bias for r in range(rep)], axis=0)


def _nsa_kernel(q_ref, kc_ref, vc_ref, ks_ref, vs_ref, kw_ref, vw_ref, gate_ref, ov_ref, o_ref, *, tq, tk):
    i = pl.program_id(1)
    rep = NSA_REP
    groups = range(NSA_GROUPS)
    t_row = i * tq + _iota((tq, 1), 0)
    t4 = jnp.concatenate([t_row] * rep, axis=0)
    n_c = kc_ref.shape[2]
    n_s = ks_ref.shape[2] // SLC_LEN
    cend = _iota((1, n_c), 1) * CMP_STRIDE + (CMP_LEN - 1)
    cm = cend <= t4
    j = _iota((tq, LANES), 1)
    tb = lax.shift_right_logical(t_row, SLC_LEN.bit_length() - 1)
    valid = j <= tb
    forced = (j == 0) | (j == tb) | (j == tb - 1)
    j_t = _iota((n_s, tq), 0)
    span = WIN + tq
    start = pl.multiple_of(jnp.maximum(i * tq - WIN, 0), tq)
    dist = t_row - (start + _iota((1, span), 1))
    band = jnp.where((dist >= 0) & (dist < WIN), 0.0, NEG)

    o_cmp, o_win, q4a = [], [], []
    for g in groups:
        q4 = jnp.concatenate([q_ref[0, rep * g + r] for r in range(rep)], axis=0)
        s = jnp.where(cm, _dot_nt(q4, kc_ref[0, g]), NEG)
        e = jnp.where(cm, jnp.exp(s - jnp.max(s, axis=-1, keepdims=True)), 0.0)
        p = e / jnp.maximum(jnp.sum(e, axis=-1, keepdims=True), 1e-30)
        o_cmp.append(_dot(p.astype(BF16), vc_ref[0, g]))
        psum = p[0:tq] + p[tq:2 * tq] + p[2 * tq:3 * tq] + p[3 * tq:4 * tq]
        imp = jnp.dot(psum, ov_ref[...], precision=HIGHEST, preferred_element_type=F32)
        val = jnp.where(valid, jnp.where(forced, jnp.inf, imp), -jnp.inf)
        v_t = val.T[:n_s]
        rank = jnp.zeros((n_s, tq), jnp.int32)
        for jp in range(n_s):
            row = v_t[jp:jp + 1, :]
            ahead = (row > v_t) | ((row == v_t) & (j_t > jp))
            rank = rank + ahead.astype(jnp.int32)
        neg_t = jnp.where((rank < SLC_TOPN) & (v_t > -jnp.inf), 0.0, NEG)
        neg = jnp.concatenate([neg_t, jnp.zeros((LANES - n_s, tq), F32)], axis=0).T[:, :HEAD_DIM]
        q4a.append(jnp.concatenate(
            [jnp.concatenate([q_ref[0, rep * g + r].astype(F32), neg], axis=1) for r in range(rep)],
            axis=0).astype(BF16))
        sw = _add_rows(_dot_nt(q4, kw_ref[0, g, pl.ds(start, span), :]), band, rep)
        pw = jnp.exp(sw - jnp.max(sw, axis=-1, keepdims=True)).astype(BF16)
        aw = _dot(pw, vw_ref[0, g, pl.ds(start, span), :])
        o_win.append(aw[:, :HEAD_DIM] / jnp.maximum(aw[:, HEAD_DIM:HEAD_DIM + 1], 1e-30))

    def slc_step(c, carries, diag):
        k0 = pl.multiple_of(c * tk, tk)
        new = []
        for g in groups:
            sc = _dot_nt(q4a[g], ks_ref[0, g, pl.ds(k0, tk), :])
            if diag:
                sc = _add_rows(sc, jnp.where(k0 + _iota((1, tk), 1) <= t_row, 0.0, NEG), rep)
            new.append(_chunk_update(carries[g], sc, vs_ref[0, g, pl.ds(k0, tk), :]))
        return tuple(new)

    last = i // (tk // tq)
    carries = lax.fori_loop(0, last, functools.partial(slc_step, diag=False),
                            tuple(_chunk_init(rep * tq) for _ in groups))
    carries = slc_step(last, carries, True)

    outs = []
    for g in groups:
        o_slc = _chunk_finish(carries[g])
        gates = gate_ref[0, g]
        for r in range(rep):
            rows = slice(r * tq, (r + 1) * tq)
            outs.append(gates[:, 3 * r:3 * r + 1] * o_cmp[g][rows]
                        + gates[:, 3 * r + 1:3 * r + 2] * o_slc[rows]
                        + gates[:, 3 * r + 2:3 * r + 3] * o_win[g][rows])
    o_ref[0] = jnp.concatenate(outs, axis=1).astype(o_ref.dtype)


def _nsa(qn, cmp_kv, nks, nkw, nv, gm, ov, *, tq=128, tk=512):
    b, _, seq, _ = qn.shape
    n_c = cmp_kv.shape[2]
    ng = NSA_GROUPS
    wide = lambda part: pl.BlockSpec((1, ng, seq, LANES), lambda bi, i: (bi, part, 0, 0))
    cmp_spec = lambda part: pl.BlockSpec((1, ng, n_c, HEAD_DIM), lambda bi, i: (bi, part, 0, 0))
    return pl.pallas_call(
        functools.partial(_nsa_kernel, tq=tq, tk=tk),
        grid=(b, seq // tq),
        in_specs=[pl.BlockSpec((1, NSA_HEADS, tq, HEAD_DIM), lambda bi, i: (bi, 0, i, 0)),
                  cmp_spec(0), cmp_spec(1),
                  wide(0), wide(0),
                  pl.BlockSpec((1, ng, seq, HEAD_DIM), lambda bi, i: (bi, 0, 0, 0)), wide(1),
                  pl.BlockSpec((1, ng, tq, LANES), lambda bi, i: (bi, 0, i, 0)),
                  pl.BlockSpec((n_c, LANES), lambda bi, i: (0, 0))],
        out_specs=pl.BlockSpec((1, tq, NSA_HEADS * HEAD_DIM), lambda bi, i: (bi, i, 0)),
        out_shape=jax.ShapeDtypeStruct((b, seq, NSA_HEADS * HEAD_DIM), BF16),
        compiler_params=_params("parallel", "arbitrary"),
        name="nsa",
    )(qn, cmp_kv, cmp_kv, nks, nv, nkw, nv, gm, ov)


def _decay_kernel(lf_ref, tri_ref, col_ref, row_ref):
    seq = lf_ref.shape[2]
    blk = tri_ref.shape[0]
    carry = jnp.zeros((1, LANES), F32)
    for c in range(seq // blk):
        rows = slice(c * blk, (c + 1) * blk)
        cs = jnp.dot(tri_ref[...], lf_ref[0, 0, rows, :], precision=HIGHEST,
                     preferred_element_type=F32) + carry
        col_ref[0, rows, :] = cs
        carry = cs[blk - 1:blk, :]
    row_ref[0] = col_ref[0].T[8:16, :]


def _decay(gm, tri):
    b, _, seq, _ = gm.shape
    blk = tri.shape[0]
    return pl.pallas_call(
        _decay_kernel,
        grid=(b,),
        in_specs=[pl.BlockSpec((1, 1, seq, LANES), lambda i: (i, 0, 0, 0)),
                  pl.BlockSpec((blk, blk), lambda i: (0, 0))],
        out_specs=[pl.BlockSpec((1, seq, LANES), lambda i: (i, 0, 0)),
                   pl.BlockSpec((1, 8, seq), lambda i: (i, 0, 0))],
        out_shape=[jax.ShapeDtypeStruct((b, seq, LANES), F32),
                   jax.ShapeDtypeStruct((b, 8, seq), F32)],
        compiler_params=_params("parallel"),
        name="fox_decay",
    )(gm, tri)


def _fox_kernel(q_ref, k_ref, v_ref, fcol_ref, frow_ref, tri_ref, o_ref, *, tq):
    i = pl.program_id(1)
    heads = range(FOX_HEADS)
    qs = [q_ref[0, h] for h in heads]
    fqs = [fcol_ref[0, :, GATE_LANES + h:GATE_LANES + h + 1] for h in heads]

    def step(c, carries, diag):
        k0 = pl.multiple_of(c * tq, tq)
        new = []
        for h in heads:
            fk = frow_ref[0, 4 + h:5 + h, pl.ds(k0, tq)]
            sc = _dot_nt(qs[h], k_ref[0, h, pl.ds(k0, tq), :]) + (fqs[h] - fk)
            if diag:
                sc = sc + tri_ref[...]
            new.append(_chunk_update(carries[h], sc, v_ref[0, h, pl.ds(k0, tq), :]))
        return tuple(new)

    carries = step(i, tuple(_chunk_init(tq) for _ in heads), True)
    carries = lax.fori_loop(0, i, functools.partial(step, diag=False), carries)
    o_ref[0] = jnp.concatenate([_chunk_finish(cr) for cr in carries], axis=1).astype(o_ref.dtype)


def _fox(fqk, fv, fcol, frow, tri, *, tq=FOX_TILE):
    b, _, seq, _ = fqk.shape
    nh = FOX_HEADS
    return pl.pallas_call(
        functools.partial(_fox_kernel, tq=tq),
        grid=(b, seq // tq),
        in_specs=[pl.BlockSpec((1, nh, tq, HEAD_DIM), lambda bi, i: (bi, 0, i, 0)),
                  pl.BlockSpec((1, nh, seq, HEAD_DIM), lambda bi, i: (bi, 1, 0, 0)),
                  pl.BlockSpec((1, nh, seq, LANES), lambda bi, i: (bi, 0, 0, 0)),
                  pl.BlockSpec((1, tq, LANES), lambda bi, i: (bi, i, 0)),
                  pl.BlockSpec((1, 8, seq), lambda bi, i: (bi, 0, 0)),
                  pl.BlockSpec((tq, tq), lambda bi, i: (0, 0))],
        out_specs=pl.BlockSpec((1, tq, nh * HEAD_DIM), lambda bi, i: (bi, i, 0)),
        out_shape=jax.ShapeDtypeStruct((b, seq, nh * HEAD_DIM), BF16),
        compiler_params=_params("parallel", "arbitrary"),
        name="fox",
    )(fqk, fqk, fv, fcol, frow, tri)


def _moba_kernel(q_ref, k_ref, v_ref, avg_ref, tri_ref, o_ref, km_ref, *, tq):
    i = pl.program_id(1)
    nb = avg_ref.shape[0]

    @pl.when(i == 0)
    def _():
        for h in range(MOBA_HEADS):
            km_ref[h] = _dot(avg_ref[...], k_ref[0, h])

    blk = _iota((nb, tq), 0)
    own = i * (tq // MOBA_BLOCK) + lax.shift_right_logical(_iota((nb, tq), 1), MOBA_BLOCK.bit_length() - 1)
    qas = []
    for h in range(MOBA_HEADS):
        q = q_ref[0, h]
        gate = _dot_nt(km_ref[h][:, :HEAD_DIM], q.astype(F32), precision=HIGHEST)
        val = jnp.where(blk < own, gate, -jnp.inf)
        rank = jnp.zeros((nb, tq), jnp.int32)
        for jp in range(nb):
            row = val[jp:jp + 1, :]
            ahead = (row > val) | ((row == val) & (blk > jp))
            rank = rank + ahead.astype(jnp.int32)
        keep = ((rank < MOBA_TOPK) & (blk < own)) | (blk == own)
        neg_t = jnp.where(keep, 0.0, NEG)
        neg = jnp.concatenate([neg_t, jnp.zeros((LANES - nb, tq), F32)], axis=0).T[:, :HEAD_DIM]
        qas.append(jnp.concatenate([q.astype(F32), neg], axis=1).astype(BF16))

    def step(c, carries, diag):
        k0 = pl.multiple_of(c * tq, tq)
        new = []
        for h in range(MOBA_HEADS):
            sc = _dot_nt(qas[h], k_ref[0, h, pl.ds(k0, tq), :])
            if diag:
                sc = sc + tri_ref[...]
            new.append(_chunk_update(carries[h], sc, v_ref[0, h, pl.ds(k0, tq), :]))
        return tuple(new)

    carries = step(i, tuple(_chunk_init(tq) for _ in range(MOBA_HEADS)), True)
    carries = lax.fori_loop(0, i, functools.partial(step, diag=False), carries)
    o_ref[0] = jnp.concatenate([_chunk_finish(cr) for cr in carries], axis=1).astype(o_ref.dtype)


def _moba(mq, mk, mv, avg, tri, *, tq=MOBA_TILE):
    b, _, seq, _ = mq.shape
    nh = MOBA_HEADS
    nb = avg.shape[0]
    return pl.pallas_call(
        functools.partial(_moba_kernel, tq=tq),
        grid=(b, seq // tq),
        in_specs=[pl.BlockSpec((1, nh, tq, HEAD_DIM), lambda bi, i: (bi, 0, i, 0)),
                  pl.BlockSpec((1, nh, seq, LANES), lambda bi, i: (bi, 0, 0, 0)),
                  pl.BlockSpec((1, nh, seq, LANES), lambda bi, i: (bi, 0, 0, 0)),
                  pl.BlockSpec((nb, seq), lambda bi, i: (0, 0)),
                  pl.BlockSpec((tq, tq), lambda bi, i: (0, 0))],
        out_specs=pl.BlockSpec((1, tq, nh * HEAD_DIM), lambda bi, i: (bi, i, 0)),
        out_shape=jax.ShapeDtypeStruct((b, seq, nh * HEAD_DIM), BF16),
        scratch_shapes=[pltpu.VMEM((nh, nb, LANES), F32)],
        compiler_params=_params("parallel", "arbitrary"),
        name="moba",
    )(mq, mk, mv, avg, tri)


def _outproj_kernel(x_ref, mod_ref, on_ref, of_ref, om_ref, w_ref, o_ref):
    n_w = NSA_HEADS * HEAD_DIM
    f_w = FOX_HEADS * HEAD_DIM
    acc = _dot(on_ref[...], w_ref[0:n_w, :])
    acc += _dot(of_ref[...], w_ref[n_w:n_w + f_w, :])
    acc += _dot(om_ref[...], w_ref[n_w + f_w:, :])
    o_ref[...] = x_ref[...] + mod_ref[0, 5:6, :] * acc


def _outproj(x2, mod, o_nsa, o_fox, o_moba, w_out, *, seq, tm=512):
    n, d = x2.shape
    per_b = seq // tm
    row = lambda width: pl.BlockSpec((tm, width), lambda i: (i, 0))
    return pl.pallas_call(
        _outproj_kernel,
        grid=(n // tm,),
        in_specs=[row(d), pl.BlockSpec((1, 9, d), lambda i: (i // per_b, 0, 0)),
                  row(o_nsa.shape[1]), row(o_fox.shape[1]), row(o_moba.shape[1]),
                  pl.BlockSpec((d, d), lambda i: (0, 0))],
        out_specs=row(d),
        out_shape=jax.ShapeDtypeStruct((n, d), F32),
        compiler_params=_params("parallel"),
        name="outproj",
    )(x2, mod, o_nsa, o_fox, o_moba, w_out)


def _pack_w_in(w_in):
    depth, d, _ = w_in.shape

    def cols(a, b):
        return w_in[:, :, a:b]

    zeros = lambda n: jnp.zeros((depth, d, n), w_in.dtype)
    g_half = 3 * NSA_REP
    parts = [cols(_OFF_NSA_Q, _OFF_GATE),
             cols(_OFF_FOX_Q, _OFF_FOX_F),
             cols(_OFF_MOBA_Q, IN_COLS),
             cols(_OFF_GATE, _OFF_GATE + g_half), cols(_OFF_FOX_F, _OFF_FOX_F + FOX_HEADS),
             zeros(LANES - g_half - FOX_HEADS),
             cols(_OFF_GATE + g_half, _OFF_GATE + 2 * g_half), zeros(LANES - g_half)]
    return jnp.concatenate(parts, axis=-1).astype(BF16)


def _constants(seq):
    n_c = seq // CMP_STRIDE
    n_s = seq // SLC_LEN
    c0 = np.arange(n_c)[:, None] * CMP_STRIDE
    j0 = np.arange(LANES)[None, :] * SLC_LEN
    real = (np.arange(n_c)[:, None] < (seq - CMP_LEN) // CMP_STRIDE + 1) & (np.arange(LANES)[None, :] < n_s)
    ov = ((c0 < j0 + SLC_LEN) & (c0 + CMP_LEN > j0) & real).astype(np.float32)
    nb = seq // MOBA_BLOCK
    avg = (np.arange(nb)[:, None] == (np.arange(seq)[None, :] // MOBA_BLOCK)).astype(np.float32) / MOBA_BLOCK
    tri = np.tril(np.ones((256, 256), np.float32))

    def causal_bias(n):
        return np.where(np.arange(n)[None, :] <= np.arange(n)[:, None], 0.0, NEG).astype(np.float32)

    r = np.arange(MOBA_TILE)
    same = (r[:, None] // MOBA_BLOCK) == (r[None, :] // MOBA_BLOCK)
    allowed = np.where(same, r[None, :] <= r[:, None], r[None, :] < r[:, None])
    causal_moba = np.where(allowed, 0.0, NEG).astype(np.float32)
    return dict(ov=jnp.asarray(ov), avg=jnp.asarray(avg, dtype=BF16), tri=jnp.asarray(tri),
                causal_fox=jnp.asarray(causal_bias(FOX_TILE)), causal_moba=jnp.asarray(causal_moba))


def kernel(x, c, positions, norm_g, w_ada, b_ada, w_in, fox_fbias, cmp_pos, cmp_w1, cmp_w2, w_out,
           ffn_w13, ffn_w2, final_g):
    batch, seq, d = x.shape
    depth = w_ada.shape[0]
    assert d == D_MODEL and seq % MOBA_BLOCK == 0 and seq % 1024 == 0

    w_in_p = _pack_w_in(w_in)
    w13 = ffn_w13.astype(BF16)
    w2 = ffn_w2.astype(BF16)
    w_o = w_out.astype(BF16)
    half = CMP_LEN * HEAD_DIM // 2
    cw1 = cmp_w1.astype(BF16).reshape(depth, 2, 2, half, CMP_HIDDEN)
    cw2 = cmp_w2.astype(BF16)
    cpos = cmp_pos.reshape(depth, 2, 2, 1, half)
    cst = _constants(seq)

    inv = ROPE_THETA ** (-jnp.arange(0, ROPE_DIM, 2, dtype=F32) / ROPE_DIM)
    lane = np.arange(LANES) % HEAD_DIM
    inv_lane = jnp.where(jnp.asarray(lane < ROPE_DIM), inv[jnp.asarray(lane % (ROPE_DIM // 2))], 0.0)
    inv_lane = inv_lane.reshape(1, LANES).astype(F32)
    fb_lane = jnp.zeros((depth, 1, LANES), F32).at[:, 0, GATE_LANES:GATE_LANES + FOX_HEADS].set(fox_fbias)

    mod_all = _ada_mod(c, w_ada, b_ada).reshape(depth, batch, 9, d)
    pos2 = positions.reshape(batch * seq, 1)
    x2 = x.reshape(batch * seq, d)
    fg = final_g.reshape(1, d)
    for l in range(depth):
        mod = mod_all[l]
        x2 = _ffn_res(x2, mod, norm_g[l, 0:1], fg, w13[l, 0], w2[l, 0], sub=0, seq=seq, final=False,
                      tm=1024, chunk=768)
        qn, cmp_in, nks, nkw, nv, fqk, fv, mq, mk, mv, gm = _inproj(
            x2, mod, norm_g[l, 1:2], pos2, inv_lane, fb_lane[l], w_in_p[l], batch=batch, seq=seq)
        cmp_kv = _compress(cmp_in.reshape(batch, 4, seq // CMP_STRIDE, CMP_STRIDE * HEAD_DIM),
                           cpos[l], cw1[l], cw2[l])
        o_nsa = _nsa(qn, cmp_kv, nks, nkw, nv, gm, cst["ov"])
        fcol, frow = _decay(gm, cst["tri"])
        o_fox = _fox(fqk, fv, fcol, frow, cst["causal_fox"])
        o_moba = _moba(mq, mk, mv, cst["avg"], cst["causal_moba"])
        x2 = _outproj(x2, mod, o_nsa.reshape(batch * seq, -1), o_fox.reshape(batch * seq, -1),
                      o_moba.reshape(batch * seq, -1), w_o[l], seq=seq)
        if l % 2 == 0:
            x2 = _ffn_res(x2, mod, norm_g[l, 2:3], fg, w13[l, 1], w2[l, 1], sub=2, seq=seq,
                          final=(l == depth - 1), tm=512, chunk=1024)
        else:
            x2 = _ffn(x2, mod, norm_g[l, 2:3], fg, w13[l, 1], w2[l, 1], sub=2, seq=seq,
                      final=(l == depth - 1))
    return x2.reshape(batch, seq, d)
```

```python
import functools

import numpy as np
import jax
import jax.numpy as jnp
from jax import lax
from jax.experimental import pallas as pl
from jax.experimental.pallas import tpu as pltpu

D_MODEL = 1024
HEAD_DIM = 64
NSA_HEADS = 8
NSA_GROUPS = 2
NSA_REP = NSA_HEADS // NSA_GROUPS
FOX_HEADS = 4
MOBA_HEADS = 4
ROPE_DIM = HEAD_DIM // 4
ROPE_THETA = 500000.0
CMP_LEN = 32
CMP_STRIDE = 16
CMP_HIDDEN = 4 * HEAD_DIM
SLC_LEN = 64
SLC_TOPN = 8
WIN = 512
MOBA_BLOCK = 256
MOBA_TOPK = 3
D_FF = 2816
EPS = 1e-6
ATT_SCALE = HEAD_DIM ** -0.5

FOX_TILE = 512
MOBA_TILE = 2 * MOBA_BLOCK
LANES = 128
MXU_TILE = 256
NEG = -1e30
VMEM_LIMIT = 56 * 1024 * 1024

F32 = jnp.float32
BF16 = jnp.bfloat16
HIGHEST = lax.Precision.HIGHEST

_KV = NSA_GROUPS * HEAD_DIM
_OFF_NSA_Q = 0
_OFF_KC = _OFF_NSA_Q + NSA_HEADS * HEAD_DIM
_OFF_VC = _OFF_KC + _KV
_OFF_KS = _OFF_VC + _KV
_OFF_VS = _OFF_KS + _KV
_OFF_KW = _OFF_VS + _KV
_OFF_VW = _OFF_KW + _KV
_OFF_GATE = _OFF_VW + _KV
_OFF_FOX_Q = _OFF_GATE + 3 * NSA_HEADS
_OFF_FOX_K = _OFF_FOX_Q + FOX_HEADS * HEAD_DIM
_OFF_FOX_V = _OFF_FOX_K + FOX_HEADS * HEAD_DIM
_OFF_FOX_F = _OFF_FOX_V + FOX_HEADS * HEAD_DIM
_OFF_MOBA_Q = _OFF_FOX_F + FOX_HEADS
_OFF_MOBA_K = _OFF_MOBA_Q + MOBA_HEADS * HEAD_DIM
_OFF_MOBA_V = _OFF_MOBA_K + MOBA_HEADS * HEAD_DIM
IN_COLS = _OFF_MOBA_V + MOBA_HEADS * HEAD_DIM

N_HEAD_COLS = 2048
GATE_LANES = 3 * NSA_REP
PROJ_COLS = N_HEAD_COLS + LANES
GATE_ROWS = 16
T_ROWS = NSA_HEADS * HEAD_DIM + 4 * HEAD_DIM + LANES


def _dot(a, b):
    return jnp.dot(a, b, preferred_element_type=F32)


def _dot_nt(a, b, precision=None):
    return lax.dot_general(a, b, (((1,), (1,)), ((), ())), precision=precision,
                           preferred_element_type=F32)


def _iota(shape, dim):
    return lax.broadcasted_iota(jnp.int32, shape, dim)


def _params(*sem, flags=None):
    return pltpu.CompilerParams(dimension_semantics=sem, vmem_limit_bytes=VMEM_LIMIT, flags=flags)


def _ada_norm(x, g, shift, scale):
    ms = jnp.mean(x * x, axis=-1, keepdims=True)
    y = x * lax.rsqrt(ms + EPS) * g
    return y * (1.0 + scale) + shift


def _ada_kernel(c_ref, w_ref, b_ref, o_ref):
    c = c_ref[...]
    ca = c * jax.nn.sigmoid(c)
    o_ref[0] = jnp.dot(ca, w_ref[0], precision=HIGHEST, preferred_element_type=F32) + b_ref[0]


def _ada_mod(c, w_ada, b_ada):
    depth, d, n = w_ada.shape
    b = c.shape[0]
    tn = 1152
    return pl.pallas_call(
        _ada_kernel,
        grid=(depth, n // tn),
        in_specs=[pl.BlockSpec((b, d), lambda l, j: (0, 0)),
                  pl.BlockSpec((1, d, tn), lambda l, j: (l, 0, j)),
                  pl.BlockSpec((1, 1, tn), lambda l, j: (l, 0, j))],
        out_specs=pl.BlockSpec((1, b, tn), lambda l, j: (l, 0, j)),
        out_shape=jax.ShapeDtypeStruct((depth, b, n), F32),
        compiler_params=_params("parallel", "parallel"),
        name="ada_mod",
    )(c, w_ada, b_ada.reshape(depth, 1, n))


def _ffn_kernel(x_ref, mod_ref, g_ref, fg_ref, w13_ref, w2_ref, o_ref, *, sub, coef, final, splits):
    x = x_ref[...]
    h = _ada_norm(x, g_ref[...], mod_ref[0, 3 * sub:3 * sub + 1, :],
                  mod_ref[0, 3 * sub + 1:3 * sub + 2, :]).astype(BF16)
    acc = None
    for lo, hi in splits:
        a = _dot(h, w13_ref[:, lo:hi])
        b = _dot(h, w13_ref[:, D_FF + lo:D_FF + hi])
        u = (a * jax.nn.sigmoid(a) * b).astype(BF16)
        part = _dot(u, w2_ref[lo:hi, :])
        acc = part if acc is None else acc + part
    y = x + (coef * mod_ref[0, 3 * sub + 2:3 * sub + 3, :]) * acc
    if final:
        y = y * lax.rsqrt(jnp.mean(y * y, axis=-1, keepdims=True) + EPS) * fg_ref[...]
    o_ref[...] = y


def _ffn(x2, mod, g, final_g, w13, w2, *, sub, seq, final, tm=1024, chunk=3 * MXU_TILE):
    n, d = x2.shape
    per_b = seq // tm
    splits = tuple((lo, min(lo + chunk, D_FF)) for lo in range(0, D_FF, chunk))
    resident = lambda shape: pl.BlockSpec(shape, lambda i: (0, 0), pipeline_mode=pl.Buffered(1))
    return pl.pallas_call(
        functools.partial(_ffn_kernel, sub=sub, coef=0.5, final=final, splits=splits),
        grid=(n // tm,),
        in_specs=[pl.BlockSpec((tm, d), lambda i: (i, 0)),
                  pl.BlockSpec((1, 9, d), lambda i: (i // per_b, 0, 0)),
                  pl.BlockSpec((1, d), lambda i: (0, 0)),
                  pl.BlockSpec((1, d), lambda i: (0, 0)),
                  resident((d, 2 * D_FF)), resident((D_FF, d))],
        out_specs=pl.BlockSpec((tm, d), lambda i: (i, 0)),
        out_shape=jax.ShapeDtypeStruct((n, d), F32),
        compiler_params=_params("parallel"),
        name="ffn",
    )(x2, mod, g, final_g, w13, w2)


def _inproj_kernel(x_ref, mod_ref, g_ref, pos_ref, posr_ref, inv_ref, invc_ref, fb_ref, w_ref, wt_ref,
                   qt_ref, nvt_ref, gt_ref, cmp_ref, nks_ref, nkw_ref, fqk_ref, fv_ref, mq_ref, mk_ref,
                   mv_ref, gm_ref, *, per_b):
    h = _ada_norm(x_ref[...], g_ref[...], mod_ref[0, 3:4, :], mod_ref[0, 4:5, :]).astype(BF16)
    tm = h.shape[0]
    seq_pos = (pl.program_id(0) % per_b) * tm + _iota((tm, LANES), 0)

    half = ROPE_DIM // 2
    ang_t = invc_ref[...] * posr_ref[0].astype(F32)
    cos_t = jnp.cos(ang_t)
    sin_t = jnp.sin(ang_t)
    for s_idx in range(NSA_HEADS // 4):
        y_t = _dot_nt(wt_ref[256 * s_idx:256 * (s_idx + 1), :], h)
        for hh in range(4):
            blk = y_t[HEAD_DIM * hh:HEAD_DIM * (hh + 1)]
            r1, r2 = blk[:half], blk[half:2 * half]
            rot = jnp.concatenate([r1 * cos_t - r2 * sin_t, r2 * cos_t + r1 * sin_t, blk[2 * half:]], axis=0)
            qt_ref[0, 4 * s_idx + hh] = (rot * ATT_SCALE).astype(BF16)
    y_t = _dot_nt(wt_ref[NSA_HEADS * HEAD_DIM:NSA_HEADS * HEAD_DIM + 256, :], h)
    ones_row = jnp.where(_iota((HEAD_DIM, tm), 0) == 0, 1.0, 0.0)
    for idx in range(4):
        nvt_ref[0, idx] = jnp.concatenate([y_t[HEAD_DIM * idx:HEAD_DIM * (idx + 1)], ones_row],
                                          axis=0).astype(BF16)
    y_t = _dot_nt(wt_ref[NSA_HEADS * HEAD_DIM + 256:, :], h)
    sig_t = jax.nn.sigmoid(y_t)
    for g in range(NSA_GROUPS):
        gt_ref[0, g] = sig_t[GATE_ROWS * g:GATE_ROWS * (g + 1)]

    ang = pos_ref[...].astype(F32) * inv_ref[...]
    cosl = jnp.cos(ang)
    sinl = jnp.sin(ang)
    lane = _iota((tm, LANES), 1)
    hl = lane & (HEAD_DIM - 1)
    sin_lo = jnp.where(hl < ROPE_DIM // 2, -sinl, 0.0)
    sin_hi = jnp.where(hl >= ROPE_DIM // 2, sinl, 0.0)
    one_hot = jnp.where(lane == HEAD_DIM, 1.0, 0.0)

    def rope(y):
        return (y * cosl + pltpu.roll(y, LANES - ROPE_DIM // 2, 1) * sin_lo
                + pltpu.roll(y, ROPE_DIM // 2, 1) * sin_hi)

    def put_heads(z, out_ref, head0, dtype):
        z = z.astype(dtype)
        out_ref[0, head0] = z[:, :HEAD_DIM]
        out_ref[0, head0 + 1] = z[:, HEAD_DIM:]

    def put_wide(z, out_ref, head0, tail):
        out_ref[0, head0] = jnp.where(lane < HEAD_DIM, z, tail).astype(BF16)
        out_ref[0, head0 + 1] = jnp.where(lane < HEAD_DIM, pltpu.roll(z, HEAD_DIM, 1), tail).astype(BF16)

    def put_values(z, out_ref, head0):
        put_wide(z, out_ref, head0, one_hot)

    def put_keys(z, out_ref, head0, block_len):
        block = lax.shift_right_logical(seq_pos, block_len.bit_length() - 1)
        put_wide(z, out_ref, head0, jnp.where(lane - HEAD_DIM == block, 1.0, 0.0))

    def slab(c0):
        y = _dot(h, w_ref[:, c0:c0 + 2 * LANES])
        return y[:, :LANES], y[:, LANES:]

    a, b = slab(0)
    put_heads(rope(a), cmp_ref, 0, F32)
    put_heads(b, cmp_ref, 2, F32)
    a, b = slab(256)
    put_keys(rope(a), nks_ref, 0, SLC_LEN)
    put_heads(rope(b), nkw_ref, 0, BF16)
    a, b = slab(512)
    put_heads(a * ATT_SCALE, fqk_ref, 0, BF16)
    put_heads(b * ATT_SCALE, fqk_ref, 2, BF16)
    a, b = slab(768)
    put_heads(a, fqk_ref, 4, BF16)
    put_heads(b, fqk_ref, 6, BF16)
    a, b = slab(1024)
    put_values(a, fv_ref, 0)
    put_values(b, fv_ref, 2)
    a, b = slab(1280)
    put_heads(rope(a) * ATT_SCALE, mq_ref, 0, BF16)
    put_heads(rope(b) * ATT_SCALE, mq_ref, 2, BF16)
    a, b = slab(1536)
    put_keys(rope(a), mk_ref, 0, MOBA_BLOCK)
    put_keys(rope(b), mk_ref, 2, MOBA_BLOCK)
    a, b = slab(1792)
    put_values(a, mv_ref, 0)
    put_values(b, mv_ref, 2)
    f = _dot(h, w_ref[:, N_HEAD_COLS:])
    gm_ref[0] = jax.nn.log_sigmoid(f + fb_ref[...])


def _inproj(x2, mod, g, positions, inv_lane, inv_col, fb_lane, w_p, w_t, *, batch, seq, tm=512):
    n, d = x2.shape
    per_b = seq // tm

    def hm(nh, dtype, width=HEAD_DIM):
        return (pl.BlockSpec((1, nh, tm, width), lambda i: (i // per_b, 0, i % per_b, 0)),
                jax.ShapeDtypeStruct((batch, nh, seq, width), dtype))

    def tr(nh, rows, dtype):
        return (pl.BlockSpec((1, nh, rows, tm), lambda i: (i // per_b, 0, 0, i % per_b)),
                jax.ShapeDtypeStruct((batch, nh, rows, seq), dtype))

    specs = [tr(NSA_HEADS, HEAD_DIM, BF16), tr(4, LANES, BF16), tr(NSA_GROUPS, GATE_ROWS, F32),
             hm(4, F32), hm(2, BF16, LANES), hm(2, BF16),
             hm(8, BF16), hm(4, BF16, LANES),
             hm(4, BF16), hm(4, BF16, LANES), hm(4, BF16, LANES),
             (pl.BlockSpec((1, tm, LANES), lambda i: (i // per_b, i % per_b, 0)),
              jax.ShapeDtypeStruct((batch, seq, LANES), F32))]
    const = lambda shape: pl.BlockSpec(shape, lambda i: (0,) * len(shape))
    return pl.pallas_call(
        functools.partial(_inproj_kernel, per_b=per_b),
        grid=(n // tm,),
        in_specs=[pl.BlockSpec((tm, d), lambda i: (i, 0)),
                  pl.BlockSpec((1, 9, d), lambda i: (i // per_b, 0, 0)),
                  const((1, d)),
                  pl.BlockSpec((tm, 1), lambda i: (i, 0)),
                  pl.BlockSpec((1, 1, tm), lambda i: (i, 0, 0)),
                  const((1, LANES)), const((ROPE_DIM // 2, 1)), const((1, LANES)),
                  const(w_p.shape), const(w_t.shape)],
        out_specs=[s for s, _ in specs],
        out_shape=[s for _, s in specs],
        compiler_params=_params("parallel"),
        name="inproj",
    )(x2, mod, g, positions.reshape(n, 1), positions.reshape(n // tm, 1, tm), inv_lane, inv_col, fb_lane,
      w_p, w_t)


def _compress_kernel(z_ref, pos_ref, w1_ref, w2_ref, w2t_ref, o_ref, ot_ref):
    z = z_ref[0, 0]
    top = _dot((z + pos_ref[0, 0]).astype(BF16), w1_ref[0, 0])
    bot = _dot((z + pos_ref[0, 1]).astype(BF16), w1_ref[0, 1])
    nch = z.shape[0]
    hid = top + pltpu.roll(bot, nch - 1, 0)
    act = jax.nn.gelu(hid).astype(BF16)
    o_ref[0, 0] = _dot(act, w2_ref[0]).astype(o_ref.dtype)
    ot_ref[0, 0] = _dot_nt(w2t_ref[0], act).astype(ot_ref.dtype)


def _compress(cmp_in, pos_l, w1_l, w2_l, w2t_l):
    b, four, nch, width = cmp_in.shape
    return pl.pallas_call(
        _compress_kernel,
        grid=(b, four),
        in_specs=[pl.BlockSpec((1, 1, nch, width), lambda i, n: (i, n, 0, 0)),
                  pl.BlockSpec((1, 2, 1, width), lambda i, n: (n // 2, 0, 0, 0)),
                  pl.BlockSpec((1, 2, width, CMP_HIDDEN), lambda i, n: (n // 2, 0, 0, 0)),
                  pl.BlockSpec((1, CMP_HIDDEN, HEAD_DIM), lambda i, n: (n // 2, 0, 0)),
                  pl.BlockSpec((1, HEAD_DIM, CMP_HIDDEN), lambda i, n: (n // 2, 0, 0))],
        out_specs=[pl.BlockSpec((1, 1, nch, HEAD_DIM), lambda i, n: (i, n, 0, 0)),
                   pl.BlockSpec((1, 1, HEAD_DIM, nch), lambda i, n: (i, n, 0, 0))],
        out_shape=[jax.ShapeDtypeStruct((b, four, nch, HEAD_DIM), BF16),
                   jax.ShapeDtypeStruct((b, four, HEAD_DIM, nch), BF16)],
        compiler_params=_params("parallel", "parallel"),
        name="nsa_compress",
    )(cmp_in, pos_l, w1_l, w2_l, w2t_l)


def _chunk_update(carry, s, v_aug):
    m, acc = carry
    m_new = jnp.maximum(m, jnp.max(s, axis=-1, keepdims=True))
    p = jnp.exp(s - m_new).astype(BF16)
    acc = jnp.exp(m - m_new) * acc + _dot(p, v_aug)
    return m_new, acc


def _chunk_init(rows):
    return jnp.full((rows, 1), NEG, F32), jnp.zeros((rows, LANES), F32)


def _chunk_finish(carry):
    _, acc = carry
    return acc[:, :HEAD_DIM] / jnp.maximum(acc[:, HEAD_DIM:HEAD_DIM + 1], 1e-30)


def _nsa_kernel(qt_ref, kc_ref, vct_ref, ks_ref, vst_ref, kw_ref, vwt_ref, gt_ref, ovt_ref, o_ref, *, tq, tk):
    i = pl.program_id(1)
    rep = NSA_REP
    groups = range(NSA_GROUPS)
    n_c = kc_ref.shape[2]
    n_s = ks_ref.shape[2] // SLC_LEN
    tile = lambda a: jnp.concatenate([a] * rep, axis=1)
    t_q = i * tq + _iota((1, tq), 1)
    cend = _iota((n_c, 1), 0) * CMP_STRIDE + (CMP_LEN - 1)
    cm = cend <= tile(t_q)
    j_t = _iota((n_s, tq), 0)
    tb = lax.shift_right_logical(t_q, SLC_LEN.bit_length() - 1)
    valid = j_t <= tb
    forced = (j_t == 0) | (j_t == tb) | (j_t == tb - 1)
    span = WIN + tq
    start = pl.multiple_of(jnp.maximum(i * tq - WIN, 0), tq)
    dist = t_q - (start + _iota((span, 1), 0))
    band = tile(jnp.where((dist >= 0) & (dist < WIN), 0.0, NEG))

    o_cmp, o_win, q4a = [], [], []
    for g in groups:
        q4 = jnp.concatenate([qt_ref[0, rep * g + r] for r in range(rep)], axis=1)
        s = jnp.where(cm, _dot(kc_ref[0, g], q4), NEG)
        e = jnp.where(cm, jnp.exp(s - jnp.max(s, axis=0, keepdims=True)), 0.0)
        p = e / jnp.maximum(jnp.sum(e, axis=0, keepdims=True), 1e-30)
        o_cmp.append(_dot(vct_ref[0, g], p.astype(BF16)))
        psum = p[:, 0:tq] + p[:, tq:2 * tq] + p[:, 2 * tq:3 * tq] + p[:, 3 * tq:4 * tq]
        imp = jnp.dot(ovt_ref[...], psum, precision=HIGHEST, preferred_element_type=F32)[:n_s]
        v_t = jnp.where(valid, jnp.where(forced, jnp.inf, imp), -jnp.inf)
        rank = jnp.zeros((n_s, tq), jnp.int32)
        for jp in range(n_s):
            row = v_t[jp:jp + 1, :]
            ahead = (row > v_t) | ((row == v_t) & (j_t > jp))
            rank = rank + ahead.astype(jnp.int32)
        neg_t = jnp.where((rank < SLC_TOPN) & (v_t > -jnp.inf), 0.0, NEG)
        neg = jnp.concatenate([neg_t, jnp.zeros((HEAD_DIM - n_s, tq), F32)], axis=0)
        q4a.append(jnp.concatenate([q4, tile(neg).astype(BF16)], axis=0))
        sw = _dot(kw_ref[0, g, pl.ds(start, span), :], q4) + band
        pw = jnp.exp(sw - jnp.max(sw, axis=0, keepdims=True)).astype(BF16)
        aw = _dot(vwt_ref[0, g, :, pl.ds(start, span)], pw)
        o_win.append(aw[:HEAD_DIM] / jnp.maximum(aw[HEAD_DIM:HEAD_DIM + 1], 1e-30))

    def slc_step(c, carries, diag):
        k0 = pl.multiple_of(c * tk, tk)
        new = []
        for g in groups:
            sc = _dot(ks_ref[0, g, pl.ds(k0, tk), :], q4a[g])
            if diag:
                sc = sc + tile(jnp.where(k0 + _iota((tk, 1), 0) <= t_q, 0.0, NEG))
            m, acc = carries[g]
            m_new = jnp.maximum(m, jnp.max(sc, axis=0, keepdims=True))
            p = jnp.exp(sc - m_new).astype(BF16)
            acc = jnp.exp(m - m_new) * acc + _dot(vst_ref[0, g, :, pl.ds(k0, tk)], p)
            new.append((m_new, acc))
        return tuple(new)

    last = i // (tk // tq)
    init = (jnp.full((1, rep * tq), NEG, F32), jnp.zeros((LANES, rep * tq), F32))
    carries = lax.fori_loop(0, last, functools.partial(slc_step, diag=False), tuple(init for _ in groups))
    carries = slc_step(last, carries, True)

    pairs = []
    for g in groups:
        acc = carries[g][1]
        o_slc = acc[:HEAD_DIM] / jnp.maximum(acc[HEAD_DIM:HEAD_DIM + 1], 1e-30)
        gates = gt_ref[0, g]
        heads = []
        for r in range(rep):
            cols = slice(r * tq, (r + 1) * tq)
            heads.append(gates[3 * r:3 * r + 1] * o_cmp[g][:, cols]
                         + gates[3 * r + 1:3 * r + 2] * o_slc[:, cols]
                         + gates[3 * r + 2:3 * r + 3] * o_win[g][:, cols])
        for a in range(0, rep, 2):
            pairs.append(jnp.concatenate(heads[a:a + 2], axis=0).T)
    o_ref[0] = jnp.concatenate(pairs, axis=1).astype(o_ref.dtype)


def _nsa(qt, cmp_kv, cmp_t, nks, nkw, nvt, gt, ovt, *, tq=128, tk=512):
    b, _, _, seq = qt.shape
    n_c = cmp_kv.shape[2]
    ng = NSA_GROUPS
    return pl.pallas_call(
        functools.partial(_nsa_kernel, tq=tq, tk=tk),
        grid=(b, seq // tq),
        in_specs=[pl.BlockSpec((1, NSA_HEADS, HEAD_DIM, tq), lambda bi, i: (bi, 0, 0, i)),
                  pl.BlockSpec((1, ng, n_c, HEAD_DIM), lambda bi, i: (bi, 0, 0, 0)),
                  pl.BlockSpec((1, ng, HEAD_DIM, n_c), lambda bi, i: (bi, 1, 0, 0)),
                  pl.BlockSpec((1, ng, seq, LANES), lambda bi, i: (bi, 0, 0, 0)),
                  pl.BlockSpec((1, ng, LANES, seq), lambda bi, i: (bi, 0, 0, 0)),
                  pl.BlockSpec((1, ng, seq, HEAD_DIM), lambda bi, i: (bi, 0, 0, 0)),
                  pl.BlockSpec((1, ng, LANES, seq), lambda bi, i: (bi, 1, 0, 0)),
                  pl.BlockSpec((1, ng, GATE_ROWS, tq), lambda bi, i: (bi, 0, 0, i)),
                  pl.BlockSpec((LANES, n_c), lambda bi, i: (0, 0))],
        out_specs=pl.BlockSpec((1, tq, NSA_HEADS * HEAD_DIM), lambda bi, i: (bi, i, 0)),
        out_shape=jax.ShapeDtypeStruct((b, seq, NSA_HEADS * HEAD_DIM), BF16),
        compiler_params=_params("parallel", "arbitrary"),
        name="nsa",
    )(qt, cmp_kv, cmp_t, nks, nvt, nkw, nvt, gt, ovt)


def _decay_kernel(lf_ref, tri_ref, col_ref, row_ref):
    seq = lf_ref.shape[1]
    blk = tri_ref.shape[0]
    carry = jnp.zeros((1, LANES), F32)
    for c in range(seq // blk):
        rows = slice(c * blk, (c + 1) * blk)
        cs = jnp.dot(tri_ref[...], lf_ref[0, rows, :], precision=HIGHEST,
                     preferred_element_type=F32) + carry
        col_ref[0, rows, :] = cs
        carry = cs[blk - 1:blk, :]
    row_ref[0] = col_ref[0].T[8:16, :]


def _decay(gm, tri):
    b, seq, _ = gm.shape
    blk = tri.shape[0]
    return pl.pallas_call(
        _decay_kernel,
        grid=(b,),
        in_specs=[pl.BlockSpec((1, seq, LANES), lambda i: (i, 0, 0)),
                  pl.BlockSpec((blk, blk), lambda i: (0, 0))],
        out_specs=[pl.BlockSpec((1, seq, LANES), lambda i: (i, 0, 0)),
                   pl.BlockSpec((1, 8, seq), lambda i: (i, 0, 0))],
        out_shape=[jax.ShapeDtypeStruct((b, seq, LANES), F32),
                   jax.ShapeDtypeStruct((b, 8, seq), F32)],
        compiler_params=_params("parallel"),
        name="fox_decay",
    )(gm, tri)


def _fox_kernel(q_ref, k_ref, v_ref, fcol_ref, frow_ref, tri_ref, o_ref, *, tq):
    i = pl.program_id(1)
    heads = range(FOX_HEADS)
    qs = [q_ref[0, h] for h in heads]
    fqs = [fcol_ref[0, :, GATE_LANES + h:GATE_LANES + h + 1] for h in heads]

    def step(c, carries, diag):
        k0 = pl.multiple_of(c * tq, tq)
        new = []
        for h in heads:
            fk = frow_ref[0, 4 + h:5 + h, pl.ds(k0, tq)]
            sc = _dot_nt(qs[h], k_ref[0, h, pl.ds(k0, tq), :]) + (fqs[h] - fk)
            if diag:
                sc = sc + tri_ref[...]
            new.append(_chunk_update(carries[h], sc, v_ref[0, h, pl.ds(k0, tq), :]))
        return tuple(new)

    carries = step(i, tuple(_chunk_init(tq) for _ in heads), True)
    carries = lax.fori_loop(0, i, functools.partial(step, diag=False), carries)
    o_ref[0] = jnp.concatenate([_chunk_finish(cr) for cr in carries], axis=1).astype(o_ref.dtype)


def _fox(fqk, fv, fcol, frow, tri, *, tq=FOX_TILE):
    b, _, seq, _ = fqk.shape
    nh = FOX_HEADS
    return pl.pallas_call(
        functools.partial(_fox_kernel, tq=tq),
        grid=(b, seq // tq),
        in_specs=[pl.BlockSpec((1, nh, tq, HEAD_DIM), lambda bi, i: (bi, 0, i, 0)),
                  pl.BlockSpec((1, nh, seq, HEAD_DIM), lambda bi, i: (bi, 1, 0, 0)),
                  pl.BlockSpec((1, nh, seq, LANES), lambda bi, i: (bi, 0, 0, 0)),
                  pl.BlockSpec((1, tq, LANES), lambda bi, i: (bi, i, 0)),
                  pl.BlockSpec((1, 8, seq), lambda bi, i: (bi, 0, 0)),
                  pl.BlockSpec((tq, tq), lambda bi, i: (0, 0))],
        out_specs=pl.BlockSpec((1, tq, nh * HEAD_DIM), lambda bi, i: (bi, i, 0)),
        out_shape=jax.ShapeDtypeStruct((b, seq, nh * HEAD_DIM), BF16),
        compiler_params=_params("parallel", "arbitrary"),
        name="fox",
    )(fqk, fqk, fv, fcol, frow, tri)


def _moba_kernel(q_ref, k_ref, v_ref, avg_ref, tri_ref, o_ref, km_ref, *, tq):
    i = pl.program_id(1)
    nb = avg_ref.shape[0]

    @pl.when(i == 0)
    def _():
        for h in range(MOBA_HEADS):
            km_ref[h] = _dot(avg_ref[...], k_ref[0, h])

    blk = _iota((nb, tq), 0)
    own = i * (tq // MOBA_BLOCK) + lax.shift_right_logical(_iota((nb, tq), 1), MOBA_BLOCK.bit_length() - 1)
    qas = []
    for h in range(MOBA_HEADS):
        q = q_ref[0, h]
        gate = _dot_nt(km_ref[h][:, :HEAD_DIM], q.astype(F32), precision=HIGHEST)
        val = jnp.where(blk < own, gate, -jnp.inf)
        rank = jnp.zeros((nb, tq), jnp.int32)
        for jp in range(nb):
            row = val[jp:jp + 1, :]
            ahead = (row > val) | ((row == val) & (blk > jp))
            rank = rank + ahead.astype(jnp.int32)
        keep = ((rank < MOBA_TOPK) & (blk < own)) | (blk == own)
        neg_t = jnp.where(keep, 0.0, NEG)
        neg = jnp.concatenate([neg_t, jnp.zeros((LANES - nb, tq), F32)], axis=0).T[:, :HEAD_DIM]
        qas.append(jnp.concatenate([q.astype(F32), neg], axis=1).astype(BF16))

    def step(c, carries, diag):
        k0 = pl.multiple_of(c * tq, tq)
        new = []
        for h in range(MOBA_HEADS):
            sc = _dot_nt(qas[h], k_ref[0, h, pl.ds(k0, tq), :])
            if diag:
                sc = sc + tri_ref[...]
            new.append(_chunk_update(carries[h], sc, v_ref[0, h, pl.ds(k0, tq), :]))
        return tuple(new)

    carries = step(i, tuple(_chunk_init(tq) for _ in range(MOBA_HEADS)), True)
    carries = lax.fori_loop(0, i, functools.partial(step, diag=False), carries)
    o_ref[0] = jnp.concatenate([_chunk_finish(cr) for cr in carries], axis=1).astype(o_ref.dtype)


def _moba(mq, mk, mv, avg, tri, *, tq=MOBA_TILE):
    b, _, seq, _ = mq.shape
    nh = MOBA_HEADS
    nb = avg.shape[0]
    return pl.pallas_call(
        functools.partial(_moba_kernel, tq=tq),
        grid=(b, seq // tq),
        in_specs=[pl.BlockSpec((1, nh, tq, HEAD_DIM), lambda bi, i: (bi, 0, i, 0)),
                  pl.BlockSpec((1, nh, seq, LANES), lambda bi, i: (bi, 0, 0, 0)),
                  pl.BlockSpec((1, nh, seq, LANES), lambda bi, i: (bi, 0, 0, 0)),
                  pl.BlockSpec((nb, seq), lambda bi, i: (0, 0)),
                  pl.BlockSpec((tq, tq), lambda bi, i: (0, 0))],
        out_specs=pl.BlockSpec((1, tq, nh * HEAD_DIM), lambda bi, i: (bi, i, 0)),
        out_shape=jax.ShapeDtypeStruct((b, seq, nh * HEAD_DIM), BF16),
        scratch_shapes=[pltpu.VMEM((nh, nb, LANES), F32)],
        compiler_params=_params("parallel", "arbitrary"),
        name="moba",
    )(mq, mk, mv, avg, tri)


def _outproj_kernel(x_ref, mod_ref, on_ref, of_ref, om_ref, w_ref, o_ref):
    n_w = NSA_HEADS * HEAD_DIM
    f_w = FOX_HEADS * HEAD_DIM
    acc = _dot(on_ref[...], w_ref[0:n_w, :])
    acc += _dot(of_ref[...], w_ref[n_w:n_w + f_w, :])
    acc += _dot(om_ref[...], w_ref[n_w + f_w:, :])
    o_ref[...] = x_ref[...] + mod_ref[0, 5:6, :] * acc


def _outproj(x2, mod, o_nsa, o_fox, o_moba, w_out, *, seq, tm=512):
    n, d = x2.shape
    per_b = seq // tm
    row = lambda width: pl.BlockSpec((tm, width), lambda i: (i, 0))
    return pl.pallas_call(
        _outproj_kernel,
        grid=(n // tm,),
        in_specs=[row(d), pl.BlockSpec((1, 9, d), lambda i: (i // per_b, 0, 0)),
                  row(o_nsa.shape[1]), row(o_fox.shape[1]), row(o_moba.shape[1]),
                  pl.BlockSpec((d, d), lambda i: (0, 0))],
        out_specs=row(d),
        out_shape=jax.ShapeDtypeStruct((n, d), F32),
        compiler_params=_params("parallel"),
        name="outproj",
    )(x2, mod, o_nsa, o_fox, o_moba, w_out)


def _pack_w_in(w_in):
    depth, d, _ = w_in.shape

    def cols(a, b):
        return w_in[:, :, a:b]

    zeros = lambda n: jnp.zeros((depth, d, n), w_in.dtype)
    g_half = 3 * NSA_REP
    token_major = [cols(_OFF_KC, _OFF_KS),
                   cols(_OFF_KS, _OFF_VS), cols(_OFF_KW, _OFF_VW),
                   cols(_OFF_FOX_Q, _OFF_FOX_F),
                   cols(_OFF_MOBA_Q, IN_COLS),
                   zeros(GATE_LANES), cols(_OFF_FOX_F, _OFF_FOX_F + FOX_HEADS),
                   zeros(LANES - GATE_LANES - FOX_HEADS)]
    transposed = [cols(_OFF_NSA_Q, _OFF_KC),
                  cols(_OFF_VS, _OFF_KW), cols(_OFF_VW, _OFF_GATE),
                  cols(_OFF_GATE, _OFF_GATE + g_half), zeros(GATE_ROWS - g_half),
                  cols(_OFF_GATE + g_half, _OFF_GATE + 2 * g_half), zeros(LANES - GATE_ROWS - g_half)]
    w_p = jnp.concatenate(token_major, axis=-1).astype(BF16)
    w_t = jnp.swapaxes(jnp.concatenate(transposed, axis=-1), 1, 2).astype(BF16)
    return w_p, w_t


def _constants(seq):
    n_c = seq // CMP_STRIDE
    n_s = seq // SLC_LEN
    c0 = np.arange(n_c)[None, :] * CMP_STRIDE
    j0 = np.arange(LANES)[:, None] * SLC_LEN
    real = (np.arange(n_c)[None, :] < (seq - CMP_LEN) // CMP_STRIDE + 1) & (np.arange(LANES)[:, None] < n_s)
    ov = ((c0 < j0 + SLC_LEN) & (c0 + CMP_LEN > j0) & real).astype(np.float32)
    nb = seq // MOBA_BLOCK
    avg = (np.arange(nb)[:, None] == (np.arange(seq)[None, :] // MOBA_BLOCK)).astype(np.float32) / MOBA_BLOCK
    tri = np.tril(np.ones((256, 256), np.float32))

    def causal_bias(n):
        return np.where(np.arange(n)[None, :] <= np.arange(n)[:, None], 0.0, NEG).astype(np.float32)

    r = np.arange(MOBA_TILE)
    same = (r[:, None] // MOBA_BLOCK) == (r[None, :] // MOBA_BLOCK)
    allowed = np.where(same, r[None, :] <= r[:, None], r[None, :] < r[:, None])
    causal_moba = np.where(allowed, 0.0, NEG).astype(np.float32)
    return dict(ov=jnp.asarray(ov), avg=jnp.asarray(avg, dtype=BF16), tri=jnp.asarray(tri),
                causal_fox=jnp.asarray(causal_bias(FOX_TILE)), causal_moba=jnp.asarray(causal_moba))


def kernel(x, c, positions, norm_g, w_ada, b_ada, w_in, fox_fbias, cmp_pos, cmp_w1, cmp_w2, w_out,
           ffn_w13, ffn_w2, final_g):
    batch, seq, d = x.shape
    depth = w_ada.shape[0]
    assert d == D_MODEL and seq % MOBA_BLOCK == 0 and seq % 1024 == 0

    w_in_p, w_in_t = _pack_w_in(w_in)
    w13 = ffn_w13.astype(BF16)
    w2 = ffn_w2.astype(BF16)
    w_o = w_out.astype(BF16)
    half = CMP_LEN * HEAD_DIM // 2
    cw1 = cmp_w1.astype(BF16).reshape(depth, 2, 2, half, CMP_HIDDEN)
    cw2 = cmp_w2.astype(BF16)
    cw2t = jnp.swapaxes(cw2, 2, 3)
    cpos = cmp_pos.reshape(depth, 2, 2, 1, half)
    cst = _constants(seq)

    inv = ROPE_THETA ** (-jnp.arange(0, ROPE_DIM, 2, dtype=F32) / ROPE_DIM)
    lane = np.arange(LANES) % HEAD_DIM
    inv_lane = jnp.where(jnp.asarray(lane < ROPE_DIM), inv[jnp.asarray(lane % (ROPE_DIM // 2))], 0.0)
    inv_lane = inv_lane.reshape(1, LANES).astype(F32)
    inv_col = inv.reshape(ROPE_DIM // 2, 1)
    fb_lane = jnp.zeros((depth, 1, LANES), F32).at[:, 0, GATE_LANES:GATE_LANES + FOX_HEADS].set(fox_fbias)

    mod_all = _ada_mod(c, w_ada, b_ada).reshape(depth, batch, 9, d)
    x2 = x.reshape(batch * seq, d)
    fg = final_g.reshape(1, d)
    for l in range(depth):
        mod = mod_all[l]
        x2 = _ffn(x2, mod, norm_g[l, 0:1], fg, w13[l, 0], w2[l, 0], sub=0, seq=seq, final=False)
        qt, nvt, gt, cmp_in, nks, nkw, fqk, fv, mq, mk, mv, gm = _inproj(
            x2, mod, norm_g[l, 1:2], positions, inv_lane, inv_col, fb_lane[l], w_in_p[l], w_in_t[l],
            batch=batch, seq=seq)
        cmp_kv, cmp_t = _compress(cmp_in.reshape(batch, 4, seq // CMP_STRIDE, CMP_STRIDE * HEAD_DIM),
                                  cpos[l], cw1[l], cw2[l], cw2t[l])
        o_nsa = _nsa(qt, cmp_kv, cmp_t, nks, nkw, nvt, gt, cst["ov"])
        fcol, frow = _decay(gm, cst["tri"])
        o_fox = _fox(fqk, fv, fcol, frow, cst["causal_fox"])
        o_moba = _moba(mq, mk, mv, cst["avg"], cst["causal_moba"])
        x2 = _outproj(x2, mod, o_nsa.reshape(batch * seq, -1), o_fox.reshape(batch * seq, -1),
                      o_moba.reshape(batch * seq, -1), w_o[l], seq=seq)
        x2 = _ffn(x2, mod, norm_g[l, 2:3], fg, w13[l, 1], w2[l, 1], sub=2, seq=seq,
                  final=(l == depth - 1))
    return x2.reshape(batch, seq, d)
```

```python
import functools

import numpy as np
import jax
import jax.numpy as jnp
from jax import lax
from jax.experimental import pallas as pl
from jax.experimental.pallas import tpu as pltpu

D_MODEL = 1024
HEAD_DIM = 64
NSA_HEADS = 8
NSA_GROUPS = 2
NSA_REP = NSA_HEADS // NSA_GROUPS
FOX_HEADS = 4
MOBA_HEADS = 4
ROPE_DIM = HEAD_DIM // 4
ROPE_THETA = 500000.0
CMP_LEN = 32
CMP_STRIDE = 16
CMP_HIDDEN = 4 * HEAD_DIM
SLC_LEN = 64
SLC_TOPN = 8
WIN = 512
MOBA_BLOCK = 256
MOBA_TOPK = 3
D_FF = 2816
EPS = 1e-6
ATT_SCALE = HEAD_DIM ** -0.5

FOX_TILE = 512
MOBA_TILE = 2 * MOBA_BLOCK
LANES = 128
MXU_TILE = 256
NEG = -1e30
VMEM_LIMIT = 56 * 1024 * 1024

F32 = jnp.float32
BF16 = jnp.bfloat16
HIGHEST = lax.Precision.HIGHEST

_KV = NSA_GROUPS * HEAD_DIM
_OFF_NSA_Q = 0
_OFF_KC = _OFF_NSA_Q + NSA_HEADS * HEAD_DIM
_OFF_VC = _OFF_KC + _KV
_OFF_KS = _OFF_VC + _KV
_OFF_VS = _OFF_KS + _KV
_OFF_KW = _OFF_VS + _KV
_OFF_VW = _OFF_KW + _KV
_OFF_GATE = _OFF_VW + _KV
_OFF_FOX_Q = _OFF_GATE + 3 * NSA_HEADS
_OFF_FOX_K = _OFF_FOX_Q + FOX_HEADS * HEAD_DIM
_OFF_FOX_V = _OFF_FOX_K + FOX_HEADS * HEAD_DIM
_OFF_FOX_F = _OFF_FOX_V + FOX_HEADS * HEAD_DIM
_OFF_MOBA_Q = _OFF_FOX_F + FOX_HEADS
_OFF_MOBA_K = _OFF_MOBA_Q + MOBA_HEADS * HEAD_DIM
_OFF_MOBA_V = _OFF_MOBA_K + MOBA_HEADS * HEAD_DIM
IN_COLS = _OFF_MOBA_V + MOBA_HEADS * HEAD_DIM

N_HEAD_COLS = 1024
GATE_LANES = 3 * NSA_REP
PROJ_COLS = N_HEAD_COLS + LANES
GATE_ROWS = 16
_T_NSA_Q = 0
_T_NSA_V = _T_NSA_Q + NSA_HEADS * HEAD_DIM
_T_GATE = _T_NSA_V + 4 * HEAD_DIM
_T_FOX_Q = _T_GATE + LANES
_T_FOX_V = _T_FOX_Q + FOX_HEADS * HEAD_DIM
_T_MOBA_Q = _T_FOX_V + FOX_HEADS * HEAD_DIM
_T_MOBA_V = _T_MOBA_Q + MOBA_HEADS * HEAD_DIM
T_ROWS = _T_MOBA_V + MOBA_HEADS * HEAD_DIM


def _dot(a, b):
    return jnp.dot(a, b, preferred_element_type=F32)


def _dot_nt(a, b, precision=None):
    return lax.dot_general(a, b, (((1,), (1,)), ((), ())), precision=precision,
                           preferred_element_type=F32)


def _iota(shape, dim):
    return lax.broadcasted_iota(jnp.int32, shape, dim)


def _params(*sem, flags=None):
    return pltpu.CompilerParams(dimension_semantics=sem, vmem_limit_bytes=VMEM_LIMIT, flags=flags)


def _ada_norm(x, g, shift, scale):
    ms = jnp.mean(x * x, axis=-1, keepdims=True)
    y = x * lax.rsqrt(ms + EPS) * g
    return y * (1.0 + scale) + shift


def _ada_kernel(c_ref, w_ref, b_ref, o_ref):
    c = c_ref[...]
    ca = c * jax.nn.sigmoid(c)
    o_ref[0] = jnp.dot(ca, w_ref[0], precision=HIGHEST, preferred_element_type=F32) + b_ref[0]


def _ada_mod(c, w_ada, b_ada):
    depth, d, n = w_ada.shape
    b = c.shape[0]
    tn = 1152
    return pl.pallas_call(
        _ada_kernel,
        grid=(depth, n // tn),
        in_specs=[pl.BlockSpec((b, d), lambda l, j: (0, 0)),
                  pl.BlockSpec((1, d, tn), lambda l, j: (l, 0, j)),
                  pl.BlockSpec((1, 1, tn), lambda l, j: (l, 0, j))],
        out_specs=pl.BlockSpec((1, b, tn), lambda l, j: (l, 0, j)),
        out_shape=jax.ShapeDtypeStruct((depth, b, n), F32),
        compiler_params=_params("parallel", "parallel"),
        name="ada_mod",
    )(c, w_ada, b_ada.reshape(depth, 1, n))


def _ffn_kernel(x_ref, mod_ref, g_ref, fg_ref, w13_ref, w2_ref, o_ref, *, sub, coef, final, splits):
    x = x_ref[...]
    h = _ada_norm(x, g_ref[...], mod_ref[0, 3 * sub:3 * sub + 1, :],
                  mod_ref[0, 3 * sub + 1:3 * sub + 2, :]).astype(BF16)
    acc = None
    for lo, hi in splits:
        a = _dot(h, w13_ref[:, lo:hi])
        b = _dot(h, w13_ref[:, D_FF + lo:D_FF + hi])
        u = (a * jax.nn.sigmoid(a) * b).astype(BF16)
        part = _dot(u, w2_ref[lo:hi, :])
        acc = part if acc is None else acc + part
    y = x + (coef * mod_ref[0, 3 * sub + 2:3 * sub + 3, :]) * acc
    if final:
        y = y * lax.rsqrt(jnp.mean(y * y, axis=-1, keepdims=True) + EPS) * fg_ref[...]
    o_ref[...] = y


def _ffn(x2, mod, g, final_g, w13, w2, *, sub, seq, final, tm=1024, chunk=3 * MXU_TILE):
    n, d = x2.shape
    per_b = seq // tm
    splits = tuple((lo, min(lo + chunk, D_FF)) for lo in range(0, D_FF, chunk))
    resident = lambda shape: pl.BlockSpec(shape, lambda i: (0, 0), pipeline_mode=pl.Buffered(1))
    return pl.pallas_call(
        functools.partial(_ffn_kernel, sub=sub, coef=0.5, final=final, splits=splits),
        grid=(n // tm,),
        in_specs=[pl.BlockSpec((tm, d), lambda i: (i, 0)),
                  pl.BlockSpec((1, 9, d), lambda i: (i // per_b, 0, 0)),
                  pl.BlockSpec((1, d), lambda i: (0, 0)),
                  pl.BlockSpec((1, d), lambda i: (0, 0)),
                  resident((d, 2 * D_FF)), resident((D_FF, d))],
        out_specs=pl.BlockSpec((tm, d), lambda i: (i, 0)),
        out_shape=jax.ShapeDtypeStruct((n, d), F32),
        compiler_params=_params("parallel"),
        name="ffn",
    )(x2, mod, g, final_g, w13, w2)


def _inproj_kernel(x_ref, mod_ref, g_ref, pos_ref, posr_ref, inv_ref, invc_ref, fb_ref, w_ref, wt_ref,
                   qt_ref, nvt_ref, gt_ref, fqt_ref, fvt_ref, mqt_ref, mvt_ref,
                   cmp_ref, nks_ref, nkw_ref, fk_ref, mk_ref, gm_ref, *, per_b):
    h = _ada_norm(x_ref[...], g_ref[...], mod_ref[0, 3:4, :], mod_ref[0, 4:5, :]).astype(BF16)
    tm = h.shape[0]
    seq_pos = (pl.program_id(0) % per_b) * tm + _iota((tm, LANES), 0)

    half = ROPE_DIM // 2
    ang_t = invc_ref[...] * posr_ref[0].astype(F32)
    cos_t = jnp.cos(ang_t)
    sin_t = jnp.sin(ang_t)
    ones_row = jnp.where(_iota((HEAD_DIM, tm), 0) == 0, 1.0, 0.0)
    slab_rows = 4 * HEAD_DIM

    def queries_t(row0, n_heads, out_ref, rotate):
        for s_idx in range(n_heads // 4):
            y_t = _dot_nt(wt_ref[row0 + slab_rows * s_idx:row0 + slab_rows * (s_idx + 1), :], h)
            for hh in range(4):
                blk = y_t[HEAD_DIM * hh:HEAD_DIM * (hh + 1)]
                if rotate:
                    r1, r2 = blk[:half], blk[half:2 * half]
                    blk = jnp.concatenate([r1 * cos_t - r2 * sin_t, r2 * cos_t + r1 * sin_t, blk[2 * half:]],
                                          axis=0)
                out_ref[0, 4 * s_idx + hh] = (blk * ATT_SCALE).astype(BF16)

    def values_t(row0, out_ref):
        y_t = _dot_nt(wt_ref[row0:row0 + slab_rows, :], h)
        for idx in range(4):
            out_ref[0, idx] = jnp.concatenate([y_t[HEAD_DIM * idx:HEAD_DIM * (idx + 1)], ones_row],
                                              axis=0).astype(BF16)

    queries_t(_T_NSA_Q, NSA_HEADS, qt_ref, True)
    values_t(_T_NSA_V, nvt_ref)
    sig_t = jax.nn.sigmoid(_dot_nt(wt_ref[_T_GATE:_T_GATE + LANES, :], h))
    for g in range(NSA_GROUPS):
        gt_ref[0, g] = sig_t[GATE_ROWS * g:GATE_ROWS * (g + 1)]
    queries_t(_T_FOX_Q, FOX_HEADS, fqt_ref, False)
    values_t(_T_FOX_V, fvt_ref)
    queries_t(_T_MOBA_Q, MOBA_HEADS, mqt_ref, True)
    values_t(_T_MOBA_V, mvt_ref)

    ang = pos_ref[...].astype(F32) * inv_ref[...]
    cosl = jnp.cos(ang)
    sinl = jnp.sin(ang)
    lane = _iota((tm, LANES), 1)
    hl = lane & (HEAD_DIM - 1)
    sin_lo = jnp.where(hl < ROPE_DIM // 2, -sinl, 0.0)
    sin_hi = jnp.where(hl >= ROPE_DIM // 2, sinl, 0.0)

    def rope(y):
        return (y * cosl + pltpu.roll(y, LANES - ROPE_DIM // 2, 1) * sin_lo
                + pltpu.roll(y, ROPE_DIM // 2, 1) * sin_hi)

    def put_heads(z, out_ref, head0, dtype):
        z = z.astype(dtype)
        out_ref[0, head0] = z[:, :HEAD_DIM]
        out_ref[0, head0 + 1] = z[:, HEAD_DIM:]

    def put_wide(z, out_ref, head0, tail):
        out_ref[0, head0] = jnp.where(lane < HEAD_DIM, z, tail).astype(BF16)
        out_ref[0, head0 + 1] = jnp.where(lane < HEAD_DIM, pltpu.roll(z, HEAD_DIM, 1), tail).astype(BF16)

    def put_keys(z, out_ref, head0, block_len):
        block = lax.shift_right_logical(seq_pos, block_len.bit_length() - 1)
        put_wide(z, out_ref, head0, jnp.where(lane - HEAD_DIM == block, 1.0, 0.0))

    def slab(c0):
        y = _dot(h, w_ref[:, c0:c0 + 2 * LANES])
        return y[:, :LANES], y[:, LANES:]

    a, b = slab(0)
    put_heads(rope(a), cmp_ref, 0, F32)
    put_heads(b, cmp_ref, 2, F32)
    a, b = slab(256)
    put_keys(rope(a), nks_ref, 0, SLC_LEN)
    put_heads(rope(b), nkw_ref, 0, BF16)
    a, b = slab(512)
    put_heads(a, fk_ref, 0, BF16)
    put_heads(b, fk_ref, 2, BF16)
    a, b = slab(768)
    put_keys(rope(a), mk_ref, 0, MOBA_BLOCK)
    put_keys(rope(b), mk_ref, 2, MOBA_BLOCK)
    f = _dot(h, w_ref[:, N_HEAD_COLS:])
    gm_ref[0] = jax.nn.log_sigmoid(f + fb_ref[...])


def _inproj(x2, mod, g, positions, inv_lane, inv_col, fb_lane, w_p, w_t, *, batch, seq, tm=512):
    n, d = x2.shape
    per_b = seq // tm

    def hm(nh, dtype, width=HEAD_DIM):
        return (pl.BlockSpec((1, nh, tm, width), lambda i: (i // per_b, 0, i % per_b, 0)),
                jax.ShapeDtypeStruct((batch, nh, seq, width), dtype))

    def tr(nh, rows, dtype):
        return (pl.BlockSpec((1, nh, rows, tm), lambda i: (i // per_b, 0, 0, i % per_b)),
                jax.ShapeDtypeStruct((batch, nh, rows, seq), dtype))

    specs = [tr(NSA_HEADS, HEAD_DIM, BF16), tr(4, LANES, BF16), tr(NSA_GROUPS, GATE_ROWS, F32),
             tr(FOX_HEADS, HEAD_DIM, BF16), tr(FOX_HEADS, LANES, BF16),
             tr(MOBA_HEADS, HEAD_DIM, BF16), tr(MOBA_HEADS, LANES, BF16),
             hm(4, F32), hm(2, BF16, LANES), hm(2, BF16), hm(FOX_HEADS, BF16), hm(MOBA_HEADS, BF16, LANES),
             (pl.BlockSpec((1, tm, LANES), lambda i: (i // per_b, i % per_b, 0)),
              jax.ShapeDtypeStruct((batch, seq, LANES), F32))]
    const = lambda shape: pl.BlockSpec(shape, lambda i: (0,) * len(shape))
    return pl.pallas_call(
        functools.partial(_inproj_kernel, per_b=per_b),
        grid=(n // tm,),
        in_specs=[pl.BlockSpec((tm, d), lambda i: (i, 0)),
                  pl.BlockSpec((1, 9, d), lambda i: (i // per_b, 0, 0)),
                  const((1, d)),
                  pl.BlockSpec((tm, 1), lambda i: (i, 0)),
                  pl.BlockSpec((1, 1, tm), lambda i: (i, 0, 0)),
                  const((1, LANES)), const((ROPE_DIM // 2, 1)), const((1, LANES)),
                  const(w_p.shape), const(w_t.shape)],
        out_specs=[s for s, _ in specs],
        out_shape=[s for _, s in specs],
        compiler_params=_params("parallel"),
        name="inproj",
    )(x2, mod, g, positions.reshape(n, 1), positions.reshape(n // tm, 1, tm), inv_lane, inv_col, fb_lane,
      w_p, w_t)


def _compress_kernel(z_ref, pos_ref, w1_ref, w2_ref, w2t_ref, o_ref, ot_ref):
    z = z_ref[0, 0]
    top = _dot((z + pos_ref[0, 0]).astype(BF16), w1_ref[0, 0])
    bot = _dot((z + pos_ref[0, 1]).astype(BF16), w1_ref[0, 1])
    nch = z.shape[0]
    hid = top + pltpu.roll(bot, nch - 1, 0)
    act = jax.nn.gelu(hid).astype(BF16)
    o_ref[0, 0] = _dot(act, w2_ref[0]).astype(o_ref.dtype)
    ot_ref[0, 0] = _dot_nt(w2t_ref[0], act).astype(ot_ref.dtype)


def _compress(cmp_in, pos_l, w1_l, w2_l, w2t_l):
    b, four, nch, width = cmp_in.shape
    return pl.pallas_call(
        _compress_kernel,
        grid=(b, four),
        in_specs=[pl.BlockSpec((1, 1, nch, width), lambda i, n: (i, n, 0, 0)),
                  pl.BlockSpec((1, 2, 1, width), lambda i, n: (n // 2, 0, 0, 0)),
                  pl.BlockSpec((1, 2, width, CMP_HIDDEN), lambda i, n: (n // 2, 0, 0, 0)),
                  pl.BlockSpec((1, CMP_HIDDEN, HEAD_DIM), lambda i, n: (n // 2, 0, 0)),
                  pl.BlockSpec((1, HEAD_DIM, CMP_HIDDEN), lambda i, n: (n // 2, 0, 0))],
        out_specs=[pl.BlockSpec((1, 1, nch, HEAD_DIM), lambda i, n: (i, n, 0, 0)),
                   pl.BlockSpec((1, 1, HEAD_DIM, nch), lambda i, n: (i, n, 0, 0))],
        out_shape=[jax.ShapeDtypeStruct((b, four, nch, HEAD_DIM), BF16),
                   jax.ShapeDtypeStruct((b, four, HEAD_DIM, nch), BF16)],
        compiler_params=_params("parallel", "parallel"),
        name="nsa_compress",
    )(cmp_in, pos_l, w1_l, w2_l, w2t_l)


def _chunk_update_t(carry, s, v_t):
    m, acc = carry
    m_new = jnp.maximum(m, jnp.max(s, axis=0, keepdims=True))
    p = jnp.exp(s - m_new).astype(BF16)
    return m_new, jnp.exp(m - m_new) * acc + _dot(v_t, p)


def _chunk_init_t(n):
    return jnp.full((1, n), NEG, F32), jnp.zeros((LANES, n), F32)


def _chunk_finish_t(carry):
    _, acc = carry
    return acc[:HEAD_DIM] / jnp.maximum(acc[HEAD_DIM:HEAD_DIM + 1], 1e-30)


def _heads_to_rows(heads_t):
    pairs = [jnp.concatenate(heads_t[a:a + 2], axis=0).T for a in range(0, len(heads_t), 2)]
    return jnp.concatenate(pairs, axis=1)


def _nsa_kernel(qt_ref, kc_ref, vct_ref, ks_ref, vst_ref, kw_ref, vwt_ref, gt_ref, ovt_ref, o_ref, *, tq, tk):
    i = pl.program_id(1)
    rep = NSA_REP
    groups = range(NSA_GROUPS)
    n_c = kc_ref.shape[2]
    n_s = ks_ref.shape[2] // SLC_LEN
    tile = lambda a: jnp.concatenate([a] * rep, axis=1)
    t_q = i * tq + _iota((1, tq), 1)
    cend = _iota((n_c, 1), 0) * CMP_STRIDE + (CMP_LEN - 1)
    cm = cend <= tile(t_q)
    j_t = _iota((n_s, tq), 0)
    tb = lax.shift_right_logical(t_q, SLC_LEN.bit_length() - 1)
    valid = j_t <= tb
    forced = (j_t == 0) | (j_t == tb) | (j_t == tb - 1)
    span = WIN + tq
    start = pl.multiple_of(jnp.maximum(i * tq - WIN, 0), tq)
    dist = t_q - (start + _iota((span, 1), 0))
    band = tile(jnp.where((dist >= 0) & (dist < WIN), 0.0, NEG))

    o_cmp, o_win, q4a = [], [], []
    for g in groups:
        q4 = jnp.concatenate([qt_ref[0, rep * g + r] for r in range(rep)], axis=1)
        s = jnp.where(cm, _dot(kc_ref[0, g], q4), NEG)
        e = jnp.where(cm, jnp.exp(s - jnp.max(s, axis=0, keepdims=True)), 0.0)
        p = e / jnp.maximum(jnp.sum(e, axis=0, keepdims=True), 1e-30)
        o_cmp.append(_dot(vct_ref[0, g], p.astype(BF16)))
        psum = p[:, 0:tq] + p[:, tq:2 * tq] + p[:, 2 * tq:3 * tq] + p[:, 3 * tq:4 * tq]
        imp = jnp.dot(ovt_ref[...], psum, precision=HIGHEST, preferred_element_type=F32)[:n_s]
        v_t = jnp.where(valid, jnp.where(forced, jnp.inf, imp), -jnp.inf)
        rank = jnp.zeros((n_s, tq), jnp.int32)
        for jp in range(n_s):
            row = v_t[jp:jp + 1, :]
            ahead = (row > v_t) | ((row == v_t) & (j_t > jp))
            rank = rank + ahead.astype(jnp.int32)
        neg_t = jnp.where((rank < SLC_TOPN) & (v_t > -jnp.inf), 0.0, NEG)
        neg = jnp.concatenate([neg_t, jnp.zeros((HEAD_DIM - n_s, tq), F32)], axis=0)
        q4a.append(jnp.concatenate([q4, tile(neg).astype(BF16)], axis=0))
        sw = _dot(kw_ref[0, g, pl.ds(start, span), :], q4) + band
        pw = jnp.exp(sw - jnp.max(sw, axis=0, keepdims=True)).astype(BF16)
        aw = _dot(vwt_ref[0, g, :, pl.ds(start, span)], pw)
        o_win.append(aw[:HEAD_DIM] / jnp.maximum(aw[HEAD_DIM:HEAD_DIM + 1], 1e-30))

    def slc_step(c, carries, diag):
        k0 = pl.multiple_of(c * tk, tk)
        new = []
        for g in groups:
            sc = _dot(ks_ref[0, g, pl.ds(k0, tk), :], q4a[g])
            if diag:
                sc = sc + tile(jnp.where(k0 + _iota((tk, 1), 0) <= t_q, 0.0, NEG))
            new.append(_chunk_update_t(carries[g], sc, vst_ref[0, g, :, pl.ds(k0, tk)]))
        return tuple(new)

    last = i // (tk // tq)
    carries = lax.fori_loop(0, last, functools.partial(slc_step, diag=False),
                            tuple(_chunk_init_t(rep * tq) for _ in groups))
    carries = slc_step(last, carries, True)

    heads = []
    for g in groups:
        o_slc = _chunk_finish_t(carries[g])
        gates = gt_ref[0, g]
        for r in range(rep):
            cols = slice(r * tq, (r + 1) * tq)
            heads.append(gates[3 * r:3 * r + 1] * o_cmp[g][:, cols]
                         + gates[3 * r + 1:3 * r + 2] * o_slc[:, cols]
                         + gates[3 * r + 2:3 * r + 3] * o_win[g][:, cols])
    o_ref[0] = _heads_to_rows(heads).astype(o_ref.dtype)


def _nsa(qt, cmp_kv, cmp_t, nks, nkw, nvt, gt, ovt, *, tq=128, tk=512):
    b, _, _, seq = qt.shape
    n_c = cmp_kv.shape[2]
    ng = NSA_GROUPS
    return pl.pallas_call(
        functools.partial(_nsa_kernel, tq=tq, tk=tk),
        grid=(b, seq // tq),
        in_specs=[pl.BlockSpec((1, NSA_HEADS, HEAD_DIM, tq), lambda bi, i: (bi, 0, 0, i)),
                  pl.BlockSpec((1, ng, n_c, HEAD_DIM), lambda bi, i: (bi, 0, 0, 0)),
                  pl.BlockSpec((1, ng, HEAD_DIM, n_c), lambda bi, i: (bi, 1, 0, 0)),
                  pl.BlockSpec((1, ng, seq, LANES), lambda bi, i: (bi, 0, 0, 0)),
                  pl.BlockSpec((1, ng, LANES, seq), lambda bi, i: (bi, 0, 0, 0)),
                  pl.BlockSpec((1, ng, seq, HEAD_DIM), lambda bi, i: (bi, 0, 0, 0)),
                  pl.BlockSpec((1, ng, LANES, seq), lambda bi, i: (bi, 1, 0, 0)),
                  pl.BlockSpec((1, ng, GATE_ROWS, tq), lambda bi, i: (bi, 0, 0, i)),
                  pl.BlockSpec((LANES, n_c), lambda bi, i: (0, 0))],
        out_specs=pl.BlockSpec((1, tq, NSA_HEADS * HEAD_DIM), lambda bi, i: (bi, i, 0)),
        out_shape=jax.ShapeDtypeStruct((b, seq, NSA_HEADS * HEAD_DIM), BF16),
        compiler_params=_params("parallel", "arbitrary"),
        name="nsa",
    )(qt, cmp_kv, cmp_t, nks, nvt, nkw, nvt, gt, ovt)


def _decay_kernel(lf_ref, tri_ref, col_ref, row_ref):
    seq = lf_ref.shape[1]
    blk = tri_ref.shape[0]
    carry = jnp.zeros((1, LANES), F32)
    for c in range(seq // blk):
        rows = slice(c * blk, (c + 1) * blk)
        cs = jnp.dot(tri_ref[...], lf_ref[0, rows, :], precision=HIGHEST,
                     preferred_element_type=F32) + carry
        col_ref[0, rows, :] = cs
        carry = cs[blk - 1:blk, :]
    row_ref[0] = col_ref[0].T[8:16, :]


def _decay(gm, tri):
    b, seq, _ = gm.shape
    blk = tri.shape[0]
    return pl.pallas_call(
        _decay_kernel,
        grid=(b,),
        in_specs=[pl.BlockSpec((1, seq, LANES), lambda i: (i, 0, 0)),
                  pl.BlockSpec((blk, blk), lambda i: (0, 0))],
        out_specs=[pl.BlockSpec((1, seq, LANES), lambda i: (i, 0, 0)),
                   pl.BlockSpec((1, 8, seq), lambda i: (i, 0, 0))],
        out_shape=[jax.ShapeDtypeStruct((b, seq, LANES), F32),
                   jax.ShapeDtypeStruct((b, 8, seq), F32)],
        compiler_params=_params("parallel"),
        name="fox_decay",
    )(gm, tri)


def _fox_kernel(qt_ref, k_ref, vt_ref, fcol_ref, frow_ref, tri_ref, o_ref, *, tq):
    i = pl.program_id(1)
    heads = range(FOX_HEADS)
    q0 = pl.multiple_of(i * tq, tq)
    qs = [qt_ref[0, h] for h in heads]
    fqs = [frow_ref[0, 4 + h:5 + h, pl.ds(q0, tq)] for h in heads]

    def step(c, carries, diag):
        k0 = pl.multiple_of(c * tq, tq)
        new = []
        for h in heads:
            fk = fcol_ref[0, pl.ds(k0, tq), GATE_LANES + h:GATE_LANES + h + 1]
            sc = _dot(k_ref[0, h, pl.ds(k0, tq), :], qs[h]) + (fqs[h] - fk)
            if diag:
                sc = sc + tri_ref[...]
            new.append(_chunk_update_t(carries[h], sc, vt_ref[0, h, :, pl.ds(k0, tq)]))
        return tuple(new)

    carries = step(i, tuple(_chunk_init_t(tq) for _ in heads), True)
    carries = lax.fori_loop(0, i, functools.partial(step, diag=False), carries)
    o_ref[0] = _heads_to_rows([_chunk_finish_t(cr) for cr in carries]).astype(o_ref.dtype)


def _fox(fqt, fk, fvt, fcol, frow, tri_t, *, tq=FOX_TILE):
    b, nh, _, seq = fqt.shape
    return pl.pallas_call(
        functools.partial(_fox_kernel, tq=tq),
        grid=(b, seq // tq),
        in_specs=[pl.BlockSpec((1, nh, HEAD_DIM, tq), lambda bi, i: (bi, 0, 0, i)),
                  pl.BlockSpec((1, nh, seq, HEAD_DIM), lambda bi, i: (bi, 0, 0, 0)),
                  pl.BlockSpec((1, nh, LANES, seq), lambda bi, i: (bi, 0, 0, 0)),
                  pl.BlockSpec((1, seq, LANES), lambda bi, i: (bi, 0, 0)),
                  pl.BlockSpec((1, 8, seq), lambda bi, i: (bi, 0, 0)),
                  pl.BlockSpec((tq, tq), lambda bi, i: (0, 0))],
        out_specs=pl.BlockSpec((1, tq, nh * HEAD_DIM), lambda bi, i: (bi, i, 0)),
        out_shape=jax.ShapeDtypeStruct((b, seq, nh * HEAD_DIM), BF16),
        compiler_params=_params("parallel", "arbitrary"),
        name="fox",
    )(fqt, fk, fvt, fcol, frow, tri_t)


def _moba_kernel(qt_ref, k_ref, vt_ref, avg_ref, tri_ref, o_ref, km_ref, *, tq):
    i = pl.program_id(1)
    nb = avg_ref.shape[0]

    @pl.when(i == 0)
    def _():
        for h in range(MOBA_HEADS):
            km_ref[h] = _dot(avg_ref[...], k_ref[0, h])

    blk = _iota((nb, tq), 0)
    own = i * (tq // MOBA_BLOCK) + lax.shift_right_logical(_iota((nb, tq), 1), MOBA_BLOCK.bit_length() - 1)
    qas = []
    for h in range(MOBA_HEADS):
        q = qt_ref[0, h]
        gate = jnp.dot(km_ref[h][:, :HEAD_DIM], q.astype(F32), precision=HIGHEST,
                       preferred_element_type=F32)
        val = jnp.where(blk < own, gate, -jnp.inf)
        rank = jnp.zeros((nb, tq), jnp.int32)
        for jp in range(nb):
            row = val[jp:jp + 1, :]
            ahead = (row > val) | ((row == val) & (blk > jp))
            rank = rank + ahead.astype(jnp.int32)
        keep = ((rank < MOBA_TOPK) & (blk < own)) | (blk == own)
        neg = jnp.concatenate([jnp.where(keep, 0.0, NEG), jnp.zeros((HEAD_DIM - nb, tq), F32)], axis=0)
        qas.append(jnp.concatenate([q, neg.astype(BF16)], axis=0))

    def step(c, carries, diag):
        k0 = pl.multiple_of(c * tq, tq)
        new = []
        for h in range(MOBA_HEADS):
            sc = _dot(k_ref[0, h, pl.ds(k0, tq), :], qas[h])
            if diag:
                sc = sc + tri_ref[...]
            new.append(_chunk_update_t(carries[h], sc, vt_ref[0, h, :, pl.ds(k0, tq)]))
        return tuple(new)

    carries = step(i, tuple(_chunk_init_t(tq) for _ in range(MOBA_HEADS)), True)
    carries = lax.fori_loop(0, i, functools.partial(step, diag=False), carries)
    o_ref[0] = _heads_to_rows([_chunk_finish_t(cr) for cr in carries]).astype(o_ref.dtype)


def _moba(mqt, mk, mvt, avg, tri_t, *, tq=MOBA_TILE):
    b, nh, _, seq = mqt.shape
    nb = avg.shape[0]
    return pl.pallas_call(
        functools.partial(_moba_kernel, tq=tq),
        grid=(b, seq // tq),
        in_specs=[pl.BlockSpec((1, nh, HEAD_DIM, tq), lambda bi, i: (bi, 0, 0, i)),
                  pl.BlockSpec((1, nh, seq, LANES), lambda bi, i: (bi, 0, 0, 0)),
                  pl.BlockSpec((1, nh, LANES, seq), lambda bi, i: (bi, 0, 0, 0)),
                  pl.BlockSpec((nb, seq), lambda bi, i: (0, 0)),
                  pl.BlockSpec((tq, tq), lambda bi, i: (0, 0))],
        out_specs=pl.BlockSpec((1, tq, nh * HEAD_DIM), lambda bi, i: (bi, i, 0)),
        out_shape=jax.ShapeDtypeStruct((b, seq, nh * HEAD_DIM), BF16),
        scratch_shapes=[pltpu.VMEM((nh, nb, LANES), F32)],
        compiler_params=_params("parallel", "arbitrary"),
        name="moba",
    )(mqt, mk, mvt, avg, tri_t)


def _outproj_kernel(x_ref, mod_ref, on_ref, of_ref, om_ref, w_ref, o_ref):
    n_w = NSA_HEADS * HEAD_DIM
    f_w = FOX_HEADS * HEAD_DIM
    acc = _dot(on_ref[...], w_ref[0:n_w, :])
    acc += _dot(of_ref[...], w_ref[n_w:n_w + f_w, :])
    acc += _dot(om_ref[...], w_ref[n_w + f_w:, :])
    o_ref[...] = x_ref[...] + mod_ref[0, 5:6, :] * acc


def _outproj(x2, mod, o_nsa, o_fox, o_moba, w_out, *, seq, tm=512):
    n, d = x2.shape
    per_b = seq // tm
    row = lambda width: pl.BlockSpec((tm, width), lambda i: (i, 0))
    return pl.pallas_call(
        _outproj_kernel,
        grid=(n // tm,),
        in_specs=[row(d), pl.BlockSpec((1, 9, d), lambda i: (i // per_b, 0, 0)),
                  row(o_nsa.shape[1]), row(o_fox.shape[1]), row(o_moba.shape[1]),
                  pl.BlockSpec((d, d), lambda i: (0, 0))],
        out_specs=row(d),
        out_shape=jax.ShapeDtypeStruct((n, d), F32),
        compiler_params=_params("parallel"),
        name="outproj",
    )(x2, mod, o_nsa, o_fox, o_moba, w_out)


def _pack_w_in(w_in):
    depth, d, _ = w_in.shape

    def cols(a, b):
        return w_in[:, :, a:b]

    zeros = lambda n: jnp.zeros((depth, d, n), w_in.dtype)
    g_half = 3 * NSA_REP
    token_major = [cols(_OFF_KC, _OFF_KS),
                   cols(_OFF_KS, _OFF_VS), cols(_OFF_KW, _OFF_VW),
                   cols(_OFF_FOX_K, _OFF_FOX_V),
                   cols(_OFF_MOBA_K, _OFF_MOBA_V),
                   zeros(GATE_LANES), cols(_OFF_FOX_F, _OFF_FOX_F + FOX_HEADS),
                   zeros(LANES - GATE_LANES - FOX_HEADS)]
    transposed = [cols(_OFF_NSA_Q, _OFF_KC),
                  cols(_OFF_VS, _OFF_KW), cols(_OFF_VW, _OFF_GATE),
                  cols(_OFF_GATE, _OFF_GATE + g_half), zeros(GATE_ROWS - g_half),
                  cols(_OFF_GATE + g_half, _OFF_GATE + 2 * g_half), zeros(LANES - GATE_ROWS - g_half),
                  cols(_OFF_FOX_Q, _OFF_FOX_K), cols(_OFF_FOX_V, _OFF_FOX_F),
                  cols(_OFF_MOBA_Q, _OFF_MOBA_K), cols(_OFF_MOBA_V, IN_COLS)]
    w_p = jnp.concatenate(token_major, axis=-1).astype(BF16)
    w_t = jnp.swapaxes(jnp.concatenate(transposed, axis=-1), 1, 2).astype(BF16)
    return w_p, w_t


def _constants(seq):
    n_c = seq // CMP_STRIDE
    n_s = seq // SLC_LEN
    c0 = np.arange(n_c)[None, :] * CMP_STRIDE
    j0 = np.arange(LANES)[:, None] * SLC_LEN
    real = (np.arange(n_c)[None, :] < (seq - CMP_LEN) // CMP_STRIDE + 1) & (np.arange(LANES)[:, None] < n_s)
    ov = ((c0 < j0 + SLC_LEN) & (c0 + CMP_LEN > j0) & real).astype(np.float32)
    nb = seq // MOBA_BLOCK
    avg = (np.arange(nb)[:, None] == (np.arange(seq)[None, :] // MOBA_BLOCK)).astype(np.float32) / MOBA_BLOCK
    tri = np.tril(np.ones((256, 256), np.float32))

    kq = np.arange(FOX_TILE)
    causal_fox = np.where(kq[:, None] <= kq[None, :], 0.0, NEG).astype(np.float32)
    r = np.arange(MOBA_TILE)
    same = (r[:, None] // MOBA_BLOCK) == (r[None, :] // MOBA_BLOCK)
    allowed = np.where(same, r[:, None] <= r[None, :], r[:, None] < r[None, :])
    causal_moba = np.where(allowed, 0.0, NEG).astype(np.float32)
    return dict(ov=jnp.asarray(ov), avg=jnp.asarray(avg, dtype=BF16), tri=jnp.asarray(tri),
                causal_fox=jnp.asarray(causal_fox), causal_moba=jnp.asarray(causal_moba))


def kernel(x, c, positions, norm_g, w_ada, b_ada, w_in, fox_fbias, cmp_pos, cmp_w1, cmp_w2, w_out,
           ffn_w13, ffn_w2, final_g):
    batch, seq, d = x.shape
    depth = w_ada.shape[0]
    assert d == D_MODEL and seq % MOBA_BLOCK == 0 and seq % 1024 == 0

    w_in_p, w_in_t = _pack_w_in(w_in)
    w13 = ffn_w13.astype(BF16)
    w2 = ffn_w2.astype(BF16)
    w_o = w_out.astype(BF16)
    half = CMP_LEN * HEAD_DIM // 2
    cw1 = cmp_w1.astype(BF16).reshape(depth, 2, 2, half, CMP_HIDDEN)
    cw2 = cmp_w2.astype(BF16)
    cw2t = jnp.swapaxes(cw2, 2, 3)
    cpos = cmp_pos.reshape(depth, 2, 2, 1, half)
    cst = _constants(seq)

    inv = ROPE_THETA ** (-jnp.arange(0, ROPE_DIM, 2, dtype=F32) / ROPE_DIM)
    lane = np.arange(LANES) % HEAD_DIM
    inv_lane = jnp.where(jnp.asarray(lane < ROPE_DIM), inv[jnp.asarray(lane % (ROPE_DIM // 2))], 0.0)
    inv_lane = inv_lane.reshape(1, LANES).astype(F32)
    inv_col = inv.reshape(ROPE_DIM // 2, 1)
    fb_lane = jnp.zeros((depth, 1, LANES), F32).at[:, 0, GATE_LANES:GATE_LANES + FOX_HEADS].set(fox_fbias)

    mod_all = _ada_mod(c, w_ada, b_ada).reshape(depth, batch, 9, d)
    x2 = x.reshape(batch * seq, d)
    fg = final_g.reshape(1, d)
    for l in range(depth):
        mod = mod_all[l]
        x2 = _ffn(x2, mod, norm_g[l, 0:1], fg, w13[l, 0], w2[l, 0], sub=0, seq=seq, final=False)
        qt, nvt, gt, fqt, fvt, mqt, mvt, cmp_in, nks, nkw, fk, mk, gm = _inproj(
            x2, mod, norm_g[l, 1:2], positions, inv_lane, inv_col, fb_lane[l], w_in_p[l], w_in_t[l],
            batch=batch, seq=seq)
        cmp_kv, cmp_t = _compress(cmp_in.reshape(batch, 4, seq // CMP_STRIDE, CMP_STRIDE * HEAD_DIM),
                                  cpos[l], cw1[l], cw2[l], cw2t[l])
        o_nsa = _nsa(qt, cmp_kv, cmp_t, nks, nkw, nvt, gt, cst["ov"])
        fcol, frow = _decay(gm, cst["tri"])
        o_fox = _fox(fqt, fk, fvt, fcol, frow, cst["causal_fox"])
        o_moba = _moba(mqt, mk, mvt, cst["avg"], cst["causal_moba"])
        x2 = _outproj(x2, mod, o_nsa.reshape(batch * seq, -1), o_fox.reshape(batch * seq, -1),
                      o_moba.reshape(batch * seq, -1), w_o[l], seq=seq)
        x2 = _ffn(x2, mod, norm_g[l, 2:3], fg, w13[l, 1], w2[l, 1], sub=2, seq=seq,
                  final=(l == depth - 1))
    return x2.reshape(batch, seq, d)
```

```python
import functools

import numpy as np
import jax
import jax.numpy as jnp
from jax import lax
from jax.experimental import pallas as pl
from jax.experimental.pallas import tpu as pltpu

D_MODEL = 1024
HEAD_DIM = 64
NSA_HEADS = 8
NSA_GROUPS = 2
NSA_REP = NSA_HEADS // NSA_GROUPS
FOX_HEADS = 4
MOBA_HEADS = 4
ROPE_DIM = HEAD_DIM // 4
ROPE_THETA = 500000.0
CMP_LEN = 32
CMP_STRIDE = 16
CMP_HIDDEN = 4 * HEAD_DIM
SLC_LEN = 64
SLC_TOPN = 8
WIN = 512
MOBA_BLOCK = 256
MOBA_TOPK = 3
D_FF = 2816
EPS = 1e-6
ATT_SCALE = HEAD_DIM ** -0.5

FOX_TILE = 512
MOBA_TILE = 2 * MOBA_BLOCK
LANES = 128
MXU_TILE = 256
NEG = -1e30
VMEM_LIMIT = 56 * 1024 * 1024

F32 = jnp.float32
BF16 = jnp.bfloat16
HIGHEST = lax.Precision.HIGHEST

_KV = NSA_GROUPS * HEAD_DIM
_OFF_NSA_Q = 0
_OFF_KC = _OFF_NSA_Q + NSA_HEADS * HEAD_DIM
_OFF_VC = _OFF_KC + _KV
_OFF_KS = _OFF_VC + _KV
_OFF_VS = _OFF_KS + _KV
_OFF_KW = _OFF_VS + _KV
_OFF_VW = _OFF_KW + _KV
_OFF_GATE = _OFF_VW + _KV
_OFF_FOX_Q = _OFF_GATE + 3 * NSA_HEADS
_OFF_FOX_K = _OFF_FOX_Q + FOX_HEADS * HEAD_DIM
_OFF_FOX_V = _OFF_FOX_K + FOX_HEADS * HEAD_DIM
_OFF_FOX_F = _OFF_FOX_V + FOX_HEADS * HEAD_DIM
_OFF_MOBA_Q = _OFF_FOX_F + FOX_HEADS
_OFF_MOBA_K = _OFF_MOBA_Q + MOBA_HEADS * HEAD_DIM
_OFF_MOBA_V = _OFF_MOBA_K + MOBA_HEADS * HEAD_DIM
IN_COLS = _OFF_MOBA_V + MOBA_HEADS * HEAD_DIM

N_HEAD_COLS = 1024
GATE_LANES = 3 * NSA_REP
PROJ_COLS = N_HEAD_COLS + LANES
GATE_ROWS = 16
_T_NSA_Q = 0
_T_NSA_V = _T_NSA_Q + NSA_HEADS * HEAD_DIM
_T_GATE = _T_NSA_V + 4 * HEAD_DIM
_T_FOX_Q = _T_GATE + LANES
_T_FOX_V = _T_FOX_Q + FOX_HEADS * HEAD_DIM
_T_MOBA_Q = _T_FOX_V + FOX_HEADS * HEAD_DIM
_T_MOBA_V = _T_MOBA_Q + MOBA_HEADS * HEAD_DIM
T_ROWS = _T_MOBA_V + MOBA_HEADS * HEAD_DIM


def _dot(a, b):
    return jnp.dot(a, b, preferred_element_type=F32)


def _dot_nt(a, b, precision=None):
    return lax.dot_general(a, b, (((1,), (1,)), ((), ())), precision=precision,
                           preferred_element_type=F32)


def _iota(shape, dim):
    return lax.broadcasted_iota(jnp.int32, shape, dim)


def _params(*sem, flags=None):
    return pltpu.CompilerParams(dimension_semantics=sem, vmem_limit_bytes=VMEM_LIMIT, flags=flags)


def _ada_norm(x, g, shift, scale):
    ms = jnp.mean(x * x, axis=-1, keepdims=True)
    y = x * lax.rsqrt(ms + EPS) * g
    return y * (1.0 + scale) + shift


def _ada_kernel(c_ref, w_ref, b_ref, o_ref):
    c = c_ref[...]
    ca = c * jax.nn.sigmoid(c)
    o_ref[0] = jnp.dot(ca, w_ref[0], precision=HIGHEST, preferred_element_type=F32) + b_ref[0]


def _ada_mod(c, w_ada, b_ada):
    depth, d, n = w_ada.shape
    b = c.shape[0]
    tn = 1152
    return pl.pallas_call(
        _ada_kernel,
        grid=(depth, n // tn),
        in_specs=[pl.BlockSpec((b, d), lambda l, j: (0, 0)),
                  pl.BlockSpec((1, d, tn), lambda l, j: (l, 0, j)),
                  pl.BlockSpec((1, 1, tn), lambda l, j: (l, 0, j))],
        out_specs=pl.BlockSpec((1, b, tn), lambda l, j: (l, 0, j)),
        out_shape=jax.ShapeDtypeStruct((depth, b, n), F32),
        compiler_params=_params("parallel", "parallel"),
        name="ada_mod",
    )(c, w_ada, b_ada.reshape(depth, 1, n))


def _swiglu_residual(x, mod_ref, g_ref, fg_ref, w13_ref, w2_ref, *, sub, coef, final, splits):
    h = _ada_norm(x, g_ref[...], mod_ref[0, 3 * sub:3 * sub + 1, :],
                  mod_ref[0, 3 * sub + 1:3 * sub + 2, :]).astype(BF16)
    acc = None
    for lo, hi in splits:
        a = _dot(h, w13_ref[:, lo:hi])
        b = _dot(h, w13_ref[:, D_FF + lo:D_FF + hi])
        u = (a * jax.nn.sigmoid(a) * b).astype(BF16)
        part = _dot(u, w2_ref[lo:hi, :])
        acc = part if acc is None else acc + part
    y = x + (coef * mod_ref[0, 3 * sub + 2:3 * sub + 3, :]) * acc
    if final:
        y = y * lax.rsqrt(jnp.mean(y * y, axis=-1, keepdims=True) + EPS) * fg_ref[...]
    return y


def _ffn_kernel(x_ref, mod_ref, g_ref, fg_ref, w13_ref, w2_ref, o_ref, **kw):
    o_ref[...] = _swiglu_residual(x_ref[...], mod_ref, g_ref, fg_ref, w13_ref, w2_ref, **kw)


def _mix_ffn_kernel(x_ref, mod_ref, g_ref, fg_ref, on_ref, of_ref, om_ref, wo_ref, w13_ref, w2_ref, o_ref, **kw):
    n_w = NSA_HEADS * HEAD_DIM
    f_w = FOX_HEADS * HEAD_DIM
    mix = _dot(on_ref[...], wo_ref[0:n_w, :])
    mix += _dot(of_ref[...], wo_ref[n_w:n_w + f_w, :])
    mix += _dot(om_ref[...], wo_ref[n_w + f_w:, :])
    x = x_ref[...] + mod_ref[0, 5:6, :] * mix
    o_ref[...] = _swiglu_residual(x, mod_ref, g_ref, fg_ref, w13_ref, w2_ref, **kw)


def _resident(shape):
    return pl.BlockSpec(shape, lambda i: (0,) * len(shape), pipeline_mode=pl.Buffered(1))


def _hidden_splits(chunk):
    return tuple((lo, min(lo + chunk, D_FF)) for lo in range(0, D_FF, chunk))


def _ffn(x2, mod, g, final_g, w13, w2, *, sub, seq, final, tm=1024, chunk=3 * MXU_TILE):
    n, d = x2.shape
    per_b = seq // tm
    return pl.pallas_call(
        functools.partial(_ffn_kernel, sub=sub, coef=0.5, final=final, splits=_hidden_splits(chunk)),
        grid=(n // tm,),
        in_specs=[pl.BlockSpec((tm, d), lambda i: (i, 0)),
                  pl.BlockSpec((1, 9, d), lambda i: (i // per_b, 0, 0)),
                  pl.BlockSpec((1, d), lambda i: (0, 0)),
                  pl.BlockSpec((1, d), lambda i: (0, 0)),
                  _resident((d, 2 * D_FF)), _resident((D_FF, d))],
        out_specs=pl.BlockSpec((tm, d), lambda i: (i, 0)),
        out_shape=jax.ShapeDtypeStruct((n, d), F32),
        compiler_params=_params("parallel"),
        name="ffn",
    )(x2, mod, g, final_g, w13, w2)


def _mix_ffn(x2, mod, g, final_g, o_nsa, o_fox, o_moba, w_out, w13, w2, *, seq, final, tm=512,
             chunk=4 * MXU_TILE):
    n, d = x2.shape
    per_b = seq // tm
    row = lambda width: pl.BlockSpec((tm, width), lambda i: (i, 0))
    return pl.pallas_call(
        functools.partial(_mix_ffn_kernel, sub=2, coef=0.5, final=final, splits=_hidden_splits(chunk)),
        grid=(n // tm,),
        in_specs=[row(d),
                  pl.BlockSpec((1, 9, d), lambda i: (i // per_b, 0, 0)),
                  pl.BlockSpec((1, d), lambda i: (0, 0)),
                  pl.BlockSpec((1, d), lambda i: (0, 0)),
                  row(o_nsa.shape[1]), row(o_fox.shape[1]), row(o_moba.shape[1]),
                  _resident((d, d)), _resident((d, 2 * D_FF)), _resident((D_FF, d))],
        out_specs=row(d),
        out_shape=jax.ShapeDtypeStruct((n, d), F32),
        compiler_params=_params("parallel"),
        name="mix_ffn",
    )(x2, mod, g, final_g, o_nsa, o_fox, o_moba, w_out, w13, w2)


def _inproj_kernel(x_ref, mod_ref, g_ref, pos_ref, posr_ref, inv_ref, invc_ref, fb_ref, w_ref, wt_ref,
                   qt_ref, nvt_ref, gt_ref, fqt_ref, fvt_ref, mqt_ref, mvt_ref,
                   cmp_ref, nks_ref, nkw_ref, fk_ref, mk_ref, gm_ref, *, per_b):
    h = _ada_norm(x_ref[...], g_ref[...], mod_ref[0, 3:4, :], mod_ref[0, 4:5, :]).astype(BF16)
    tm = h.shape[0]
    seq_pos = (pl.program_id(0) % per_b) * tm + _iota((tm, LANES), 0)

    half = ROPE_DIM // 2
    ang_t = invc_ref[...] * posr_ref[0].astype(F32)
    cos_t = jnp.cos(ang_t)
    sin_t = jnp.sin(ang_t)
    ones_row = jnp.where(_iota((HEAD_DIM, tm), 0) == 0, 1.0, 0.0)
    slab_rows = 4 * HEAD_DIM

    def queries_t(row0, n_heads, out_ref, rotate):
        for s_idx in range(n_heads // 4):
            y_t = _dot_nt(wt_ref[row0 + slab_rows * s_idx:row0 + slab_rows * (s_idx + 1), :], h)
            for hh in range(4):
                blk = y_t[HEAD_DIM * hh:HEAD_DIM * (hh + 1)]
                if rotate:
                    r1, r2 = blk[:half], blk[half:2 * half]
                    blk = jnp.concatenate([r1 * cos_t - r2 * sin_t, r2 * cos_t + r1 * sin_t, blk[2 * half:]],
                                          axis=0)
                out_ref[0, 4 * s_idx + hh] = (blk * ATT_SCALE).astype(BF16)

    def values_t(row0, out_ref):
        y_t = _dot_nt(wt_ref[row0:row0 + slab_rows, :], h)
        for idx in range(4):
            out_ref[0, idx] = jnp.concatenate([y_t[HEAD_DIM * idx:HEAD_DIM * (idx + 1)], ones_row],
                                              axis=0).astype(BF16)

    queries_t(_T_NSA_Q, NSA_HEADS, qt_ref, True)
    values_t(_T_NSA_V, nvt_ref)
    sig_t = jax.nn.sigmoid(_dot_nt(wt_ref[_T_GATE:_T_GATE + LANES, :], h))
    for g in range(NSA_GROUPS):
        gt_ref[0, g] = sig_t[GATE_ROWS * g:GATE_ROWS * (g + 1)]
    queries_t(_T_FOX_Q, FOX_HEADS, fqt_ref, False)
    values_t(_T_FOX_V, fvt_ref)
    queries_t(_T_MOBA_Q, MOBA_HEADS, mqt_ref, True)
    values_t(_T_MOBA_V, mvt_ref)

    ang = pos_ref[...].astype(F32) * inv_ref[...]
    cosl = jnp.cos(ang)
    sinl = jnp.sin(ang)
    lane = _iota((tm, LANES), 1)
    hl = lane & (HEAD_DIM - 1)
    sin_lo = jnp.where(hl < ROPE_DIM // 2, -sinl, 0.0)
    sin_hi = jnp.where(hl >= ROPE_DIM // 2, sinl, 0.0)

    def rope(y):
        return (y * cosl + pltpu.roll(y, LANES - ROPE_DIM // 2, 1) * sin_lo
                + pltpu.roll(y, ROPE_DIM // 2, 1) * sin_hi)

    def put_heads(z, out_ref, head0, dtype):
        z = z.astype(dtype)
        out_ref[0, head0] = z[:, :HEAD_DIM]
        out_ref[0, head0 + 1] = z[:, HEAD_DIM:]

    def put_wide(z, out_ref, head0, tail):
        out_ref[0, head0] = jnp.where(lane < HEAD_DIM, z, tail).astype(BF16)
        out_ref[0, head0 + 1] = jnp.where(lane < HEAD_DIM, pltpu.roll(z, HEAD_DIM, 1), tail).astype(BF16)

    def put_keys(z, out_ref, head0, block_len):
        block = lax.shift_right_logical(seq_pos, block_len.bit_length() - 1)
        put_wide(z, out_ref, head0, jnp.where(lane - HEAD_DIM == block, 1.0, 0.0))

    def slab(c0):
        y = _dot(h, w_ref[:, c0:c0 + 2 * LANES])
        return y[:, :LANES], y[:, LANES:]

    a, b = slab(0)
    put_heads(rope(a), cmp_ref, 0, F32)
    put_heads(b, cmp_ref, 2, F32)
    a, b = slab(256)
    put_keys(rope(a), nks_ref, 0, SLC_LEN)
    put_heads(rope(b), nkw_ref, 0, BF16)
    a, b = slab(512)
    put_heads(a, fk_ref, 0, BF16)
    put_heads(b, fk_ref, 2, BF16)
    a, b = slab(768)
    put_keys(rope(a), mk_ref, 0, MOBA_BLOCK)
    put_keys(rope(b), mk_ref, 2, MOBA_BLOCK)
    f = _dot(h, w_ref[:, N_HEAD_COLS:])
    gm_ref[0] = jax.nn.log_sigmoid(f + fb_ref[...])


def _inproj(x2, mod, g, positions, inv_lane, inv_col, fb_lane, w_p, w_t, *, batch, seq, tm=512):
    n, d = x2.shape
    per_b = seq // tm

    def hm(nh, dtype, width=HEAD_DIM):
        return (pl.BlockSpec((1, nh, tm, width), lambda i: (i // per_b, 0, i % per_b, 0)),
                jax.ShapeDtypeStruct((batch, nh, seq, width), dtype))

    def tr(nh, rows, dtype):
        return (pl.BlockSpec((1, nh, rows, tm), lambda i: (i // per_b, 0, 0, i % per_b)),
                jax.ShapeDtypeStruct((batch, nh, rows, seq), dtype))

    specs = [tr(NSA_HEADS, HEAD_DIM, BF16), tr(4, LANES, BF16), tr(NSA_GROUPS, GATE_ROWS, F32),
             tr(FOX_HEADS, HEAD_DIM, BF16), tr(FOX_HEADS, LANES, BF16),
             tr(MOBA_HEADS, HEAD_DIM, BF16), tr(MOBA_HEADS, LANES, BF16),
             hm(4, F32), hm(2, BF16, LANES), hm(2, BF16), hm(FOX_HEADS, BF16), hm(MOBA_HEADS, BF16, LANES),
             (pl.BlockSpec((1, tm, LANES), lambda i: (i // per_b, i % per_b, 0)),
              jax.ShapeDtypeStruct((batch, seq, LANES), F32))]
    const = lambda shape: pl.BlockSpec(shape, lambda i: (0,) * len(shape))
    return pl.pallas_call(
        functools.partial(_inproj_kernel, per_b=per_b),
        grid=(n // tm,),
        in_specs=[pl.BlockSpec((tm, d), lambda i: (i, 0)),
                  pl.BlockSpec((1, 9, d), lambda i: (i // per_b, 0, 0)),
                  const((1, d)),
                  pl.BlockSpec((tm, 1), lambda i: (i, 0)),
                  pl.BlockSpec((1, 1, tm), lambda i: (i, 0, 0)),
                  const((1, LANES)), const((ROPE_DIM // 2, 1)), const((1, LANES)),
                  const(w_p.shape), const(w_t.shape)],
        out_specs=[s for s, _ in specs],
        out_shape=[s for _, s in specs],
        compiler_params=_params("parallel"),
        name="inproj",
    )(x2, mod, g, positions.reshape(n, 1), positions.reshape(n // tm, 1, tm), inv_lane, inv_col, fb_lane,
      w_p, w_t)


def _compress_kernel(z_ref, pos_ref, w1_ref, w2_ref, w2t_ref, o_ref, ot_ref):
    z = z_ref[0, 0]
    top = _dot((z + pos_ref[0, 0]).astype(BF16), w1_ref[0, 0])
    bot = _dot((z + pos_ref[0, 1]).astype(BF16), w1_ref[0, 1])
    nch = z.shape[0]
    hid = top + pltpu.roll(bot, nch - 1, 0)
    act = jax.nn.gelu(hid).astype(BF16)
    o_ref[0, 0] = _dot(act, w2_ref[0]).astype(o_ref.dtype)
    ot_ref[0, 0] = _dot_nt(w2t_ref[0], act).astype(ot_ref.dtype)


def _compress(cmp_in, pos_l, w1_l, w2_l, w2t_l):
    b, four, nch, width = cmp_in.shape
    return pl.pallas_call(
        _compress_kernel,
        grid=(b, four),
        in_specs=[pl.BlockSpec((1, 1, nch, width), lambda i, n: (i, n, 0, 0)),
                  pl.BlockSpec((1, 2, 1, width), lambda i, n: (n // 2, 0, 0, 0)),
                  pl.BlockSpec((1, 2, width, CMP_HIDDEN), lambda i, n: (n // 2, 0, 0, 0)),
                  pl.BlockSpec((1, CMP_HIDDEN, HEAD_DIM), lambda i, n: (n // 2, 0, 0)),
                  pl.BlockSpec((1, HEAD_DIM, CMP_HIDDEN), lambda i, n: (n // 2, 0, 0))],
        out_specs=[pl.BlockSpec((1, 1, nch, HEAD_DIM), lambda i, n: (i, n, 0, 0)),
                   pl.BlockSpec((1, 1, HEAD_DIM, nch), lambda i, n: (i, n, 0, 0))],
        out_shape=[jax.ShapeDtypeStruct((b, four, nch, HEAD_DIM), BF16),
                   jax.ShapeDtypeStruct((b, four, HEAD_DIM, nch), BF16)],
        compiler_params=_params("parallel", "parallel"),
        name="nsa_compress",
    )(cmp_in, pos_l, w1_l, w2_l, w2t_l)


def _chunk_update_t(carry, s, v_t):
    m, acc = carry
    m_new = jnp.maximum(m, jnp.max(s, axis=0, keepdims=True))
    p = jnp.exp(s - m_new).astype(BF16)
    return m_new, jnp.exp(m - m_new) * acc + _dot(v_t, p)


def _chunk_init_t(n):
    return jnp.full((1, n), NEG, F32), jnp.zeros((LANES, n), F32)


def _chunk_finish_t(carry):
    _, acc = carry
    return acc[:HEAD_DIM] / jnp.maximum(acc[HEAD_DIM:HEAD_DIM + 1], 1e-30)


def _heads_to_rows(heads_t):
    pairs = [jnp.concatenate(heads_t[a:a + 2], axis=0).T for a in range(0, len(heads_t), 2)]
    return jnp.concatenate(pairs, axis=1)


def _nsa_kernel(qt_ref, kc_ref, vct_ref, ks_ref, vst_ref, kw_ref, vwt_ref, gt_ref, ovt_ref, o_ref, *, tq, tk):
    i = pl.program_id(1)
    rep = NSA_REP
    groups = range(NSA_GROUPS)
    n_c = kc_ref.shape[2]
    n_s = ks_ref.shape[2] // SLC_LEN
    tile = lambda a: jnp.concatenate([a] * rep, axis=1)
    t_q = i * tq + _iota((1, tq), 1)
    cend = _iota((n_c, 1), 0) * CMP_STRIDE + (CMP_LEN - 1)
    cm = cend <= tile(t_q)
    j_t = _iota((n_s, tq), 0)
    tb = lax.shift_right_logical(t_q, SLC_LEN.bit_length() - 1)
    valid = j_t <= tb
    forced = (j_t == 0) | (j_t == tb) | (j_t == tb - 1)
    span = WIN + tq
    start = pl.multiple_of(jnp.maximum(i * tq - WIN, 0), tq)
    dist = t_q - (start + _iota((span, 1), 0))
    band = tile(jnp.where((dist >= 0) & (dist < WIN), 0.0, NEG))

    o_cmp, o_win, q4a = [], [], []
    for g in groups:
        q4 = jnp.concatenate([qt_ref[0, rep * g + r] for r in range(rep)], axis=1)
        s = jnp.where(cm, _dot(kc_ref[0, g], q4), NEG)
        e = jnp.where(cm, jnp.exp(s - jnp.max(s, axis=0, keepdims=True)), 0.0)
        p = e / jnp.maximum(jnp.sum(e, axis=0, keepdims=True), 1e-30)
        o_cmp.append(_dot(vct_ref[0, g], p.astype(BF16)))
        psum = p[:, 0:tq] + p[:, tq:2 * tq] + p[:, 2 * tq:3 * tq] + p[:, 3 * tq:4 * tq]
        imp = jnp.dot(ovt_ref[...], psum, precision=HIGHEST, preferred_element_type=F32)[:n_s]
        v_t = jnp.where(valid, jnp.where(forced, jnp.inf, imp), -jnp.inf)
        rank = jnp.zeros((n_s, tq), jnp.int32)
        for jp in range(n_s):
            row = v_t[jp:jp + 1, :]
            ahead = (row > v_t) | ((row == v_t) & (j_t > jp))
            rank = rank + ahead.astype(jnp.int32)
        neg_t = jnp.where((rank < SLC_TOPN) & (v_t > -jnp.inf), 0.0, NEG)
        neg = jnp.concatenate([neg_t, jnp.zeros((HEAD_DIM - n_s, tq), F32)], axis=0)
        q4a.append(jnp.concatenate([q4, tile(neg).astype(BF16)], axis=0))
        sw = _dot(kw_ref[0, g, pl.ds(start, span), :], q4) + band
        pw = jnp.exp(sw - jnp.max(sw, axis=0, keepdims=True)).astype(BF16)
        aw = _dot(vwt_ref[0, g, :, pl.ds(start, span)], pw)
        o_win.append(aw[:HEAD_DIM] / jnp.maximum(aw[HEAD_DIM:HEAD_DIM + 1], 1e-30))

    def slc_step(c, carries, diag):
        k0 = pl.multiple_of(c * tk, tk)
        new = []
        for g in groups:
            sc = _dot(ks_ref[0, g, pl.ds(k0, tk), :], q4a[g])
            if diag:
                sc = sc + tile(jnp.where(k0 + _iota((tk, 1), 0) <= t_q, 0.0, NEG))
            new.append(_chunk_update_t(carries[g], sc, vst_ref[0, g, :, pl.ds(k0, tk)]))
        return tuple(new)

    last = i // (tk // tq)
    carries = lax.fori_loop(0, last, functools.partial(slc_step, diag=False),
                            tuple(_chunk_init_t(rep * tq) for _ in groups))
    carries = slc_step(last, carries, True)

    heads = []
    for g in groups:
        o_slc = _chunk_finish_t(carries[g])
        gates = gt_ref[0, g]
        for r in range(rep):
            cols = slice(r * tq, (r + 1) * tq)
            heads.append(gates[3 * r:3 * r + 1] * o_cmp[g][:, cols]
                         + gates[3 * r + 1:3 * r + 2] * o_slc[:, cols]
                         + gates[3 * r + 2:3 * r + 3] * o_win[g][:, cols])
    o_ref[0] = _heads_to_rows(heads).astype(o_ref.dtype)


def _nsa(qt, cmp_kv, cmp_t, nks, nkw, nvt, gt, ovt, *, tq=256, tk=512):
    b, _, _, seq = qt.shape
    n_c = cmp_kv.shape[2]
    ng = NSA_GROUPS
    return pl.pallas_call(
        functools.partial(_nsa_kernel, tq=tq, tk=tk),
        grid=(b, seq // tq),
        in_specs=[pl.BlockSpec((1, NSA_HEADS, HEAD_DIM, tq), lambda bi, i: (bi, 0, 0, i)),
                  pl.BlockSpec((1, ng, n_c, HEAD_DIM), lambda bi, i: (bi, 0, 0, 0)),
                  pl.BlockSpec((1, ng, HEAD_DIM, n_c), lambda bi, i: (bi, 1, 0, 0)),
                  pl.BlockSpec((1, ng, seq, LANES), lambda bi, i: (bi, 0, 0, 0)),
                  pl.BlockSpec((1, ng, LANES, seq), lambda bi, i: (bi, 0, 0, 0)),
                  pl.BlockSpec((1, ng, seq, HEAD_DIM), lambda bi, i: (bi, 0, 0, 0)),
                  pl.BlockSpec((1, ng, LANES, seq), lambda bi, i: (bi, 1, 0, 0)),
                  pl.BlockSpec((1, ng, GATE_ROWS, tq), lambda bi, i: (bi, 0, 0, i)),
                  pl.BlockSpec((LANES, n_c), lambda bi, i: (0, 0))],
        out_specs=pl.BlockSpec((1, tq, NSA_HEADS * HEAD_DIM), lambda bi, i: (bi, i, 0)),
        out_shape=jax.ShapeDtypeStruct((b, seq, NSA_HEADS * HEAD_DIM), BF16),
        compiler_params=_params("parallel", "arbitrary"),
        name="nsa",
    )(qt, cmp_kv, cmp_t, nks, nvt, nkw, nvt, gt, ovt)


def _fox_kernel(qt_ref, k_ref, vt_ref, lf_ref, cum_ref, tri_ref, o_ref, fcol_ref, frow_ref, *, tq):
    i = pl.program_id(1)
    heads = range(FOX_HEADS)

    @pl.when(i == 0)
    def _():
        seq = lf_ref.shape[1]
        blk = cum_ref.shape[0]
        carry = jnp.zeros((1, LANES), F32)
        for c in range(seq // blk):
            rows = slice(c * blk, (c + 1) * blk)
            cs = jnp.dot(cum_ref[...], lf_ref[0, rows, :], precision=HIGHEST,
                         preferred_element_type=F32) + carry
            fcol_ref[rows, :] = cs
            carry = cs[blk - 1:blk, :]
        frow_ref[...] = fcol_ref[...].T[8:16, :]

    q0 = pl.multiple_of(i * tq, tq)
    qs = [qt_ref[0, h] for h in heads]
    fqs = [frow_ref[4 + h:5 + h, pl.ds(q0, tq)] for h in heads]

    def step(c, carries, diag):
        k0 = pl.multiple_of(c * tq, tq)
        new = []
        for h in heads:
            fk = fcol_ref[pl.ds(k0, tq), GATE_LANES + h:GATE_LANES + h + 1]
            sc = _dot(k_ref[0, h, pl.ds(k0, tq), :], qs[h]) + (fqs[h] - fk)
            if diag:
                sc = sc + tri_ref[...]
            new.append(_chunk_update_t(carries[h], sc, vt_ref[0, h, :, pl.ds(k0, tq)]))
        return tuple(new)

    carries = step(i, tuple(_chunk_init_t(tq) for _ in heads), True)
    carries = lax.fori_loop(0, i, functools.partial(step, diag=False), carries)
    o_ref[0] = _heads_to_rows([_chunk_finish_t(cr) for cr in carries]).astype(o_ref.dtype)


def _fox(fqt, fk, fvt, gm, cum, tri_t, *, tq=FOX_TILE):
    b, nh, _, seq = fqt.shape
    blk = cum.shape[0]
    return pl.pallas_call(
        functools.partial(_fox_kernel, tq=tq),
        grid=(b, seq // tq),
        in_specs=[pl.BlockSpec((1, nh, HEAD_DIM, tq), lambda bi, i: (bi, 0, 0, i)),
                  pl.BlockSpec((1, nh, seq, HEAD_DIM), lambda bi, i: (bi, 0, 0, 0)),
                  pl.BlockSpec((1, nh, LANES, seq), lambda bi, i: (bi, 0, 0, 0)),
                  pl.BlockSpec((1, seq, LANES), lambda bi, i: (bi, 0, 0)),
                  pl.BlockSpec((blk, blk), lambda bi, i: (0, 0)),
                  pl.BlockSpec((tq, tq), lambda bi, i: (0, 0))],
        out_specs=pl.BlockSpec((1, tq, nh * HEAD_DIM), lambda bi, i: (bi, i, 0)),
        out_shape=jax.ShapeDtypeStruct((b, seq, nh * HEAD_DIM), BF16),
        scratch_shapes=[pltpu.VMEM((seq, LANES), F32), pltpu.VMEM((8, seq), F32)],
        compiler_params=_params("parallel", "arbitrary"),
        name="fox",
    )(fqt, fk, fvt, gm, cum, tri_t)


def _moba_kernel(qt_ref, k_ref, vt_ref, avg_ref, tri_ref, o_ref, km_ref, *, tq):
    i = pl.program_id(1)
    nb = avg_ref.shape[0]

    @pl.when(i == 0)
    def _():
        for h in range(MOBA_HEADS):
            km_ref[h] = _dot(avg_ref[...], k_ref[0, h])

    blk = _iota((nb, tq), 0)
    own = i * (tq // MOBA_BLOCK) + lax.shift_right_logical(_iota((nb, tq), 1), MOBA_BLOCK.bit_length() - 1)
    qas = []
    for h in range(MOBA_HEADS):
        q = qt_ref[0, h]
        gate = jnp.dot(km_ref[h][:, :HEAD_DIM], q.astype(F32), precision=HIGHEST,
                       preferred_element_type=F32)
        val = jnp.where(blk < own, gate, -jnp.inf)
        rank = jnp.zeros((nb, tq), jnp.int32)
        for jp in range(nb):
            row = val[jp:jp + 1, :]
            ahead = (row > val) | ((row == val) & (blk > jp))
            rank = rank + ahead.astype(jnp.int32)
        keep = ((rank < MOBA_TOPK) & (blk < own)) | (blk == own)
        neg = jnp.concatenate([jnp.where(keep, 0.0, NEG), jnp.zeros((HEAD_DIM - nb, tq), F32)], axis=0)
        qas.append(jnp.concatenate([q, neg.astype(BF16)], axis=0))

    def step(c, carries, diag):
        k0 = pl.multiple_of(c * tq, tq)
        new = []
        for h in range(MOBA_HEADS):
            sc = _dot(k_ref[0, h, pl.ds(k0, tq), :], qas[h])
            if diag:
                sc = sc + tri_ref[...]
            new.append(_chunk_update_t(carries[h], sc, vt_ref[0, h, :, pl.ds(k0, tq)]))
        return tuple(new)

    carries = step(i, tuple(_chunk_init_t(tq) for _ in range(MOBA_HEADS)), True)
    carries = lax.fori_loop(0, i, functools.partial(step, diag=False), carries)
    o_ref[0] = _heads_to_rows([_chunk_finish_t(cr) for cr in carries]).astype(o_ref.dtype)


def _moba(mqt, mk, mvt, avg, tri_t, *, tq=MOBA_TILE):
    b, nh, _, seq = mqt.shape
    nb = avg.shape[0]
    return pl.pallas_call(
        functools.partial(_moba_kernel, tq=tq),
        grid=(b, seq // tq),
        in_specs=[pl.BlockSpec((1, nh, HEAD_DIM, tq), lambda bi, i: (bi, 0, 0, i)),
                  pl.BlockSpec((1, nh, seq, LANES), lambda bi, i: (bi, 0, 0, 0)),
                  pl.BlockSpec((1, nh, LANES, seq), lambda bi, i: (bi, 0, 0, 0)),
                  pl.BlockSpec((nb, seq), lambda bi, i: (0, 0)),
                  pl.BlockSpec((tq, tq), lambda bi, i: (0, 0))],
        out_specs=pl.BlockSpec((1, tq, nh * HEAD_DIM), lambda bi, i: (bi, i, 0)),
        out_shape=jax.ShapeDtypeStruct((b, seq, nh * HEAD_DIM), BF16),
        scratch_shapes=[pltpu.VMEM((nh, nb, LANES), F32)],
        compiler_params=_params("parallel", "arbitrary"),
        name="moba",
    )(mqt, mk, mvt, avg, tri_t)


def _pack_w_in(w_in):
    depth, d, _ = w_in.shape

    def cols(a, b):
        return w_in[:, :, a:b]

    zeros = lambda n: jnp.zeros((depth, d, n), w_in.dtype)
    g_half = 3 * NSA_REP
    token_major = [cols(_OFF_KC, _OFF_KS),
                   cols(_OFF_KS, _OFF_VS), cols(_OFF_KW, _OFF_VW),
                   cols(_OFF_FOX_K, _OFF_FOX_V),
                   cols(_OFF_MOBA_K, _OFF_MOBA_V),
                   zeros(GATE_LANES), cols(_OFF_FOX_F, _OFF_FOX_F + FOX_HEADS),
                   zeros(LANES - GATE_LANES - FOX_HEADS)]
    transposed = [cols(_OFF_NSA_Q, _OFF_KC),
                  cols(_OFF_VS, _OFF_KW), cols(_OFF_VW, _OFF_GATE),
                  cols(_OFF_GATE, _OFF_GATE + g_half), zeros(GATE_ROWS - g_half),
                  cols(_OFF_GATE + g_half, _OFF_GATE + 2 * g_half), zeros(LANES - GATE_ROWS - g_half),
                  cols(_OFF_FOX_Q, _OFF_FOX_K), cols(_OFF_FOX_V, _OFF_FOX_F),
                  cols(_OFF_MOBA_Q, _OFF_MOBA_K), cols(_OFF_MOBA_V, IN_COLS)]
    w_p = jnp.concatenate(token_major, axis=-1).astype(BF16)
    w_t = jnp.swapaxes(jnp.concatenate(transposed, axis=-1), 1, 2).astype(BF16)
    return w_p, w_t


def _constants(seq):
    n_c = seq // CMP_STRIDE
    n_s = seq // SLC_LEN
    c0 = np.arange(n_c)[None, :] * CMP_STRIDE
    j0 = np.arange(LANES)[:, None] * SLC_LEN
    real = (np.arange(n_c)[None, :] < (seq - CMP_LEN) // CMP_STRIDE + 1) & (np.arange(LANES)[:, None] < n_s)
    ov = ((c0 < j0 + SLC_LEN) & (c0 + CMP_LEN > j0) & real).astype(np.float32)
    nb = seq // MOBA_BLOCK
    avg = (np.arange(nb)[:, None] == (np.arange(seq)[None, :] // MOBA_BLOCK)).astype(np.float32) / MOBA_BLOCK
    tri = np.tril(np.ones((256, 256), np.float32))

    kq = np.arange(FOX_TILE)
    causal_fox = np.where(kq[:, None] <= kq[None, :], 0.0, NEG).astype(np.float32)
    r = np.arange(MOBA_TILE)
    same = (r[:, None] // MOBA_BLOCK) == (r[None, :] // MOBA_BLOCK)
    allowed = np.where(same, r[:, None] <= r[None, :], r[:, None] < r[None, :])
    causal_moba = np.where(allowed, 0.0, NEG).astype(np.float32)
    return dict(ov=jnp.asarray(ov), avg=jnp.asarray(avg, dtype=BF16), tri=jnp.asarray(tri),
                causal_fox=jnp.asarray(causal_fox), causal_moba=jnp.asarray(causal_moba))


def kernel(x, c, positions, norm_g, w_ada, b_ada, w_in, fox_fbias, cmp_pos, cmp_w1, cmp_w2, w_out,
           ffn_w13, ffn_w2, final_g):
    batch, seq, d = x.shape
    depth = w_ada.shape[0]
    assert d == D_MODEL and seq % MOBA_BLOCK == 0 and seq % 1024 == 0

    w_in_p, w_in_t = _pack_w_in(w_in)
    w13 = ffn_w13.astype(BF16)
    w2 = ffn_w2.astype(BF16)
    w_o = w_out.astype(BF16)
    half = CMP_LEN * HEAD_DIM // 2
    cw1 = cmp_w1.astype(BF16).reshape(depth, 2, 2, half, CMP_HIDDEN)
    cw2 = cmp_w2.astype(BF16)
    cw2t = jnp.swapaxes(cw2, 2, 3)
    cpos = cmp_pos.reshape(depth, 2, 2, 1, half)
    cst = _constants(seq)

    inv = ROPE_THETA ** (-jnp.arange(0, ROPE_DIM, 2, dtype=F32) / ROPE_DIM)
    lane = np.arange(LANES) % HEAD_DIM
    inv_lane = jnp.where(jnp.asarray(lane < ROPE_DIM), inv[jnp.asarray(lane % (ROPE_DIM // 2))], 0.0)
    inv_lane = inv_lane.reshape(1, LANES).astype(F32)
    inv_col = inv.reshape(ROPE_DIM // 2, 1)
    fb_lane = jnp.zeros((depth, 1, LANES), F32).at[:, 0, GATE_LANES:GATE_LANES + FOX_HEADS].set(fox_fbias)

    mod_all = _ada_mod(c, w_ada, b_ada).reshape(depth, batch, 9, d)
    x2 = x.reshape(batch * seq, d)
    fg = final_g.reshape(1, d)
    for l in range(depth):
        mod = mod_all[l]
        x2 = _ffn(x2, mod, norm_g[l, 0:1], fg, w13[l, 0], w2[l, 0], sub=0, seq=seq, final=False)
        qt, nvt, gt, fqt, fvt, mqt, mvt, cmp_in, nks, nkw, fk, mk, gm = _inproj(
            x2, mod, norm_g[l, 1:2], positions, inv_lane, inv_col, fb_lane[l], w_in_p[l], w_in_t[l],
            batch=batch, seq=seq)
        cmp_kv, cmp_t = _compress(cmp_in.reshape(batch, 4, seq // CMP_STRIDE, CMP_STRIDE * HEAD_DIM),
                                  cpos[l], cw1[l], cw2[l], cw2t[l])
        o_nsa = _nsa(qt, cmp_kv, cmp_t, nks, nkw, nvt, gt, cst["ov"])
        o_fox = _fox(fqt, fk, fvt, gm, cst["tri"], cst["causal_fox"])
        o_moba = _moba(mqt, mk, mvt, cst["avg"], cst["causal_moba"])
        x2 = _mix_ffn(x2, mod, norm_g[l, 2:3], fg, o_nsa.reshape(batch * seq, -1),
                      o_fox.reshape(batch * seq, -1), o_moba.reshape(batch * seq, -1), w_o[l],
                      w13[l, 1], w2[l, 1], seq=seq, final=(l == depth - 1))
    return x2.reshape(batch, seq, d)
```

```python
import functools

import numpy as np
import jax
import jax.numpy as jnp
from jax import lax
from jax.experimental import pallas as pl
from jax.experimental.pallas import tpu as pltpu

D_MODEL = 1024
HEAD_DIM = 64
NSA_HEADS = 8
NSA_GROUPS = 2
NSA_REP = NSA_HEADS // NSA_GROUPS
FOX_HEADS = 4
MOBA_HEADS = 4
ROPE_DIM = HEAD_DIM // 4
ROPE_THETA = 500000.0
CMP_LEN = 32
CMP_STRIDE = 16
CMP_HIDDEN = 4 * HEAD_DIM
SLC_LEN = 64
SLC_TOPN = 8
WIN = 512
MOBA_BLOCK = 256
MOBA_TOPK = 3
D_FF = 2816
EPS = 1e-6
ATT_SCALE = HEAD_DIM ** -0.5

FOX_TILE = 512
MOBA_TILE = 2 * MOBA_BLOCK
LANES = 128
MXU_TILE = 256
NEG = -1e30
VMEM_LIMIT = 56 * 1024 * 1024

F32 = jnp.float32
BF16 = jnp.bfloat16
HIGHEST = lax.Precision.HIGHEST

_KV = NSA_GROUPS * HEAD_DIM
_OFF_NSA_Q = 0
_OFF_KC = _OFF_NSA_Q + NSA_HEADS * HEAD_DIM
_OFF_VC = _OFF_KC + _KV
_OFF_KS = _OFF_VC + _KV
_OFF_VS = _OFF_KS + _KV
_OFF_KW = _OFF_VS + _KV
_OFF_VW = _OFF_KW + _KV
_OFF_GATE = _OFF_VW + _KV
_OFF_FOX_Q = _OFF_GATE + 3 * NSA_HEADS
_OFF_FOX_K = _OFF_FOX_Q + FOX_HEADS * HEAD_DIM
_OFF_FOX_V = _OFF_FOX_K + FOX_HEADS * HEAD_DIM
_OFF_FOX_F = _OFF_FOX_V + FOX_HEADS * HEAD_DIM
_OFF_MOBA_Q = _OFF_FOX_F + FOX_HEADS
_OFF_MOBA_K = _OFF_MOBA_Q + MOBA_HEADS * HEAD_DIM
_OFF_MOBA_V = _OFF_MOBA_K + MOBA_HEADS * HEAD_DIM
IN_COLS = _OFF_MOBA_V + MOBA_HEADS * HEAD_DIM

N_HEAD_COLS = 1024
GATE_LANES = 3 * NSA_REP
PROJ_COLS = N_HEAD_COLS + LANES
GATE_ROWS = 16
_T_NSA_Q = 0
_T_NSA_V = _T_NSA_Q + NSA_HEADS * HEAD_DIM
_T_GATE = _T_NSA_V + 4 * HEAD_DIM
_T_FOX_Q = _T_GATE + LANES
_T_FOX_V = _T_FOX_Q + FOX_HEADS * HEAD_DIM
_T_MOBA_Q = _T_FOX_V + FOX_HEADS * HEAD_DIM
_T_MOBA_V = _T_MOBA_Q + MOBA_HEADS * HEAD_DIM
T_ROWS = _T_MOBA_V + MOBA_HEADS * HEAD_DIM


def _dot(a, b):
    return jnp.dot(a, b, preferred_element_type=F32)


def _dot_nt(a, b, precision=None):
    return lax.dot_general(a, b, (((1,), (1,)), ((), ())), precision=precision,
                           preferred_element_type=F32)


def _iota(shape, dim):
    return lax.broadcasted_iota(jnp.int32, shape, dim)


def _params(*sem, flags=None):
    return pltpu.CompilerParams(dimension_semantics=sem, vmem_limit_bytes=VMEM_LIMIT, flags=flags)


def _ada_norm(x, g, shift, scale):
    ms = jnp.mean(x * x, axis=-1, keepdims=True)
    y = x * lax.rsqrt(ms + EPS) * g
    return y * (1.0 + scale) + shift


def _ada_kernel(c_ref, w_ref, b_ref, o_ref):
    c = c_ref[...]
    ca = c * jax.nn.sigmoid(c)
    o_ref[0] = jnp.dot(ca, w_ref[0], precision=HIGHEST, preferred_element_type=F32) + b_ref[0]


def _ada_mod(c, w_ada, b_ada):
    depth, d, n = w_ada.shape
    b = c.shape[0]
    tn = 2304
    return pl.pallas_call(
        _ada_kernel,
        grid=(depth, n // tn),
        in_specs=[pl.BlockSpec((b, d), lambda l, j: (0, 0)),
                  pl.BlockSpec((1, d, tn), lambda l, j: (l, 0, j)),
                  pl.BlockSpec((1, 1, tn), lambda l, j: (l, 0, j))],
        out_specs=pl.BlockSpec((1, b, tn), lambda l, j: (l, 0, j)),
        out_shape=jax.ShapeDtypeStruct((depth, b, n), F32),
        compiler_params=_params("parallel", "parallel"),
        name="ada_mod",
    )(c, w_ada, b_ada.reshape(depth, 1, n))


def _swiglu_residual(x, mod_ref, g_ref, fg_ref, w13_ref, w2_ref, *, sub, coef, final, splits):
    h = _ada_norm(x, g_ref[...], mod_ref[0, 3 * sub:3 * sub + 1, :],
                  mod_ref[0, 3 * sub + 1:3 * sub + 2, :]).astype(BF16)
    acc = None
    for lo, hi in splits:
        a = _dot(h, w13_ref[:, lo:hi])
        b = _dot(h, w13_ref[:, D_FF + lo:D_FF + hi])
        u = (a * jax.nn.sigmoid(a) * b).astype(BF16)
        part = _dot(u, w2_ref[lo:hi, :])
        acc = part if acc is None else acc + part
    y = x + (coef * mod_ref[0, 3 * sub + 2:3 * sub + 3, :]) * acc
    if final:
        y = y * lax.rsqrt(jnp.mean(y * y, axis=-1, keepdims=True) + EPS) * fg_ref[...]
    return y


def _ffn_kernel(x_ref, mod_ref, g_ref, fg_ref, w13_ref, w2_ref, o_ref, **kw):
    o_ref[...] = _swiglu_residual(x_ref[...], mod_ref, g_ref, fg_ref, w13_ref, w2_ref, **kw)


def _mix_ffn_kernel(x_ref, mod_ref, g_ref, fg_ref, on_ref, of_ref, om_ref, wo_ref, w13_ref, w2_ref, o_ref, **kw):
    n_w = NSA_HEADS * HEAD_DIM
    f_w = FOX_HEADS * HEAD_DIM
    mix = _dot(on_ref[...], wo_ref[0:n_w, :])
    mix += _dot(of_ref[...], wo_ref[n_w:n_w + f_w, :])
    mix += _dot(om_ref[...], wo_ref[n_w + f_w:, :])
    x = x_ref[...] + mod_ref[0, 5:6, :] * mix
    o_ref[...] = _swiglu_residual(x, mod_ref, g_ref, fg_ref, w13_ref, w2_ref, **kw)


def _resident(shape):
    return pl.BlockSpec(shape, lambda i: (0,) * len(shape), pipeline_mode=pl.Buffered(1))


def _hidden_splits(chunk):
    return tuple((lo, min(lo + chunk, D_FF)) for lo in range(0, D_FF, chunk))


def _ffn(x2, mod, g, final_g, w13, w2, *, sub, seq, final, tm=1024, chunk=3 * MXU_TILE):
    n, d = x2.shape
    per_b = seq // tm
    return pl.pallas_call(
        functools.partial(_ffn_kernel, sub=sub, coef=0.5, final=final, splits=_hidden_splits(chunk)),
        grid=(n // tm,),
        in_specs=[pl.BlockSpec((tm, d), lambda i: (i, 0)),
                  pl.BlockSpec((1, 9, d), lambda i: (i // per_b, 0, 0)),
                  pl.BlockSpec((1, d), lambda i: (0, 0)),
                  pl.BlockSpec((1, d), lambda i: (0, 0)),
                  _resident((d, 2 * D_FF)), _resident((D_FF, d))],
        out_specs=pl.BlockSpec((tm, d), lambda i: (i, 0)),
        out_shape=jax.ShapeDtypeStruct((n, d), F32),
        compiler_params=_params("parallel"),
        name="ffn",
    )(x2, mod, g, final_g, w13, w2)


def _mix_ffn(x2, mod, g, final_g, o_nsa, o_fox, o_moba, w_out, w13, w2, *, seq, final, tm=512,
             chunk=4 * MXU_TILE):
    n, d = x2.shape
    per_b = seq // tm
    row = lambda width: pl.BlockSpec((tm, width), lambda i: (i, 0))
    return pl.pallas_call(
        functools.partial(_mix_ffn_kernel, sub=2, coef=0.5, final=final, splits=_hidden_splits(chunk)),
        grid=(n // tm,),
        in_specs=[row(d),
                  pl.BlockSpec((1, 9, d), lambda i: (i // per_b, 0, 0)),
                  pl.BlockSpec((1, d), lambda i: (0, 0)),
                  pl.BlockSpec((1, d), lambda i: (0, 0)),
                  row(o_nsa.shape[1]), row(o_fox.shape[1]), row(o_moba.shape[1]),
                  _resident((d, d)), _resident((d, 2 * D_FF)), _resident((D_FF, d))],
        out_specs=row(d),
        out_shape=jax.ShapeDtypeStruct((n, d), F32),
        compiler_params=_params("parallel"),
        name="mix_ffn",
    )(x2, mod, g, final_g, o_nsa, o_fox, o_moba, w_out, w13, w2)


def _inproj_kernel(x_ref, mod_ref, g_ref, posr_ref, invc_ref, fb_ref, w_ref, wt_ref,
                   qt_ref, nvt_ref, gt_ref, fqt_ref, fvt_ref, mqt_ref, mvt_ref,
                   cmp_ref, nks_ref, nkw_ref, fk_ref, mk_ref, gm_ref, *, per_b):
    h = _ada_norm(x_ref[...], g_ref[...], mod_ref[0, 3:4, :], mod_ref[0, 4:5, :]).astype(BF16)
    tm = h.shape[0]
    seq_pos = (pl.program_id(0) % per_b) * tm + _iota((tm, LANES), 0)

    half = ROPE_DIM // 2
    ang_t = invc_ref[...] * posr_ref[0].astype(F32)
    cos_t = jnp.cos(ang_t)
    sin_t = jnp.sin(ang_t)
    ones_row = jnp.where(_iota((HEAD_DIM, tm), 0) == 0, 1.0, 0.0)
    slab_rows = 4 * HEAD_DIM

    def queries_t(row0, n_heads, out_ref, rotate):
        for s_idx in range(n_heads // 4):
            y_t = _dot_nt(wt_ref[row0 + slab_rows * s_idx:row0 + slab_rows * (s_idx + 1), :], h)
            for hh in range(4):
                blk = y_t[HEAD_DIM * hh:HEAD_DIM * (hh + 1)]
                if rotate:
                    r1, r2 = blk[:half], blk[half:2 * half]
                    blk = jnp.concatenate([r1 * cos_t - r2 * sin_t, r2 * cos_t + r1 * sin_t, blk[2 * half:]],
                                          axis=0)
                out_ref[0, 4 * s_idx + hh] = (blk * ATT_SCALE).astype(BF16)

    def values_t(row0, out_ref):
        y_t = _dot_nt(wt_ref[row0:row0 + slab_rows, :], h)
        for idx in range(4):
            out_ref[0, idx] = jnp.concatenate([y_t[HEAD_DIM * idx:HEAD_DIM * (idx + 1)], ones_row],
                                              axis=0).astype(BF16)

    queries_t(_T_NSA_Q, NSA_HEADS, qt_ref, True)
    values_t(_T_NSA_V, nvt_ref)
    sig_t = jax.nn.sigmoid(_dot_nt(wt_ref[_T_GATE:_T_GATE + LANES, :], h))
    for g in range(NSA_GROUPS):
        gt_ref[0, g] = sig_t[GATE_ROWS * g:GATE_ROWS * (g + 1)]
    queries_t(_T_FOX_Q, FOX_HEADS, fqt_ref, False)
    values_t(_T_FOX_V, fvt_ref)
    queries_t(_T_MOBA_Q, MOBA_HEADS, mqt_ref, True)
    values_t(_T_MOBA_V, mvt_ref)

    zeros_t = jnp.zeros((half, tm), F32)

    def lane_table(first, second, fill):
        head = jnp.concatenate([first, second, jnp.full((HEAD_DIM - ROPE_DIM, tm), fill, F32)], axis=0)
        return jnp.concatenate([head, head], axis=0).T

    cosl = lane_table(cos_t, cos_t, 1.0)
    sin_lo = lane_table(-sin_t, zeros_t, 0.0)
    sin_hi = lane_table(zeros_t, sin_t, 0.0)
    lane = _iota((tm, LANES), 1)

    def rope(y):
        return (y * cosl + pltpu.roll(y, LANES - ROPE_DIM // 2, 1) * sin_lo
                + pltpu.roll(y, ROPE_DIM // 2, 1) * sin_hi)

    def put_heads(z, out_ref, head0, dtype):
        z = z.astype(dtype)
        out_ref[0, head0] = z[:, :HEAD_DIM]
        out_ref[0, head0 + 1] = z[:, HEAD_DIM:]

    def put_wide(z, out_ref, head0, tail):
        out_ref[0, head0] = jnp.where(lane < HEAD_DIM, z, tail).astype(BF16)
        out_ref[0, head0 + 1] = jnp.where(lane < HEAD_DIM, pltpu.roll(z, HEAD_DIM, 1), tail).astype(BF16)

    def put_keys(z, out_ref, head0, block_len):
        block = lax.shift_right_logical(seq_pos, block_len.bit_length() - 1)
        put_wide(z, out_ref, head0, jnp.where(lane - HEAD_DIM == block, 1.0, 0.0))

    def slab(c0):
        y = _dot(h, w_ref[:, c0:c0 + 2 * LANES])
        return y[:, :LANES], y[:, LANES:]

    a, b = slab(0)
    put_heads(rope(a), cmp_ref, 0, F32)
    put_heads(b, cmp_ref, 2, F32)
    a, b = slab(256)
    put_keys(rope(a), nks_ref, 0, SLC_LEN)
    put_heads(rope(b), nkw_ref, 0, BF16)
    a, b = slab(512)
    put_heads(a, fk_ref, 0, BF16)
    put_heads(b, fk_ref, 2, BF16)
    a, b = slab(768)
    put_keys(rope(a), mk_ref, 0, MOBA_BLOCK)
    put_keys(rope(b), mk_ref, 2, MOBA_BLOCK)
    f = _dot(h, w_ref[:, N_HEAD_COLS:])
    gm_ref[0] = jax.nn.log_sigmoid(f + fb_ref[...])


def _inproj(x2, mod, g, positions, inv_col, fb_lane, w_p, w_t, *, batch, seq, tm=512):
    n, d = x2.shape
    per_b = seq // tm

    def hm(nh, dtype, width=HEAD_DIM):
        return (pl.BlockSpec((1, nh, tm, width), lambda i: (i // per_b, 0, i % per_b, 0)),
                jax.ShapeDtypeStruct((batch, nh, seq, width), dtype))

    def tr(nh, rows, dtype):
        return (pl.BlockSpec((1, nh, rows, tm), lambda i: (i // per_b, 0, 0, i % per_b)),
                jax.ShapeDtypeStruct((batch, nh, rows, seq), dtype))

    specs = [tr(NSA_HEADS, HEAD_DIM, BF16), tr(4, LANES, BF16), tr(NSA_GROUPS, GATE_ROWS, F32),
             tr(FOX_HEADS, HEAD_DIM, BF16), tr(FOX_HEADS, LANES, BF16),
             tr(MOBA_HEADS, HEAD_DIM, BF16), tr(MOBA_HEADS, LANES, BF16),
             hm(4, F32), hm(2, BF16, LANES), hm(2, BF16), hm(FOX_HEADS, BF16), hm(MOBA_HEADS, BF16, LANES),
             (pl.BlockSpec((1, tm, LANES), lambda i: (i // per_b, i % per_b, 0)),
              jax.ShapeDtypeStruct((batch, seq, LANES), F32))]
    const = lambda shape: pl.BlockSpec(shape, lambda i: (0,) * len(shape))
    return pl.pallas_call(
        functools.partial(_inproj_kernel, per_b=per_b),
        grid=(n // tm,),
        in_specs=[pl.BlockSpec((tm, d), lambda i: (i, 0)),
                  pl.BlockSpec((1, 9, d), lambda i: (i // per_b, 0, 0)),
                  const((1, d)),
                  pl.BlockSpec((1, 1, tm), lambda i: (i, 0, 0)),
                  const((ROPE_DIM // 2, 1)), const((1, LANES)),
                  _resident(w_p.shape), _resident(w_t.shape)],
        out_specs=[s for s, _ in specs],
        out_shape=[s for _, s in specs],
        compiler_params=_params("parallel"),
        name="inproj",
    )(x2, mod, g, positions.reshape(n // tm, 1, tm), inv_col, fb_lane, w_p, w_t)


def _compress_kernel(z_ref, pos_ref, w1_ref, w2_ref, w2t_ref, o_ref, ot_ref):
    z = z_ref[0, 0]
    top = _dot((z + pos_ref[0, 0]).astype(BF16), w1_ref[0, 0])
    bot = _dot((z + pos_ref[0, 1]).astype(BF16), w1_ref[0, 1])
    nch = z.shape[0]
    hid = top + pltpu.roll(bot, nch - 1, 0)
    act = jax.nn.gelu(hid).astype(BF16)
    o_ref[0, 0] = _dot(act, w2_ref[0]).astype(o_ref.dtype)
    ot_ref[0, 0] = _dot_nt(w2t_ref[0], act).astype(ot_ref.dtype)


def _compress(cmp_in, pos_l, w1_l, w2_l, w2t_l):
    b, four, nch, width = cmp_in.shape
    return pl.pallas_call(
        _compress_kernel,
        grid=(b, four),
        in_specs=[pl.BlockSpec((1, 1, nch, width), lambda i, n: (i, n, 0, 0)),
                  pl.BlockSpec((1, 2, 1, width), lambda i, n: (n // 2, 0, 0, 0)),
                  pl.BlockSpec((1, 2, width, CMP_HIDDEN), lambda i, n: (n // 2, 0, 0, 0)),
                  pl.BlockSpec((1, CMP_HIDDEN, HEAD_DIM), lambda i, n: (n // 2, 0, 0)),
                  pl.BlockSpec((1, HEAD_DIM, CMP_HIDDEN), lambda i, n: (n // 2, 0, 0))],
        out_specs=[pl.BlockSpec((1, 1, nch, HEAD_DIM), lambda i, n: (i, n, 0, 0)),
                   pl.BlockSpec((1, 1, HEAD_DIM, nch), lambda i, n: (i, n, 0, 0))],
        out_shape=[jax.ShapeDtypeStruct((b, four, nch, HEAD_DIM), BF16),
                   jax.ShapeDtypeStruct((b, four, HEAD_DIM, nch), BF16)],
        compiler_params=_params("parallel", "parallel"),
        name="nsa_compress",
    )(cmp_in, pos_l, w1_l, w2_l, w2t_l)


def _chunk_update_t(carry, s, v_t):
    m, acc = carry
    m_new = jnp.maximum(m, jnp.max(s, axis=0, keepdims=True))
    p = jnp.exp(s - m_new).astype(BF16)
    return m_new, jnp.exp(m - m_new) * acc + _dot(v_t, p)


def _chunk_init_t(n):
    return jnp.full((1, n), NEG, F32), jnp.zeros((LANES, n), F32)


def _chunk_finish_t(carry):
    _, acc = carry
    return acc[:HEAD_DIM] / jnp.maximum(acc[HEAD_DIM:HEAD_DIM + 1], 1e-30)


def _heads_to_rows(heads_t):
    pairs = [jnp.concatenate(heads_t[a:a + 2], axis=0).T for a in range(0, len(heads_t), 2)]
    return jnp.concatenate(pairs, axis=1)


def _nsa_kernel(qt_ref, kc_ref, vct_ref, ks_ref, vst_ref, kw_ref, vwt_ref, gt_ref, ovt_ref, o_ref, *, tq, tk):
    i = pl.program_id(1)
    rep = NSA_REP
    groups = range(NSA_GROUPS)
    n_c = kc_ref.shape[2]
    n_s = ks_ref.shape[2] // SLC_LEN
    tile = lambda a: jnp.concatenate([a] * rep, axis=1)
    t_q = i * tq + _iota((1, tq), 1)
    cend = _iota((n_c, 1), 0) * CMP_STRIDE + (CMP_LEN - 1)
    cm = cend <= tile(t_q)
    j_t = _iota((n_s, tq), 0)
    tb = lax.shift_right_logical(t_q, SLC_LEN.bit_length() - 1)
    valid = j_t <= tb
    forced = (j_t == 0) | (j_t == tb) | (j_t == tb - 1)
    span = WIN + tq
    start = pl.multiple_of(jnp.maximum(i * tq - WIN, 0), tq)
    dist = t_q - (start + _iota((span, 1), 0))
    band = tile(jnp.where((dist >= 0) & (dist < WIN), 0.0, NEG))

    o_cmp, o_win, q4a = [], [], []
    for g in groups:
        q4 = jnp.concatenate([qt_ref[0, rep * g + r] for r in range(rep)], axis=1)
        s = jnp.where(cm, _dot(kc_ref[0, g], q4), NEG)
        e = jnp.where(cm, jnp.exp(s - jnp.max(s, axis=0, keepdims=True)), 0.0)
        p = e / jnp.maximum(jnp.sum(e, axis=0, keepdims=True), 1e-30)
        o_cmp.append(_dot(vct_ref[0, g], p.astype(BF16)))
        psum = p[:, 0:tq] + p[:, tq:2 * tq] + p[:, 2 * tq:3 * tq] + p[:, 3 * tq:4 * tq]
        imp = jnp.dot(ovt_ref[...], psum, precision=HIGHEST, preferred_element_type=F32)[:n_s]
        v_t = jnp.where(valid, jnp.where(forced, jnp.inf, imp), -jnp.inf)
        rank = jnp.zeros((n_s, tq), jnp.int32)
        for jp in range(n_s):
            row = v_t[jp:jp + 1, :]
            ahead = (row > v_t) | ((row == v_t) & (j_t > jp))
            rank = rank + ahead.astype(jnp.int32)
        neg_t = jnp.where((rank < SLC_TOPN) & (v_t > -jnp.inf), 0.0, NEG)
        neg = jnp.concatenate([neg_t, jnp.zeros((HEAD_DIM - n_s, tq), F32)], axis=0)
        q4a.append(jnp.concatenate([q4, tile(neg).astype(BF16)], axis=0))
        sw = _dot(kw_ref[0, g, pl.ds(start, span), :], q4) + band
        pw = jnp.exp(sw - jnp.max(sw, axis=0, keepdims=True)).astype(BF16)
        aw = _dot(vwt_ref[0, g, :, pl.ds(start, span)], pw)
        o_win.append(aw[:HEAD_DIM] / jnp.maximum(aw[HEAD_DIM:HEAD_DIM + 1], 1e-30))

    def slc_step(c, carries, diag):
        k0 = pl.multiple_of(c * tk, tk)
        new = []
        for g in groups:
            sc = _dot(ks_ref[0, g, pl.ds(k0, tk), :], q4a[g])
            if diag:
                sc = sc + tile(jnp.where(k0 + _iota((tk, 1), 0) <= t_q, 0.0, NEG))
            new.append(_chunk_update_t(carries[g], sc, vst_ref[0, g, :, pl.ds(k0, tk)]))
        return tuple(new)

    last = i // (tk // tq)
    carries = lax.fori_loop(0, last, functools.partial(slc_step, diag=False),
                            tuple(_chunk_init_t(rep * tq) for _ in groups))
    carries = slc_step(last, carries, True)

    heads = []
    for g in groups:
        o_slc = _chunk_finish_t(carries[g])
        gates = gt_ref[0, g]
        for r in range(rep):
            cols = slice(r * tq, (r + 1) * tq)
            heads.append(gates[3 * r:3 * r + 1] * o_cmp[g][:, cols]
                         + gates[3 * r + 1:3 * r + 2] * o_slc[:, cols]
                         + gates[3 * r + 2:3 * r + 3] * o_win[g][:, cols])
    o_ref[0] = _heads_to_rows(heads).astype(o_ref.dtype)


def _nsa(qt, cmp_kv, cmp_t, nks, nkw, nvt, gt, ovt, *, tq=256, tk=512):
    b, _, _, seq = qt.shape
    n_c = cmp_kv.shape[2]
    ng = NSA_GROUPS
    return pl.pallas_call(
        functools.partial(_nsa_kernel, tq=tq, tk=tk),
        grid=(b, seq // tq),
        in_specs=[pl.BlockSpec((1, NSA_HEADS, HEAD_DIM, tq), lambda bi, i: (bi, 0, 0, i)),
                  pl.BlockSpec((1, ng, n_c, HEAD_DIM), lambda bi, i: (bi, 0, 0, 0)),
                  pl.BlockSpec((1, ng, HEAD_DIM, n_c), lambda bi, i: (bi, 1, 0, 0)),
                  pl.BlockSpec((1, ng, seq, LANES), lambda bi, i: (bi, 0, 0, 0)),
                  pl.BlockSpec((1, ng, LANES, seq), lambda bi, i: (bi, 0, 0, 0)),
                  pl.BlockSpec((1, ng, seq, HEAD_DIM), lambda bi, i: (bi, 0, 0, 0)),
                  pl.BlockSpec((1, ng, LANES, seq), lambda bi, i: (bi, 1, 0, 0)),
                  pl.BlockSpec((1, ng, GATE_ROWS, tq), lambda bi, i: (bi, 0, 0, i)),
                  pl.BlockSpec((LANES, n_c), lambda bi, i: (0, 0))],
        out_specs=pl.BlockSpec((1, tq, NSA_HEADS * HEAD_DIM), lambda bi, i: (bi, i, 0)),
        out_shape=jax.ShapeDtypeStruct((b, seq, NSA_HEADS * HEAD_DIM), BF16),
        compiler_params=_params("parallel", "arbitrary"),
        name="nsa",
    )(qt, cmp_kv, cmp_t, nks, nvt, nkw, nvt, gt, ovt)


def _fox_kernel(qt_ref, k_ref, vt_ref, lf_ref, cum_ref, tri_ref, o_ref, fcol_ref, frow_ref, *, tq):
    i = pl.program_id(1)
    heads = range(FOX_HEADS)

    @pl.when(i == 0)
    def _():
        seq = lf_ref.shape[1]
        blk = cum_ref.shape[0]
        carry = jnp.zeros((1, LANES), F32)
        for c in range(seq // blk):
            rows = slice(c * blk, (c + 1) * blk)
            cs = jnp.dot(cum_ref[...], lf_ref[0, rows, :], precision=HIGHEST,
                         preferred_element_type=F32) + carry
            fcol_ref[rows, :] = cs
            carry = cs[blk - 1:blk, :]
        frow_ref[...] = fcol_ref[...].T[8:16, :]

    q0 = pl.multiple_of(i * tq, tq)
    qs = [qt_ref[0, h] for h in heads]
    fqs = [frow_ref[4 + h:5 + h, pl.ds(q0, tq)] for h in heads]

    def step(c, carries, diag):
        k0 = pl.multiple_of(c * tq, tq)
        new = []
        for h in heads:
            fk = fcol_ref[pl.ds(k0, tq), GATE_LANES + h:GATE_LANES + h + 1]
            sc = _dot(k_ref[0, h, pl.ds(k0, tq), :], qs[h]) + (fqs[h] - fk)
            if diag:
                sc = sc + tri_ref[...]
            new.append(_chunk_update_t(carries[h], sc, vt_ref[0, h, :, pl.ds(k0, tq)]))
        return tuple(new)

    carries = step(i, tuple(_chunk_init_t(tq) for _ in heads), True)
    carries = lax.fori_loop(0, i, functools.partial(step, diag=False), carries)
    o_ref[0] = _heads_to_rows([_chunk_finish_t(cr) for cr in carries]).astype(o_ref.dtype)


def _fox(fqt, fk, fvt, gm, cum, tri_t, *, tq=FOX_TILE):
    b, nh, _, seq = fqt.shape
    blk = cum.shape[0]
    return pl.pallas_call(
        functools.partial(_fox_kernel, tq=tq),
        grid=(b, seq // tq),
        in_specs=[pl.BlockSpec((1, nh, HEAD_DIM, tq), lambda bi, i: (bi, 0, 0, i)),
                  pl.BlockSpec((1, nh, seq, HEAD_DIM), lambda bi, i: (bi, 0, 0, 0)),
                  pl.BlockSpec((1, nh, LANES, seq), lambda bi, i: (bi, 0, 0, 0)),
                  pl.BlockSpec((1, seq, LANES), lambda bi, i: (bi, 0, 0)),
                  pl.BlockSpec((blk, blk), lambda bi, i: (0, 0)),
                  pl.BlockSpec((tq, tq), lambda bi, i: (0, 0))],
        out_specs=pl.BlockSpec((1, tq, nh * HEAD_DIM), lambda bi, i: (bi, i, 0)),
        out_shape=jax.ShapeDtypeStruct((b, seq, nh * HEAD_DIM), BF16),
        scratch_shapes=[pltpu.VMEM((seq, LANES), F32), pltpu.VMEM((8, seq), F32)],
        compiler_params=_params("parallel", "arbitrary"),
        name="fox",
    )(fqt, fk, fvt, gm, cum, tri_t)


def _moba_kernel(qt_ref, k_ref, vt_ref, avg_ref, tri_ref, o_ref, km_ref, *, tq):
    i = pl.program_id(1)
    nb = avg_ref.shape[0]

    @pl.when(i == 0)
    def _():
        for h in range(MOBA_HEADS):
            km_ref[h] = _dot(avg_ref[...], k_ref[0, h])

    blk = _iota((nb, tq), 0)
    own = i * (tq // MOBA_BLOCK) + lax.shift_right_logical(_iota((nb, tq), 1), MOBA_BLOCK.bit_length() - 1)
    qas = []
    for h in range(MOBA_HEADS):
        q = qt_ref[0, h]
        gate = jnp.dot(km_ref[h][:, :HEAD_DIM], q.astype(F32), precision=HIGHEST,
                       preferred_element_type=F32)
        val = jnp.where(blk < own, gate, -jnp.inf)
        rank = jnp.zeros((nb, tq), jnp.int32)
        for jp in range(nb):
            row = val[jp:jp + 1, :]
            ahead = (row > val) | ((row == val) & (blk > jp))
            rank = rank + ahead.astype(jnp.int32)
        keep = ((rank < MOBA_TOPK) & (blk < own)) | (blk == own)
        neg = jnp.concatenate([jnp.where(keep, 0.0, NEG), jnp.zeros((HEAD_DIM - nb, tq), F32)], axis=0)
        qas.append(jnp.concatenate([q, neg.astype(BF16)], axis=0))

    def step(c, carries, diag):
        k0 = pl.multiple_of(c * tq, tq)
        new = []
        for h in range(MOBA_HEADS):
            sc = _dot(k_ref[0, h, pl.ds(k0, tq), :], qas[h])
            if diag:
                sc = sc + tri_ref[...]
            new.append(_chunk_update_t(carries[h], sc, vt_ref[0, h, :, pl.ds(k0, tq)]))
        return tuple(new)

    carries = step(i, tuple(_chunk_init_t(tq) for _ in range(MOBA_HEADS)), True)
    carries = lax.fori_loop(0, i, functools.partial(step, diag=False), carries)
    o_ref[0] = _heads_to_rows([_chunk_finish_t(cr) for cr in carries]).astype(o_ref.dtype)


def _moba(mqt, mk, mvt, avg, tri_t, *, tq=MOBA_TILE):
    b, nh, _, seq = mqt.shape
    nb = avg.shape[0]
    return pl.pallas_call(
        functools.partial(_moba_kernel, tq=tq),
        grid=(b, seq // tq),
        in_specs=[pl.BlockSpec((1, nh, HEAD_DIM, tq), lambda bi, i: (bi, 0, 0, i)),
                  pl.BlockSpec((1, nh, seq, LANES), lambda bi, i: (bi, 0, 0, 0)),
                  pl.BlockSpec((1, nh, LANES, seq), lambda bi, i: (bi, 0, 0, 0)),
                  pl.BlockSpec((nb, seq), lambda bi, i: (0, 0)),
                  pl.BlockSpec((tq, tq), lambda bi, i: (0, 0))],
        out_specs=pl.BlockSpec((1, tq, nh * HEAD_DIM), lambda bi, i: (bi, i, 0)),
        out_shape=jax.ShapeDtypeStruct((b, seq, nh * HEAD_DIM), BF16),
        scratch_shapes=[pltpu.VMEM((nh, nb, LANES), F32)],
        compiler_params=_params("parallel", "arbitrary"),
        name="moba",
    )(mqt, mk, mvt, avg, tri_t)


def _pack_w_in(w_in):
    depth, d, _ = w_in.shape

    def cols(a, b):
        return w_in[:, :, a:b]

    zeros = lambda n: jnp.zeros((depth, d, n), w_in.dtype)
    g_half = 3 * NSA_REP
    token_major = [cols(_OFF_KC, _OFF_KS),
                   cols(_OFF_KS, _OFF_VS), cols(_OFF_KW, _OFF_VW),
                   cols(_OFF_FOX_K, _OFF_FOX_V),
                   cols(_OFF_MOBA_K, _OFF_MOBA_V),
                   zeros(GATE_LANES), cols(_OFF_FOX_F, _OFF_FOX_F + FOX_HEADS),
                   zeros(LANES - GATE_LANES - FOX_HEADS)]
    transposed = [cols(_OFF_NSA_Q, _OFF_KC),
                  cols(_OFF_VS, _OFF_KW), cols(_OFF_VW, _OFF_GATE),
                  cols(_OFF_GATE, _OFF_GATE + g_half), zeros(GATE_ROWS - g_half),
                  cols(_OFF_GATE + g_half, _OFF_GATE + 2 * g_half), zeros(LANES - GATE_ROWS - g_half),
                  cols(_OFF_FOX_Q, _OFF_FOX_K), cols(_OFF_FOX_V, _OFF_FOX_F),
                  cols(_OFF_MOBA_Q, _OFF_MOBA_K), cols(_OFF_MOBA_V, IN_COLS)]
    w_p = jnp.concatenate(token_major, axis=-1).astype(BF16)
    w_t = jnp.swapaxes(jnp.concatenate(transposed, axis=-1), 1, 2).astype(BF16)
    return w_p, w_t


def _constants(seq):
    n_c = seq // CMP_STRIDE
    n_s = seq // SLC_LEN
    c0 = np.arange(n_c)[None, :] * CMP_STRIDE
    j0 = np.arange(LANES)[:, None] * SLC_LEN
    real = (np.arange(n_c)[None, :] < (seq - CMP_LEN) // CMP_STRIDE + 1) & (np.arange(LANES)[:, None] < n_s)
    ov = ((c0 < j0 + SLC_LEN) & (c0 + CMP_LEN > j0) & real).astype(np.float32)
    nb = seq // MOBA_BLOCK
    avg = (np.arange(nb)[:, None] == (np.arange(seq)[None, :] // MOBA_BLOCK)).astype(np.float32) / MOBA_BLOCK
    tri = np.tril(np.ones((256, 256), np.float32))

    kq = np.arange(FOX_TILE)
    causal_fox = np.where(kq[:, None] <= kq[None, :], 0.0, NEG).astype(np.float32)
    r = np.arange(MOBA_TILE)
    same = (r[:, None] // MOBA_BLOCK) == (r[None, :] // MOBA_BLOCK)
    allowed = np.where(same, r[:, None] <= r[None, :], r[:, None] < r[None, :])
    causal_moba = np.where(allowed, 0.0, NEG).astype(np.float32)
    return dict(ov=jnp.asarray(ov), avg=jnp.asarray(avg, dtype=BF16), tri=jnp.asarray(tri),
                causal_fox=jnp.asarray(causal_fox), causal_moba=jnp.asarray(causal_moba))


def kernel(x, c, positions, norm_g, w_ada, b_ada, w_in, fox_fbias, cmp_pos, cmp_w1, cmp_w2, w_out,
           ffn_w13, ffn_w2, final_g):
    batch, seq, d = x.shape
    depth = w_ada.shape[0]
    assert d == D_MODEL and seq % MOBA_BLOCK == 0 and seq % 1024 == 0

    w_in_p, w_in_t = _pack_w_in(w_in)
    w13 = ffn_w13.astype(BF16)
    w2 = ffn_w2.astype(BF16)
    w_o = w_out.astype(BF16)
    half = CMP_LEN * HEAD_DIM // 2
    cw1 = cmp_w1.astype(BF16).reshape(depth, 2, 2, half, CMP_HIDDEN)
    cw2 = cmp_w2.astype(BF16)
    cw2t = jnp.swapaxes(cw2, 2, 3)
    cpos = cmp_pos.reshape(depth, 2, 2, 1, half)
    cst = _constants(seq)

    inv = ROPE_THETA ** (-jnp.arange(0, ROPE_DIM, 2, dtype=F32) / ROPE_DIM)
    inv_col = inv.reshape(ROPE_DIM // 2, 1)
    fb_lane = jnp.zeros((depth, 1, LANES), F32).at[:, 0, GATE_LANES:GATE_LANES + FOX_HEADS].set(fox_fbias)

    mod_all = _ada_mod(c, w_ada, b_ada).reshape(depth, batch, 9, d)
    x2 = x.reshape(batch * seq, d)
    fg = final_g.reshape(1, d)
    for l in range(depth):
        mod = mod_all[l]
        x2 = _ffn(x2, mod, norm_g[l, 0:1], fg, w13[l, 0], w2[l, 0], sub=0, seq=seq, final=False)
        qt, nvt, gt, fqt, fvt, mqt, mvt, cmp_in, nks, nkw, fk, mk, gm = _inproj(
            x2, mod, norm_g[l, 1:2], positions, inv_col, fb_lane[l], w_in_p[l], w_in_t[l],
            batch=batch, seq=seq)
        cmp_kv, cmp_t = _compress(cmp_in.reshape(batch, 4, seq // CMP_STRIDE, CMP_STRIDE * HEAD_DIM),
                                  cpos[l], cw1[l], cw2[l], cw2t[l])
        o_nsa = _nsa(qt, cmp_kv, cmp_t, nks, nkw, nvt, gt, cst["ov"])
        o_fox = _fox(fqt, fk, fvt, gm, cst["tri"], cst["causal_fox"])
        o_moba = _moba(mqt, mk, mvt, cst["avg"], cst["causal_moba"])
        x2 = _mix_ffn(x2, mod, norm_g[l, 2:3], fg, o_nsa.reshape(batch * seq, -1),
                      o_fox.reshape(batch * seq, -1), o_moba.reshape(batch * seq, -1), w_o[l],
                      w13[l, 1], w2[l, 1], seq=seq, final=(l == depth - 1))
    return x2.reshape(batch, seq, d)
```

```python
import functools

import numpy as np
import jax
import jax.numpy as jnp
from jax import lax
from jax.experimental import pallas as pl
from jax.experimental.pallas import tpu as pltpu

D_MODEL = 1024
HEAD_DIM = 64
NSA_HEADS = 8
NSA_GROUPS = 2
NSA_REP = NSA_HEADS // NSA_GROUPS
FOX_HEADS = 4
MOBA_HEADS = 4
ROPE_DIM = HEAD_DIM // 4
ROPE_THETA = 500000.0
CMP_LEN = 32
CMP_STRIDE = 16
CMP_HIDDEN = 4 * HEAD_DIM
SLC_LEN = 64
SLC_TOPN = 8
WIN = 512
MOBA_BLOCK = 256
MOBA_TOPK = 3
D_FF = 2816
EPS = 1e-6
ATT_SCALE = HEAD_DIM ** -0.5

FOX_TILE = 512
FOX_CHUNK = 512
MOBA_TILE = 2 * MOBA_BLOCK
LANES = 128
MXU_TILE = 256
NEG = -1e30
VMEM_LIMIT = 56 * 1024 * 1024

F32 = jnp.float32
BF16 = jnp.bfloat16
HIGHEST = lax.Precision.HIGHEST

_KV = NSA_GROUPS * HEAD_DIM
_OFF_NSA_Q = 0
_OFF_KC = _OFF_NSA_Q + NSA_HEADS * HEAD_DIM
_OFF_VC = _OFF_KC + _KV
_OFF_KS = _OFF_VC + _KV
_OFF_VS = _OFF_KS + _KV
_OFF_KW = _OFF_VS + _KV
_OFF_VW = _OFF_KW + _KV
_OFF_GATE = _OFF_VW + _KV
_OFF_FOX_Q = _OFF_GATE + 3 * NSA_HEADS
_OFF_FOX_K = _OFF_FOX_Q + FOX_HEADS * HEAD_DIM
_OFF_FOX_V = _OFF_FOX_K + FOX_HEADS * HEAD_DIM
_OFF_FOX_F = _OFF_FOX_V + FOX_HEADS * HEAD_DIM
_OFF_MOBA_Q = _OFF_FOX_F + FOX_HEADS
_OFF_MOBA_K = _OFF_MOBA_Q + MOBA_HEADS * HEAD_DIM
_OFF_MOBA_V = _OFF_MOBA_K + MOBA_HEADS * HEAD_DIM
IN_COLS = _OFF_MOBA_V + MOBA_HEADS * HEAD_DIM

N_HEAD_COLS = 1024
GATE_LANES = 3 * NSA_REP
PROJ_COLS = N_HEAD_COLS + LANES
GATE_ROWS = 16
_T_NSA_Q = 0
_T_NSA_V = _T_NSA_Q + NSA_HEADS * HEAD_DIM
_T_GATE = _T_NSA_V + 4 * HEAD_DIM
_T_FOX_Q = _T_GATE + LANES
_T_FOX_V = _T_FOX_Q + FOX_HEADS * HEAD_DIM
_T_MOBA_Q = _T_FOX_V + FOX_HEADS * HEAD_DIM
_T_MOBA_V = _T_MOBA_Q + MOBA_HEADS * HEAD_DIM
T_ROWS = _T_MOBA_V + MOBA_HEADS * HEAD_DIM


def _dot(a, b):
    return jnp.dot(a, b, preferred_element_type=F32)


def _dot_nt(a, b, precision=None):
    return lax.dot_general(a, b, (((1,), (1,)), ((), ())), precision=precision,
                           preferred_element_type=F32)


def _iota(shape, dim):
    return lax.broadcasted_iota(jnp.int32, shape, dim)


def _params(*sem, flags=None):
    return pltpu.CompilerParams(dimension_semantics=sem, vmem_limit_bytes=VMEM_LIMIT, flags=flags)


def _ada_norm(x, g, shift, scale):
    ms = jnp.mean(x * x, axis=-1, keepdims=True)
    y = x * lax.rsqrt(ms + EPS) * g
    return y * (1.0 + scale) + shift


def _ada_kernel(c_ref, w_ref, b_ref, o_ref):
    c = c_ref[...]
    ca = c * jax.nn.sigmoid(c)
    o_ref[0] = jnp.dot(ca, w_ref[0], precision=HIGHEST, preferred_element_type=F32) + b_ref[0]


def _ada_mod(c, w_ada, b_ada):
    depth, d, n = w_ada.shape
    b = c.shape[0]
    tn = 2304
    return pl.pallas_call(
        _ada_kernel,
        grid=(depth, n // tn),
        in_specs=[pl.BlockSpec((b, d), lambda l, j: (0, 0)),
                  pl.BlockSpec((1, d, tn), lambda l, j: (l, 0, j)),
                  pl.BlockSpec((1, 1, tn), lambda l, j: (l, 0, j))],
        out_specs=pl.BlockSpec((1, b, tn), lambda l, j: (l, 0, j)),
        out_shape=jax.ShapeDtypeStruct((depth, b, n), F32),
        compiler_params=_params("parallel", "parallel"),
        name="ada_mod",
    )(c, w_ada, b_ada.reshape(depth, 1, n))


def _swiglu_residual(x, mod_ref, g_ref, fg_ref, w13_ref, w2_ref, *, sub, coef, final, splits):
    h = _ada_norm(x, g_ref[...], mod_ref[0, 3 * sub:3 * sub + 1, :],
                  mod_ref[0, 3 * sub + 1:3 * sub + 2, :]).astype(BF16)
    acc = None
    for lo, hi in splits:
        a = _dot(h, w13_ref[:, lo:hi])
        b = _dot(h, w13_ref[:, D_FF + lo:D_FF + hi])
        u = (a * jax.nn.sigmoid(a) * b).astype(BF16)
        part = _dot(u, w2_ref[lo:hi, :])
        acc = part if acc is None else acc + part
    y = x + (coef * mod_ref[0, 3 * sub + 2:3 * sub + 3, :]) * acc
    if final:
        y = y * lax.rsqrt(jnp.mean(y * y, axis=-1, keepdims=True) + EPS) * fg_ref[...]
    return y


def _ffn_kernel(x_ref, mod_ref, g_ref, fg_ref, w13_ref, w2_ref, o_ref, **kw):
    o_ref[...] = _swiglu_residual(x_ref[...], mod_ref, g_ref, fg_ref, w13_ref, w2_ref, **kw)


def _mix_ffn_kernel(x_ref, mod_ref, g_ref, fg_ref, on_ref, of_ref, om_ref, wo_ref, w13_ref, w2_ref, o_ref, **kw):
    n_w = NSA_HEADS * HEAD_DIM
    f_w = FOX_HEADS * HEAD_DIM
    mix = _dot(on_ref[...], wo_ref[0:n_w, :])
    mix += _dot(of_ref[...], wo_ref[n_w:n_w + f_w, :])
    mix += _dot(om_ref[...], wo_ref[n_w + f_w:, :])
    x = x_ref[...] + mod_ref[0, 5:6, :] * mix
    o_ref[...] = _swiglu_residual(x, mod_ref, g_ref, fg_ref, w13_ref, w2_ref, **kw)


def _resident(shape):
    return pl.BlockSpec(shape, lambda i: (0,) * len(shape), pipeline_mode=pl.Buffered(1))


def _hidden_splits(chunk):
    return tuple((lo, min(lo + chunk, D_FF)) for lo in range(0, D_FF, chunk))


def _ffn(x2, mod, g, final_g, w13, w2, *, sub, seq, final, tm=1024, chunk=3 * MXU_TILE):
    n, d = x2.shape
    per_b = seq // tm
    return pl.pallas_call(
        functools.partial(_ffn_kernel, sub=sub, coef=0.5, final=final, splits=_hidden_splits(chunk)),
        grid=(n // tm,),
        in_specs=[pl.BlockSpec((tm, d), lambda i: (i, 0)),
                  pl.BlockSpec((1, 9, d), lambda i: (i // per_b, 0, 0)),
                  pl.BlockSpec((1, d), lambda i: (0, 0)),
                  pl.BlockSpec((1, d), lambda i: (0, 0)),
                  _resident((d, 2 * D_FF)), _resident((D_FF, d))],
        out_specs=pl.BlockSpec((tm, d), lambda i: (i, 0)),
        out_shape=jax.ShapeDtypeStruct((n, d), F32),
        compiler_params=_params("parallel"),
        name="ffn",
    )(x2, mod, g, final_g, w13, w2)


def _mix_ffn(x2, mod, g, final_g, o_nsa, o_fox, o_moba, w_out, w13, w2, *, seq, final, tm=512,
             chunk=4 * MXU_TILE):
    n, d = x2.shape
    per_b = seq // tm
    row = lambda width: pl.BlockSpec((tm, width), lambda i: (i, 0))
    return pl.pallas_call(
        functools.partial(_mix_ffn_kernel, sub=2, coef=0.5, final=final, splits=_hidden_splits(chunk)),
        grid=(n // tm,),
        in_specs=[row(d),
                  pl.BlockSpec((1, 9, d), lambda i: (i // per_b, 0, 0)),
                  pl.BlockSpec((1, d), lambda i: (0, 0)),
                  pl.BlockSpec((1, d), lambda i: (0, 0)),
                  row(o_nsa.shape[1]), row(o_fox.shape[1]), row(o_moba.shape[1]),
                  _resident((d, d)), _resident((d, 2 * D_FF)), _resident((D_FF, d))],
        out_specs=row(d),
        out_shape=jax.ShapeDtypeStruct((n, d), F32),
        compiler_params=_params("parallel"),
        name="mix_ffn",
    )(x2, mod, g, final_g, o_nsa, o_fox, o_moba, w_out, w13, w2)


def _inproj_kernel(x_ref, mod_ref, g_ref, posr_ref, invc_ref, fb_ref, w_ref, wt_ref,
                   qt_ref, nvt_ref, gt_ref, fqt_ref, fvt_ref, mqt_ref, mvt_ref,
                   cmp_ref, nks_ref, nkw_ref, fk_ref, mk_ref, gm_ref, *, per_b):
    h = _ada_norm(x_ref[...], g_ref[...], mod_ref[0, 3:4, :], mod_ref[0, 4:5, :]).astype(BF16)
    tm = h.shape[0]
    seq_pos = (pl.program_id(0) % per_b) * tm + _iota((tm, LANES), 0)

    half = ROPE_DIM // 2
    ang_t = invc_ref[...] * posr_ref[0].astype(F32)
    cos_t = jnp.cos(ang_t)
    sin_t = jnp.sin(ang_t)
    ones_row = jnp.where(_iota((HEAD_DIM, tm), 0) == 0, 1.0, 0.0)
    slab_rows = 4 * HEAD_DIM

    def queries_t(row0, n_heads, out_ref, rotate):
        for s_idx in range(n_heads // 4):
            y_t = _dot_nt(wt_ref[row0 + slab_rows * s_idx:row0 + slab_rows * (s_idx + 1), :], h)
            for hh in range(4):
                blk = y_t[HEAD_DIM * hh:HEAD_DIM * (hh + 1)]
                if rotate:
                    r1, r2 = blk[:half], blk[half:2 * half]
                    blk = jnp.concatenate([r1 * cos_t - r2 * sin_t, r2 * cos_t + r1 * sin_t, blk[2 * half:]],
                                          axis=0)
                out_ref[0, 4 * s_idx + hh] = (blk * ATT_SCALE).astype(BF16)

    def values_t(row0, out_ref):
        y_t = _dot_nt(wt_ref[row0:row0 + slab_rows, :], h)
        for idx in range(4):
            out_ref[0, idx] = jnp.concatenate([y_t[HEAD_DIM * idx:HEAD_DIM * (idx + 1)], ones_row],
                                              axis=0).astype(BF16)

    queries_t(_T_NSA_Q, NSA_HEADS, qt_ref, True)
    values_t(_T_NSA_V, nvt_ref)
    sig_t = jax.nn.sigmoid(_dot_nt(wt_ref[_T_GATE:_T_GATE + LANES, :], h))
    for g in range(NSA_GROUPS):
        gt_ref[0, g] = sig_t[GATE_ROWS * g:GATE_ROWS * (g + 1)]
    queries_t(_T_FOX_Q, FOX_HEADS, fqt_ref, False)
    values_t(_T_FOX_V, fvt_ref)
    queries_t(_T_MOBA_Q, MOBA_HEADS, mqt_ref, True)
    values_t(_T_MOBA_V, mvt_ref)

    zeros_t = jnp.zeros((half, tm), F32)

    def lane_table(first, second, fill):
        head = jnp.concatenate([first, second, jnp.full((HEAD_DIM - ROPE_DIM, tm), fill, F32)], axis=0)
        return jnp.concatenate([head, head], axis=0).T

    cosl = lane_table(cos_t, cos_t, 1.0)
    sin_lo = lane_table(-sin_t, zeros_t, 0.0)
    sin_hi = lane_table(zeros_t, sin_t, 0.0)
    lane = _iota((tm, LANES), 1)

    def rope(y):
        return (y * cosl + pltpu.roll(y, LANES - ROPE_DIM // 2, 1) * sin_lo
                + pltpu.roll(y, ROPE_DIM // 2, 1) * sin_hi)

    def put_heads(z, out_ref, head0, dtype):
        z = z.astype(dtype)
        out_ref[0, head0] = z[:, :HEAD_DIM]
        out_ref[0, head0 + 1] = z[:, HEAD_DIM:]

    def put_wide(z, out_ref, head0, tail):
        out_ref[0, head0] = jnp.where(lane < HEAD_DIM, z, tail).astype(BF16)
        out_ref[0, head0 + 1] = jnp.where(lane < HEAD_DIM, pltpu.roll(z, HEAD_DIM, 1), tail).astype(BF16)

    def put_keys(z, out_ref, head0, block_len):
        block = lax.shift_right_logical(seq_pos, block_len.bit_length() - 1)
        put_wide(z, out_ref, head0, jnp.where(lane - HEAD_DIM == block, 1.0, 0.0))

    def slab(c0):
        y = _dot(h, w_ref[:, c0:c0 + 2 * LANES])
        return y[:, :LANES], y[:, LANES:]

    a, b = slab(0)
    put_heads(rope(a), cmp_ref, 0, F32)
    put_heads(b, cmp_ref, 2, F32)
    a, b = slab(256)
    put_keys(rope(a), nks_ref, 0, SLC_LEN)
    put_heads(rope(b), nkw_ref, 0, BF16)
    a, b = slab(512)
    put_heads(a, fk_ref, 0, BF16)
    put_heads(b, fk_ref, 2, BF16)
    a, b = slab(768)
    put_keys(rope(a), mk_ref, 0, MOBA_BLOCK)
    put_keys(rope(b), mk_ref, 2, MOBA_BLOCK)
    f = _dot(h, w_ref[:, N_HEAD_COLS:])
    gm_ref[0] = jax.nn.log_sigmoid(f + fb_ref[...])


def _inproj(x2, mod, g, positions, inv_col, fb_lane, w_p, w_t, *, batch, seq, tm=512):
    n, d = x2.shape
    per_b = seq // tm

    def hm(nh, dtype, width=HEAD_DIM):
        return (pl.BlockSpec((1, nh, tm, width), lambda i: (i // per_b, 0, i % per_b, 0)),
                jax.ShapeDtypeStruct((batch, nh, seq, width), dtype))

    def tr(nh, rows, dtype):
        return (pl.BlockSpec((1, nh, rows, tm), lambda i: (i // per_b, 0, 0, i % per_b)),
                jax.ShapeDtypeStruct((batch, nh, rows, seq), dtype))

    specs = [tr(NSA_HEADS, HEAD_DIM, BF16), tr(4, LANES, BF16), tr(NSA_GROUPS, GATE_ROWS, F32),
             tr(FOX_HEADS, HEAD_DIM, BF16), tr(FOX_HEADS, LANES, BF16),
             tr(MOBA_HEADS, HEAD_DIM, BF16), tr(MOBA_HEADS, LANES, BF16),
             hm(4, F32), hm(2, BF16, LANES), hm(2, BF16), hm(FOX_HEADS, BF16), hm(MOBA_HEADS, BF16, LANES),
             (pl.BlockSpec((1, tm, LANES), lambda i: (i // per_b, i % per_b, 0)),
              jax.ShapeDtypeStruct((batch, seq, LANES), F32))]
    const = lambda shape: pl.BlockSpec(shape, lambda i: (0,) * len(shape))
    return pl.pallas_call(
        functools.partial(_inproj_kernel, per_b=per_b),
        grid=(n // tm,),
        in_specs=[pl.BlockSpec((tm, d), lambda i: (i, 0)),
                  pl.BlockSpec((1, 9, d), lambda i: (i // per_b, 0, 0)),
                  const((1, d)),
                  pl.BlockSpec((1, 1, tm), lambda i: (i, 0, 0)),
                  const((ROPE_DIM // 2, 1)), const((1, LANES)),
                  _resident(w_p.shape), _resident(w_t.shape)],
        out_specs=[s for s, _ in specs],
        out_shape=[s for _, s in specs],
        compiler_params=_params("parallel"),
        name="inproj",
    )(x2, mod, g, positions.reshape(n // tm, 1, tm), inv_col, fb_lane, w_p, w_t)


def _compress_kernel(z_ref, pos_ref, w1_ref, w2_ref, w2t_ref, o_ref, ot_ref):
    z = z_ref[0, 0]
    top = _dot((z + pos_ref[0, 0]).astype(BF16), w1_ref[0, 0])
    bot = _dot((z + pos_ref[0, 1]).astype(BF16), w1_ref[0, 1])
    nch = z.shape[0]
    hid = top + pltpu.roll(bot, nch - 1, 0)
    act = jax.nn.gelu(hid).astype(BF16)
    o_ref[0, 0] = _dot(act, w2_ref[0]).astype(o_ref.dtype)
    ot_ref[0, 0] = _dot_nt(w2t_ref[0], act).astype(ot_ref.dtype)


def _compress(cmp_in, pos_l, w1_l, w2_l, w2t_l):
    b, four, nch, width = cmp_in.shape
    return pl.pallas_call(
        _compress_kernel,
        grid=(b, four),
        in_specs=[pl.BlockSpec((1, 1, nch, width), lambda i, n: (i, n, 0, 0)),
                  pl.BlockSpec((1, 2, 1, width), lambda i, n: (n // 2, 0, 0, 0)),
                  pl.BlockSpec((1, 2, width, CMP_HIDDEN), lambda i, n: (n // 2, 0, 0, 0)),
                  pl.BlockSpec((1, CMP_HIDDEN, HEAD_DIM), lambda i, n: (n // 2, 0, 0)),
                  pl.BlockSpec((1, HEAD_DIM, CMP_HIDDEN), lambda i, n: (n // 2, 0, 0))],
        out_specs=[pl.BlockSpec((1, 1, nch, HEAD_DIM), lambda i, n: (i, n, 0, 0)),
                   pl.BlockSpec((1, 1, HEAD_DIM, nch), lambda i, n: (i, n, 0, 0))],
        out_shape=[jax.ShapeDtypeStruct((b, four, nch, HEAD_DIM), BF16),
                   jax.ShapeDtypeStruct((b, four, HEAD_DIM, nch), BF16)],
        compiler_params=_params("parallel", "parallel"),
        name="nsa_compress",
    )(cmp_in, pos_l, w1_l, w2_l, w2t_l)


def _chunk_update_t(carry, s, v_t):
    m, acc = carry
    m_new = jnp.maximum(m, jnp.max(s, axis=0, keepdims=True))
    p = jnp.exp(s - m_new).astype(BF16)
    return m_new, jnp.exp(m - m_new) * acc + _dot(v_t, p)


def _chunk_init_t(n):
    return jnp.full((1, n), NEG, F32), jnp.zeros((LANES, n), F32)


def _chunk_finish_t(carry):
    _, acc = carry
    return acc[:HEAD_DIM] / jnp.maximum(acc[HEAD_DIM:HEAD_DIM + 1], 1e-30)


def _heads_to_rows(heads_t):
    pairs = [jnp.concatenate(heads_t[a:a + 2], axis=0).T for a in range(0, len(heads_t), 2)]
    return jnp.concatenate(pairs, axis=1)


def _nsa_kernel(qt_ref, kc_ref, vct_ref, ks_ref, vst_ref, kw_ref, vwt_ref, gt_ref, ovt_ref, o_ref, *, tq, tk):
    seq = ks_ref.shape[2]
    for i in range(seq // tq):
        q_cols = slice(i * tq, (i + 1) * tq)
        _nsa_tile(i, qt_ref.at[:, :, :, q_cols], kc_ref, vct_ref, ks_ref, vst_ref, kw_ref, vwt_ref,
                  gt_ref.at[:, :, :, q_cols], ovt_ref, o_ref.at[:, q_cols, :], tq=tq, tk=tk)


def _nsa_tile(i, qt_ref, kc_ref, vct_ref, ks_ref, vst_ref, kw_ref, vwt_ref, gt_ref, ovt_ref, o_ref, *, tq, tk):
    rep = NSA_REP
    groups = range(NSA_GROUPS)
    n_c = kc_ref.shape[2]
    n_s = ks_ref.shape[2] // SLC_LEN
    tile = lambda a: jnp.concatenate([a] * rep, axis=1)
    t_q = i * tq + _iota((1, tq), 1)
    cend = _iota((n_c, 1), 0) * CMP_STRIDE + (CMP_LEN - 1)
    cm = cend <= tile(t_q)
    j_t = _iota((n_s, tq), 0)
    tb = lax.shift_right_logical(t_q, SLC_LEN.bit_length() - 1)
    valid = j_t <= tb
    forced = (j_t == 0) | (j_t == tb) | (j_t == tb - 1)
    start = max(i * tq - WIN, 0)
    span = (i + 1) * tq - start
    dist = t_q - (start + _iota((span, 1), 0))
    band = tile(jnp.where((dist >= 0) & (dist < WIN), 0.0, NEG))

    o_cmp, o_win, q4a = [], [], []
    for g in groups:
        q4 = jnp.concatenate([qt_ref[0, rep * g + r] for r in range(rep)], axis=1)
        s = jnp.where(cm, _dot(kc_ref[0, g], q4), NEG)
        e = jnp.where(cm, jnp.exp(s - jnp.max(s, axis=0, keepdims=True)), 0.0)
        p = e / jnp.maximum(jnp.sum(e, axis=0, keepdims=True), 1e-30)
        o_cmp.append(_dot(vct_ref[0, g], p.astype(BF16)))
        psum = p[:, 0:tq] + p[:, tq:2 * tq] + p[:, 2 * tq:3 * tq] + p[:, 3 * tq:4 * tq]
        imp = jnp.dot(ovt_ref[...], psum, precision=HIGHEST, preferred_element_type=F32)[:n_s]
        v_t = jnp.where(valid, jnp.where(forced, jnp.inf, imp), -jnp.inf)
        rank = jnp.zeros((n_s, tq), jnp.int32)
        for jp in range(n_s):
            row = v_t[jp:jp + 1, :]
            ahead = (row > v_t) | ((row == v_t) & (j_t > jp))
            rank = rank + ahead.astype(jnp.int32)
        neg_t = jnp.where((rank < SLC_TOPN) & (v_t > -jnp.inf), 0.0, NEG)
        neg = jnp.concatenate([neg_t, jnp.zeros((HEAD_DIM - n_s, tq), F32)], axis=0)
        q4a.append(jnp.concatenate([q4, tile(neg).astype(BF16)], axis=0))
        sw = _dot(kw_ref[0, g, start:start + span, :], q4) + band
        pw = jnp.exp(sw - jnp.max(sw, axis=0, keepdims=True)).astype(BF16)
        aw = _dot(vwt_ref[0, g, :, start:start + span], pw)
        o_win.append(aw[:HEAD_DIM] / jnp.maximum(aw[HEAD_DIM:HEAD_DIM + 1], 1e-30))

    def slc_step(k0, k1, carries, diag):
        new = []
        for g in groups:
            sc = _dot(ks_ref[0, g, k0:k1, :], q4a[g])
            if diag:
                sc = sc + tile(jnp.where(k0 + _iota((k1 - k0, 1), 0) <= t_q, 0.0, NEG))
            new.append(_chunk_update_t(carries[g], sc, vst_ref[0, g, :, k0:k1]))
        return tuple(new)

    end = (i + 1) * tq
    last = (end - 1) // tk * tk
    carries = slc_step(last, end, tuple(_chunk_init_t(rep * tq) for _ in groups), True)
    for k0 in range(0, last, tk):
        carries = slc_step(k0, k0 + tk, carries, False)

    heads = []
    for g in groups:
        o_slc = _chunk_finish_t(carries[g])
        gates = gt_ref[0, g]
        for r in range(rep):
            cols = slice(r * tq, (r + 1) * tq)
            heads.append(gates[3 * r:3 * r + 1] * o_cmp[g][:, cols]
                         + gates[3 * r + 1:3 * r + 2] * o_slc[:, cols]
                         + gates[3 * r + 2:3 * r + 3] * o_win[g][:, cols])
    o_ref[0] = _heads_to_rows(heads).astype(o_ref.dtype)


def _nsa(qt, cmp_kv, cmp_t, nks, nkw, nvt, gt, ovt, *, tq=256, tk=512):
    b, _, _, seq = qt.shape
    n_c = cmp_kv.shape[2]
    ng = NSA_GROUPS
    return pl.pallas_call(
        functools.partial(_nsa_kernel, tq=tq, tk=tk),
        grid=(b,),
        in_specs=[pl.BlockSpec((1, NSA_HEADS, HEAD_DIM, seq), lambda bi: (bi, 0, 0, 0)),
                  pl.BlockSpec((1, ng, n_c, HEAD_DIM), lambda bi: (bi, 0, 0, 0)),
                  pl.BlockSpec((1, ng, HEAD_DIM, n_c), lambda bi: (bi, 1, 0, 0)),
                  pl.BlockSpec((1, ng, seq, LANES), lambda bi: (bi, 0, 0, 0)),
                  pl.BlockSpec((1, ng, LANES, seq), lambda bi: (bi, 0, 0, 0)),
                  pl.BlockSpec((1, ng, seq, HEAD_DIM), lambda bi: (bi, 0, 0, 0)),
                  pl.BlockSpec((1, ng, LANES, seq), lambda bi: (bi, 1, 0, 0)),
                  pl.BlockSpec((1, ng, GATE_ROWS, seq), lambda bi: (bi, 0, 0, 0)),
                  pl.BlockSpec((LANES, n_c), lambda bi: (0, 0))],
        out_specs=pl.BlockSpec((1, seq, NSA_HEADS * HEAD_DIM), lambda bi: (bi, 0, 0)),
        out_shape=jax.ShapeDtypeStruct((b, seq, NSA_HEADS * HEAD_DIM), BF16),
        compiler_params=_params("parallel"),
        name="nsa",
    )(qt, cmp_kv, cmp_t, nks, nvt, nkw, nvt, gt, ovt)


def _fox_kernel(qt_ref, k_ref, vt_ref, lf_ref, cum_ref, tri_ref, o_ref, fcol_ref, frow_ref, *, tq):
    heads = range(FOX_HEADS)
    seq = lf_ref.shape[1]
    blk = cum_ref.shape[0]
    carry = jnp.zeros((1, LANES), F32)
    for c in range(seq // blk):
        rows = slice(c * blk, (c + 1) * blk)
        cs = jnp.dot(cum_ref[...], lf_ref[0, rows, :], precision=HIGHEST,
                     preferred_element_type=F32) + carry
        fcol_ref[rows, :] = cs
        carry = cs[blk - 1:blk, :]
    frow_ref[...] = fcol_ref[...].T[8:16, :]

    tk = tri_ref.shape[1]
    per_chunk = tk // tq
    for i in range(seq // tq):
        q_cols = slice(i * tq, (i + 1) * tq)
        qs = [qt_ref[0, h, :, q_cols] for h in heads]
        fqs = [frow_ref[4 + h:5 + h, q_cols] for h in heads]

        def step(c, carries, diag, qs=qs, fqs=fqs, i=i):
            keys = slice(c * tk, (c + 1) * tk)
            new = []
            for h in heads:
                fk = fcol_ref[keys, GATE_LANES + h:GATE_LANES + h + 1]
                sc = _dot(k_ref[0, h, keys, :], qs[h]) + (fqs[h] - fk)
                if diag:
                    sc = sc + tri_ref[i % per_chunk]
                new.append(_chunk_update_t(carries[h], sc, vt_ref[0, h, :, keys]))
            return tuple(new)

        carries = step(i // per_chunk, tuple(_chunk_init_t(tq) for _ in heads), True)
        for c in range(i // per_chunk):
            carries = step(c, carries, False)
        o_ref[0, q_cols, :] = _heads_to_rows([_chunk_finish_t(cr) for cr in carries]).astype(o_ref.dtype)


def _fox(fqt, fk, fvt, gm, cum, tri_t, *, tq=FOX_TILE):
    b, nh, _, seq = fqt.shape
    blk = cum.shape[0]
    return pl.pallas_call(
        functools.partial(_fox_kernel, tq=tq),
        grid=(b,),
        in_specs=[pl.BlockSpec((1, nh, HEAD_DIM, seq), lambda bi: (bi, 0, 0, 0)),
                  pl.BlockSpec((1, nh, seq, HEAD_DIM), lambda bi: (bi, 0, 0, 0)),
                  pl.BlockSpec((1, nh, LANES, seq), lambda bi: (bi, 0, 0, 0)),
                  pl.BlockSpec((1, seq, LANES), lambda bi: (bi, 0, 0)),
                  pl.BlockSpec((blk, blk), lambda bi: (0, 0)),
                  pl.BlockSpec(tri_t.shape, lambda bi: (0, 0, 0))],
        out_specs=pl.BlockSpec((1, seq, nh * HEAD_DIM), lambda bi: (bi, 0, 0)),
        out_shape=jax.ShapeDtypeStruct((b, seq, nh * HEAD_DIM), BF16),
        scratch_shapes=[pltpu.VMEM((seq, LANES), F32), pltpu.VMEM((8, seq), F32)],
        compiler_params=_params("parallel"),
        name="fox",
    )(fqt, fk, fvt, gm, cum, tri_t)


def _moba_kernel(qt_ref, k_ref, vt_ref, avg_ref, tri_ref, o_ref, *, tq):
    heads = range(MOBA_HEADS)
    nb = avg_ref.shape[0]
    seq = k_ref.shape[2]
    kms = [_dot(avg_ref[...], k_ref[0, h])[:, :HEAD_DIM] for h in heads]
    blk = _iota((nb, tq), 0)
    in_tile = lax.shift_right_logical(_iota((nb, tq), 1), MOBA_BLOCK.bit_length() - 1)

    for i in range(seq // tq):
        q_cols = slice(i * tq, (i + 1) * tq)
        own = i * (tq // MOBA_BLOCK) + in_tile
        qas = []
        for h in heads:
            q = qt_ref[0, h, :, q_cols]
            gate = jnp.dot(kms[h], q.astype(F32), precision=HIGHEST, preferred_element_type=F32)
            val = jnp.where(blk < own, gate, -jnp.inf)
            rank = jnp.zeros((nb, tq), jnp.int32)
            for jp in range(nb):
                row = val[jp:jp + 1, :]
                ahead = (row > val) | ((row == val) & (blk > jp))
                rank = rank + ahead.astype(jnp.int32)
            keep = ((rank < MOBA_TOPK) & (blk < own)) | (blk == own)
            neg = jnp.concatenate([jnp.where(keep, 0.0, NEG), jnp.zeros((HEAD_DIM - nb, tq), F32)], axis=0)
            qas.append(jnp.concatenate([q, neg.astype(BF16)], axis=0))

        def step(c, carries, diag, qas=qas):
            keys = slice(c * tq, (c + 1) * tq)
            new = []
            for h in heads:
                sc = _dot(k_ref[0, h, keys, :], qas[h])
                if diag:
                    sc = sc + tri_ref[...]
                new.append(_chunk_update_t(carries[h], sc, vt_ref[0, h, :, keys]))
            return tuple(new)

        carries = step(i, tuple(_chunk_init_t(tq) for _ in heads), True)
        for c in range(i):
            carries = step(c, carries, False)
        o_ref[0, q_cols, :] = _heads_to_rows([_chunk_finish_t(cr) for cr in carries]).astype(o_ref.dtype)


def _moba(mqt, mk, mvt, avg, tri_t, *, tq=MOBA_TILE):
    b, nh, _, seq = mqt.shape
    nb = avg.shape[0]
    return pl.pallas_call(
        functools.partial(_moba_kernel, tq=tq),
        grid=(b,),
        in_specs=[pl.BlockSpec((1, nh, HEAD_DIM, seq), lambda bi: (bi, 0, 0, 0)),
                  pl.BlockSpec((1, nh, seq, LANES), lambda bi: (bi, 0, 0, 0)),
                  pl.BlockSpec((1, nh, LANES, seq), lambda bi: (bi, 0, 0, 0)),
                  pl.BlockSpec((nb, seq), lambda bi: (0, 0)),
                  pl.BlockSpec((tq, tq), lambda bi: (0, 0))],
        out_specs=pl.BlockSpec((1, seq, nh * HEAD_DIM), lambda bi: (bi, 0, 0)),
        out_shape=jax.ShapeDtypeStruct((b, seq, nh * HEAD_DIM), BF16),
        compiler_params=_params("parallel"),
        name="moba",
    )(mqt, mk, mvt, avg, tri_t)


def _pack_w_in(w_in):
    depth, d, _ = w_in.shape

    def cols(a, b):
        return w_in[:, :, a:b]

    zeros = lambda n: jnp.zeros((depth, d, n), w_in.dtype)
    g_half = 3 * NSA_REP
    token_major = [cols(_OFF_KC, _OFF_KS),
                   cols(_OFF_KS, _OFF_VS), cols(_OFF_KW, _OFF_VW),
                   cols(_OFF_FOX_K, _OFF_FOX_V),
                   cols(_OFF_MOBA_K, _OFF_MOBA_V),
                   zeros(GATE_LANES), cols(_OFF_FOX_F, _OFF_FOX_F + FOX_HEADS),
                   zeros(LANES - GATE_LANES - FOX_HEADS)]
    transposed = [cols(_OFF_NSA_Q, _OFF_KC),
                  cols(_OFF_VS, _OFF_KW), cols(_OFF_VW, _OFF_GATE),
                  cols(_OFF_GATE, _OFF_GATE + g_half), zeros(GATE_ROWS - g_half),
                  cols(_OFF_GATE + g_half, _OFF_GATE + 2 * g_half), zeros(LANES - GATE_ROWS - g_half),
                  cols(_OFF_FOX_Q, _OFF_FOX_K), cols(_OFF_FOX_V, _OFF_FOX_F),
                  cols(_OFF_MOBA_Q, _OFF_MOBA_K), cols(_OFF_MOBA_V, IN_COLS)]
    w_p = jnp.concatenate(token_major, axis=-1).astype(BF16)
    w_t = jnp.swapaxes(jnp.concatenate(transposed, axis=-1), 1, 2).astype(BF16)
    return w_p, w_t


def _constants(seq):
    n_c = seq // CMP_STRIDE
    n_s = seq // SLC_LEN
    c0 = np.arange(n_c)[None, :] * CMP_STRIDE
    j0 = np.arange(LANES)[:, None] * SLC_LEN
    real = (np.arange(n_c)[None, :] < (seq - CMP_LEN) // CMP_STRIDE + 1) & (np.arange(LANES)[:, None] < n_s)
    ov = ((c0 < j0 + SLC_LEN) & (c0 + CMP_LEN > j0) & real).astype(np.float32)
    nb = seq // MOBA_BLOCK
    avg = (np.arange(nb)[:, None] == (np.arange(seq)[None, :] // MOBA_BLOCK)).astype(np.float32) / MOBA_BLOCK
    tri = np.tril(np.ones((256, 256), np.float32))

    kk = np.arange(FOX_CHUNK)[None, :, None]
    qq = np.arange(FOX_TILE)[None, None, :] + FOX_TILE * np.arange(FOX_CHUNK // FOX_TILE)[:, None, None]
    causal_fox = np.where(kk <= qq, 0.0, NEG).astype(np.float32)
    r = np.arange(MOBA_TILE)
    same = (r[:, None] // MOBA_BLOCK) == (r[None, :] // MOBA_BLOCK)
    allowed = np.where(same, r[:, None] <= r[None, :], r[:, None] < r[None, :])
    causal_moba = np.where(allowed, 0.0, NEG).astype(np.float32)
    return dict(ov=jnp.asarray(ov), avg=jnp.asarray(avg, dtype=BF16), tri=jnp.asarray(tri),
                causal_fox=jnp.asarray(causal_fox), causal_moba=jnp.asarray(causal_moba))


def kernel(x, c, positions, norm_g, w_ada, b_ada, w_in, fox_fbias, cmp_pos, cmp_w1, cmp_w2, w_out,
           ffn_w13, ffn_w2, final_g):
    batch, seq, d = x.shape
    depth = w_ada.shape[0]
    assert d == D_MODEL and seq % MOBA_BLOCK == 0 and seq % 1024 == 0

    w_in_p, w_in_t = _pack_w_in(w_in)
    w13 = ffn_w13.astype(BF16)
    w2 = ffn_w2.astype(BF16)
    w_o = w_out.astype(BF16)
    half = CMP_LEN * HEAD_DIM // 2
    cw1 = cmp_w1.astype(BF16).reshape(depth, 2, 2, half, CMP_HIDDEN)
    cw2 = cmp_w2.astype(BF16)
    cw2t = jnp.swapaxes(cw2, 2, 3)
    cpos = cmp_pos.reshape(depth, 2, 2, 1, half)
    cst = _constants(seq)

    inv = ROPE_THETA ** (-jnp.arange(0, ROPE_DIM, 2, dtype=F32) / ROPE_DIM)
    inv_col = inv.reshape(ROPE_DIM // 2, 1)
    fb_lane = jnp.zeros((depth, 1, LANES), F32).at[:, 0, GATE_LANES:GATE_LANES + FOX_HEADS].set(fox_fbias)

    mod_all = _ada_mod(c, w_ada, b_ada).reshape(depth, batch, 9, d)
    x2 = x.reshape(batch * seq, d)
    fg = final_g.reshape(1, d)
    for l in range(depth):
        mod = mod_all[l]
        x2 = _ffn(x2, mod, norm_g[l, 0:1], fg, w13[l, 0], w2[l, 0], sub=0, seq=seq, final=False)
        qt, nvt, gt, fqt, fvt, mqt, mvt, cmp_in, nks, nkw, fk, mk, gm = _inproj(
            x2, mod, norm_g[l, 1:2], positions, inv_col, fb_lane[l], w_in_p[l], w_in_t[l],
            batch=batch, seq=seq)
        cmp_kv, cmp_t = _compress(cmp_in.reshape(batch, 4, seq // CMP_STRIDE, CMP_STRIDE * HEAD_DIM),
                                  cpos[l], cw1[l], cw2[l], cw2t[l])
        o_nsa = _nsa(qt, cmp_kv, cmp_t, nks, nkw, nvt, gt, cst["ov"])
        o_fox = _fox(fqt, fk, fvt, gm, cst["tri"], cst["causal_fox"])
        o_moba = _moba(mqt, mk, mvt, cst["avg"], cst["causal_moba"])
        x2 = _mix_ffn(x2, mod, norm_g[l, 2:3], fg, o_nsa.reshape(batch * seq, -1),
                      o_fox.reshape(batch * seq, -1), o_moba.reshape(batch * seq, -1), w_o[l],
                      w13[l, 1], w2[l, 1], seq=seq, final=(l == depth - 1))
    return x2.reshape(batch, seq, d)
```

```python
import functools

import numpy as np
import jax
import jax.numpy as jnp
from jax import lax
from jax.experimental import pallas as pl
from jax.experimental.pallas import tpu as pltpu

D_MODEL = 1024
HEAD_DIM = 64
NSA_HEADS = 8
NSA_GROUPS = 2
NSA_REP = NSA_HEADS // NSA_GROUPS
FOX_HEADS = 4
MOBA_HEADS = 4
ROPE_DIM = HEAD_DIM // 4
ROPE_THETA = 500000.0
CMP_LEN = 32
CMP_STRIDE = 16
CMP_HIDDEN = 4 * HEAD_DIM
SLC_LEN = 64
SLC_TOPN = 8
WIN = 512
MOBA_BLOCK = 256
MOBA_TOPK = 3
D_FF = 2816
EPS = 1e-6
ATT_SCALE = HEAD_DIM ** -0.5

FOX_TILE = 512
FOX_CHUNK = 512
MOBA_TILE = 2 * MOBA_BLOCK
LANES = 128
MXU_TILE = 256
NEG = -1e30
VMEM_LIMIT = 56 * 1024 * 1024

F32 = jnp.float32
BF16 = jnp.bfloat16
HIGHEST = lax.Precision.HIGHEST

_KV = NSA_GROUPS * HEAD_DIM
_OFF_NSA_Q = 0
_OFF_KC = _OFF_NSA_Q + NSA_HEADS * HEAD_DIM
_OFF_VC = _OFF_KC + _KV
_OFF_KS = _OFF_VC + _KV
_OFF_VS = _OFF_KS + _KV
_OFF_KW = _OFF_VS + _KV
_OFF_VW = _OFF_KW + _KV
_OFF_GATE = _OFF_VW + _KV
_OFF_FOX_Q = _OFF_GATE + 3 * NSA_HEADS
_OFF_FOX_K = _OFF_FOX_Q + FOX_HEADS * HEAD_DIM
_OFF_FOX_V = _OFF_FOX_K + FOX_HEADS * HEAD_DIM
_OFF_FOX_F = _OFF_FOX_V + FOX_HEADS * HEAD_DIM
_OFF_MOBA_Q = _OFF_FOX_F + FOX_HEADS
_OFF_MOBA_K = _OFF_MOBA_Q + MOBA_HEADS * HEAD_DIM
_OFF_MOBA_V = _OFF_MOBA_K + MOBA_HEADS * HEAD_DIM
IN_COLS = _OFF_MOBA_V + MOBA_HEADS * HEAD_DIM

N_HEAD_COLS = 4 * _KV + (FOX_HEADS + MOBA_HEADS) * HEAD_DIM
GATE_LANES = 3 * NSA_REP
PROJ_COLS = N_HEAD_COLS + LANES
GATE_ROWS = 16
SUBLANES = 8
_F_SLAB = GATE_LANES // SUBLANES * SUBLANES
_F_ROW0 = GATE_LANES - _F_SLAB
_T_NSA_Q = 0
_T_NSA_V = _T_NSA_Q + NSA_HEADS * HEAD_DIM
_T_GATE = _T_NSA_V + 4 * HEAD_DIM
_T_FOX_Q = _T_GATE + LANES
_T_FOX_V = _T_FOX_Q + FOX_HEADS * HEAD_DIM
_T_MOBA_Q = _T_FOX_V + FOX_HEADS * HEAD_DIM
_T_MOBA_V = _T_MOBA_Q + MOBA_HEADS * HEAD_DIM
T_ROWS = _T_MOBA_V + MOBA_HEADS * HEAD_DIM


def _dot(a, b):
    return jnp.dot(a, b, preferred_element_type=F32)


def _dot_nt(a, b, precision=None):
    return lax.dot_general(a, b, (((1,), (1,)), ((), ())), precision=precision,
                           preferred_element_type=F32)


def _iota(shape, dim):
    return lax.broadcasted_iota(jnp.int32, shape, dim)


def _params(*sem):
    return pltpu.CompilerParams(dimension_semantics=sem, vmem_limit_bytes=VMEM_LIMIT)


def _ada_norm(x, g, shift, scale):
    ms = jnp.mean(x * x, axis=-1, keepdims=True)
    y = x * lax.rsqrt(ms + EPS) * g
    return y * (1.0 + scale) + shift


def _ada_kernel(c_ref, w_ref, b_ref, o_ref):
    c = c_ref[...]
    ca = c * jax.nn.sigmoid(c)
    o_ref[0] = jnp.dot(ca, w_ref[0], precision=HIGHEST, preferred_element_type=F32) + b_ref[0]


def _ada_mod(c, w_ada, b_ada):
    depth, d, n = w_ada.shape
    b = c.shape[0]
    tn = n // 4
    return pl.pallas_call(
        _ada_kernel,
        grid=(depth, n // tn),
        in_specs=[pl.BlockSpec((b, d), lambda l, j: (0, 0)),
                  pl.BlockSpec((1, d, tn), lambda l, j: (l, 0, j)),
                  pl.BlockSpec((1, 1, tn), lambda l, j: (l, 0, j))],
        out_specs=pl.BlockSpec((1, b, tn), lambda l, j: (l, 0, j)),
        out_shape=jax.ShapeDtypeStruct((depth, b, n), F32),
        compiler_params=_params("parallel", "parallel"),
        name="ada_mod",
    )(c, w_ada, b_ada.reshape(depth, 1, n))


def _swiglu_residual(x, mod_ref, g_ref, fg_ref, w13_ref, w2_ref, *, sub, coef, final, splits):
    h = _ada_norm(x, g_ref[...], mod_ref[0, 3 * sub:3 * sub + 1, :],
                  mod_ref[0, 3 * sub + 1:3 * sub + 2, :]).astype(BF16)
    acc = None
    for lo, hi in splits:
        a = _dot(h, w13_ref[:, lo:hi])
        b = _dot(h, w13_ref[:, D_FF + lo:D_FF + hi])
        u = (a * jax.nn.sigmoid(a) * b).astype(BF16)
        part = _dot(u, w2_ref[lo:hi, :])
        acc = part if acc is None else acc + part
    y = x + (coef * mod_ref[0, 3 * sub + 2:3 * sub + 3, :]) * acc
    if final:
        y = y * lax.rsqrt(jnp.mean(y * y, axis=-1, keepdims=True) + EPS) * fg_ref[...]
    return y


def _ffn_kernel(x_ref, mod_ref, g_ref, fg_ref, w13_ref, w2_ref, o_ref, **kw):
    o_ref[...] = _swiglu_residual(x_ref[...], mod_ref, g_ref, fg_ref, w13_ref, w2_ref, **kw)


def _mix_ffn_kernel(x_ref, mod_ref, g_ref, fg_ref, on_ref, of_ref, om_ref, wo_ref, w13_ref, w2_ref, o_ref, **kw):
    n_w = NSA_HEADS * HEAD_DIM
    f_w = FOX_HEADS * HEAD_DIM
    mix = _dot(on_ref[...], wo_ref[0:n_w, :])
    mix += _dot(of_ref[...], wo_ref[n_w:n_w + f_w, :])
    mix += _dot(om_ref[...], wo_ref[n_w + f_w:, :])
    x = x_ref[...] + mod_ref[0, 5:6, :] * mix
    o_ref[...] = _swiglu_residual(x, mod_ref, g_ref, fg_ref, w13_ref, w2_ref, **kw)


def _resident(shape):
    return pl.BlockSpec(shape, lambda i: (0,) * len(shape), pipeline_mode=pl.Buffered(1))


def _hidden_splits(chunk):
    return tuple((lo, min(lo + chunk, D_FF)) for lo in range(0, D_FF, chunk))


def _ffn(x2, mod, g, final_g, w13, w2, *, sub, seq, final, tm=1024, chunk=3 * MXU_TILE):
    n, d = x2.shape
    per_b = seq // tm
    return pl.pallas_call(
        functools.partial(_ffn_kernel, sub=sub, coef=0.5, final=final, splits=_hidden_splits(chunk)),
        grid=(n // tm,),
        in_specs=[pl.BlockSpec((tm, d), lambda i: (i, 0)),
                  pl.BlockSpec((1, 9, d), lambda i: (i // per_b, 0, 0)),
                  pl.BlockSpec((1, d), lambda i: (0, 0)),
                  pl.BlockSpec((1, d), lambda i: (0, 0)),
                  _resident((d, 2 * D_FF)), _resident((D_FF, d))],
        out_specs=pl.BlockSpec((tm, d), lambda i: (i, 0)),
        out_shape=jax.ShapeDtypeStruct((n, d), F32),
        compiler_params=_params("parallel"),
        name="ffn",
    )(x2, mod, g, final_g, w13, w2)


def _mix_ffn(x2, mod, g, final_g, o_nsa, o_fox, o_moba, w_out, w13, w2, *, seq, final, tm=512,
             chunk=4 * MXU_TILE):
    n, d = x2.shape
    per_b = seq // tm
    row = lambda width: pl.BlockSpec((tm, width), lambda i: (i, 0))
    return pl.pallas_call(
        functools.partial(_mix_ffn_kernel, sub=2, coef=0.5, final=final, splits=_hidden_splits(chunk)),
        grid=(n // tm,),
        in_specs=[row(d),
                  pl.BlockSpec((1, 9, d), lambda i: (i // per_b, 0, 0)),
                  pl.BlockSpec((1, d), lambda i: (0, 0)),
                  pl.BlockSpec((1, d), lambda i: (0, 0)),
                  row(o_nsa.shape[1]), row(o_fox.shape[1]), row(o_moba.shape[1]),
                  _resident((d, d)), _resident((d, 2 * D_FF)), _resident((D_FF, d))],
        out_specs=row(d),
        out_shape=jax.ShapeDtypeStruct((n, d), F32),
        compiler_params=_params("parallel"),
        name="mix_ffn",
    )(x2, mod, g, final_g, o_nsa, o_fox, o_moba, w_out, w13, w2)


def _inproj_kernel(x_ref, mod_ref, g_ref, posr_ref, invc_ref, fb_ref, w_ref, wt_ref,
                   qt_ref, nvt_ref, gt_ref, fqt_ref, fvt_ref, mqt_ref, mvt_ref,
                   cmp_ref, nks_ref, nkw_ref, fk_ref, mk_ref, gm_ref, *, per_b):
    h = _ada_norm(x_ref[...], g_ref[...], mod_ref[0, 3:4, :], mod_ref[0, 4:5, :]).astype(BF16)
    tm = h.shape[0]
    seq_pos = (pl.program_id(0) % per_b) * tm + _iota((tm, LANES), 0)

    half = ROPE_DIM // 2
    ang_t = invc_ref[...] * posr_ref[0].astype(F32)
    cos_t = jnp.cos(ang_t)
    sin_t = jnp.sin(ang_t)
    ones_row = jnp.where(_iota((HEAD_DIM, tm), 0) == 0, 1.0, 0.0)
    slab_rows = 4 * HEAD_DIM

    def queries_t(row0, n_heads, out_ref, rotate):
        for s_idx in range(n_heads // 4):
            y_t = _dot_nt(wt_ref[row0 + slab_rows * s_idx:row0 + slab_rows * (s_idx + 1), :], h)
            for hh in range(4):
                blk = y_t[HEAD_DIM * hh:HEAD_DIM * (hh + 1)]
                if rotate:
                    r1, r2 = blk[:half], blk[half:2 * half]
                    blk = jnp.concatenate([r1 * cos_t - r2 * sin_t, r2 * cos_t + r1 * sin_t, blk[2 * half:]],
                                          axis=0)
                out_ref[0, 4 * s_idx + hh] = (blk * ATT_SCALE).astype(BF16)

    def values_t(row0, out_ref):
        y_t = _dot_nt(wt_ref[row0:row0 + slab_rows, :], h)
        for idx in range(4):
            out_ref[0, idx] = jnp.concatenate([y_t[HEAD_DIM * idx:HEAD_DIM * (idx + 1)], ones_row],
                                              axis=0).astype(BF16)

    queries_t(_T_NSA_Q, NSA_HEADS, qt_ref, True)
    values_t(_T_NSA_V, nvt_ref)
    sig_t = jax.nn.sigmoid(_dot_nt(wt_ref[_T_GATE:_T_GATE + LANES, :], h))
    for g in range(NSA_GROUPS):
        gt_ref[0, g] = sig_t[GATE_ROWS * g:GATE_ROWS * (g + 1)]
    queries_t(_T_FOX_Q, FOX_HEADS, fqt_ref, False)
    values_t(_T_FOX_V, fvt_ref)
    queries_t(_T_MOBA_Q, MOBA_HEADS, mqt_ref, True)
    values_t(_T_MOBA_V, mvt_ref)

    zeros_t = jnp.zeros((half, tm), F32)

    def lane_table(first, second, fill):
        head = jnp.concatenate([first, second, jnp.full((HEAD_DIM - ROPE_DIM, tm), fill, F32)], axis=0)
        return jnp.concatenate([head, head], axis=0).T

    cosl = lane_table(cos_t, cos_t, 1.0)
    sin_lo = lane_table(-sin_t, zeros_t, 0.0)
    sin_hi = lane_table(zeros_t, sin_t, 0.0)
    lane = _iota((tm, LANES), 1)

    def rope(y):
        return (y * cosl + pltpu.roll(y, LANES - ROPE_DIM // 2, 1) * sin_lo
                + pltpu.roll(y, ROPE_DIM // 2, 1) * sin_hi)

    def put_heads(z, out_ref, head0, dtype):
        z = z.astype(dtype)
        out_ref[0, head0] = z[:, :HEAD_DIM]
        out_ref[0, head0 + 1] = z[:, HEAD_DIM:]

    def put_wide(z, out_ref, head0, tail):
        out_ref[0, head0] = jnp.where(lane < HEAD_DIM, z, tail).astype(BF16)
        out_ref[0, head0 + 1] = jnp.where(lane < HEAD_DIM, pltpu.roll(z, HEAD_DIM, 1), tail).astype(BF16)

    def put_keys(z, out_ref, head0, block_len):
        block = lax.shift_right_logical(seq_pos, block_len.bit_length() - 1)
        put_wide(z, out_ref, head0, jnp.where(lane - HEAD_DIM == block, 1.0, 0.0))

    def slab(idx):
        y = _dot(h, w_ref[:, idx * 2 * LANES:(idx + 1) * 2 * LANES])
        return y[:, :LANES], y[:, LANES:]

    a, b = slab(0)
    put_heads(rope(a), cmp_ref, 0, F32)
    put_heads(b, cmp_ref, 2, F32)
    a, b = slab(1)
    put_keys(rope(a), nks_ref, 0, SLC_LEN)
    put_heads(rope(b), nkw_ref, 0, BF16)
    a, b = slab(2)
    put_heads(a, fk_ref, 0, BF16)
    put_heads(b, fk_ref, 2, BF16)
    a, b = slab(3)
    put_keys(rope(a), mk_ref, 0, MOBA_BLOCK)
    put_keys(rope(b), mk_ref, 2, MOBA_BLOCK)
    f = _dot(h, w_ref[:, N_HEAD_COLS:])
    gm_ref[0] = jax.nn.log_sigmoid(f + fb_ref[...])


def _inproj(x2, mod, g, positions, inv_col, fb_lane, w_p, w_t, *, batch, seq, tm=512):
    n, d = x2.shape
    per_b = seq // tm

    def hm(nh, dtype, width=HEAD_DIM):
        return (pl.BlockSpec((1, nh, tm, width), lambda i: (i // per_b, 0, i % per_b, 0)),
                jax.ShapeDtypeStruct((batch, nh, seq, width), dtype))

    def tr(nh, rows, dtype):
        return (pl.BlockSpec((1, nh, rows, tm), lambda i: (i // per_b, 0, 0, i % per_b)),
                jax.ShapeDtypeStruct((batch, nh, rows, seq), dtype))

    specs = [tr(NSA_HEADS, HEAD_DIM, BF16), tr(4, LANES, BF16), tr(NSA_GROUPS, GATE_ROWS, F32),
             tr(FOX_HEADS, HEAD_DIM, BF16), tr(FOX_HEADS, LANES, BF16),
             tr(MOBA_HEADS, HEAD_DIM, BF16), tr(MOBA_HEADS, LANES, BF16),
             hm(4, F32), hm(2, BF16, LANES), hm(2, BF16), hm(FOX_HEADS, BF16), hm(MOBA_HEADS, BF16, LANES),
             (pl.BlockSpec((1, tm, LANES), lambda i: (i // per_b, i % per_b, 0)),
              jax.ShapeDtypeStruct((batch, seq, LANES), F32))]
    const = lambda shape: pl.BlockSpec(shape, lambda i: (0,) * len(shape))
    return pl.pallas_call(
        functools.partial(_inproj_kernel, per_b=per_b),
        grid=(n // tm,),
        in_specs=[pl.BlockSpec((tm, d), lambda i: (i, 0)),
                  pl.BlockSpec((1, 9, d), lambda i: (i // per_b, 0, 0)),
                  const((1, d)),
                  pl.BlockSpec((1, 1, tm), lambda i: (i, 0, 0)),
                  const((ROPE_DIM // 2, 1)), const((1, LANES)),
                  _resident(w_p.shape), _resident(w_t.shape)],
        out_specs=[s for s, _ in specs],
        out_shape=[s for _, s in specs],
        compiler_params=_params("parallel"),
        name="inproj",
    )(x2, mod, g, positions.reshape(n // tm, 1, tm), inv_col, fb_lane, w_p, w_t)


def _compress_kernel(z_ref, pos_ref, w1_ref, w2_ref, w2t_ref, o_ref, ot_ref):
    z = z_ref[0, 0]
    top = _dot((z + pos_ref[0, 0]).astype(BF16), w1_ref[0, 0])
    bot = _dot((z + pos_ref[0, 1]).astype(BF16), w1_ref[0, 1])
    nch = z.shape[0]
    hid = top + pltpu.roll(bot, nch - 1, 0)
    act = jax.nn.gelu(hid).astype(BF16)
    o_ref[0, 0] = _dot(act, w2_ref[0]).astype(o_ref.dtype)
    ot_ref[0, 0] = _dot_nt(w2t_ref[0], act).astype(ot_ref.dtype)


def _compress(cmp_in, pos_l, w1_l, w2_l, w2t_l):
    b, four, nch, width = cmp_in.shape
    return pl.pallas_call(
        _compress_kernel,
        grid=(b, four),
        in_specs=[pl.BlockSpec((1, 1, nch, width), lambda i, n: (i, n, 0, 0)),
                  pl.BlockSpec((1, 2, 1, width), lambda i, n: (n // 2, 0, 0, 0)),
                  pl.BlockSpec((1, 2, width, CMP_HIDDEN), lambda i, n: (n // 2, 0, 0, 0)),
                  pl.BlockSpec((1, CMP_HIDDEN, HEAD_DIM), lambda i, n: (n // 2, 0, 0)),
                  pl.BlockSpec((1, HEAD_DIM, CMP_HIDDEN), lambda i, n: (n // 2, 0, 0))],
        out_specs=[pl.BlockSpec((1, 1, nch, HEAD_DIM), lambda i, n: (i, n, 0, 0)),
                   pl.BlockSpec((1, 1, HEAD_DIM, nch), lambda i, n: (i, n, 0, 0))],
        out_shape=[jax.ShapeDtypeStruct((b, four, nch, HEAD_DIM), BF16),
                   jax.ShapeDtypeStruct((b, four, HEAD_DIM, nch), BF16)],
        compiler_params=_params("parallel", "parallel"),
        name="nsa_compress",
    )(cmp_in, pos_l, w1_l, w2_l, w2t_l)


def _chunk_update_t(carry, s, v_t):
    m, acc = carry
    m_new = jnp.maximum(m, jnp.max(s, axis=0, keepdims=True))
    p = jnp.exp(s - m_new).astype(BF16)
    return m_new, jnp.exp(m - m_new) * acc + _dot(v_t, p)


def _chunk_init_t(n):
    return jnp.full((1, n), NEG, F32), jnp.zeros((LANES, n), F32)


def _chunk_finish_t(carry):
    _, acc = carry
    return acc[:HEAD_DIM] / jnp.maximum(acc[HEAD_DIM:HEAD_DIM + 1], 1e-30)


def _heads_to_rows(heads_t):
    pairs = [jnp.concatenate(heads_t[a:a + 2], axis=0).T for a in range(0, len(heads_t), 2)]
    return jnp.concatenate(pairs, axis=1)


def _nsa_kernel(qt_ref, kc_ref, vct_ref, ks_ref, vst_ref, kw_ref, vwt_ref, gt_ref, ovt_ref, o_ref, *, tq, tk):
    seq = ks_ref.shape[2]
    for i in range(seq // tq):
        q_cols = slice(i * tq, (i + 1) * tq)
        _nsa_tile(i, qt_ref.at[:, :, :, q_cols], kc_ref, vct_ref, ks_ref, vst_ref, kw_ref, vwt_ref,
                  gt_ref.at[:, :, :, q_cols], ovt_ref, o_ref.at[:, q_cols, :], tq=tq, tk=tk)


def _nsa_tile(i, qt_ref, kc_ref, vct_ref, ks_ref, vst_ref, kw_ref, vwt_ref, gt_ref, ovt_ref, o_ref, *, tq, tk):
    rep = NSA_REP
    groups = range(NSA_GROUPS)
    n_c = kc_ref.shape[2]
    n_s = ks_ref.shape[2] // SLC_LEN
    tile = lambda a: jnp.concatenate([a] * rep, axis=1)
    t_q = i * tq + _iota((1, tq), 1)
    cend = _iota((n_c, 1), 0) * CMP_STRIDE + (CMP_LEN - 1)
    cm = cend <= tile(t_q)
    j_t = _iota((n_s, tq), 0)
    tb = lax.shift_right_logical(t_q, SLC_LEN.bit_length() - 1)
    valid = j_t <= tb
    forced = (j_t == 0) | (j_t == tb) | (j_t == tb - 1)
    start = max(i * tq - WIN, 0)
    span = (i + 1) * tq - start
    dist = t_q - (start + _iota((span, 1), 0))
    band = tile(jnp.where((dist >= 0) & (dist < WIN), 0.0, NEG))

    o_cmp, o_win, q4a = [], [], []
    for g in groups:
        q4 = jnp.concatenate([qt_ref[0, rep * g + r] for r in range(rep)], axis=1)
        s = jnp.where(cm, _dot(kc_ref[0, g], q4), NEG)
        e = jnp.where(cm, jnp.exp(s - jnp.max(s, axis=0, keepdims=True)), 0.0)
        p = e / jnp.maximum(jnp.sum(e, axis=0, keepdims=True), 1e-30)
        o_cmp.append(_dot(vct_ref[0, g], p.astype(BF16)))
        psum = p[:, 0:tq] + p[:, tq:2 * tq] + p[:, 2 * tq:3 * tq] + p[:, 3 * tq:4 * tq]
        imp = jnp.dot(ovt_ref[...], psum, precision=HIGHEST, preferred_element_type=F32)[:n_s]
        v_t = jnp.where(valid, jnp.where(forced, jnp.inf, imp), -jnp.inf)
        rank = jnp.zeros((n_s, tq), jnp.int32)
        for jp in range(n_s):
            row = v_t[jp:jp + 1, :]
            ahead = (row > v_t) | ((row == v_t) & (j_t > jp))
            rank = rank + ahead.astype(jnp.int32)
        neg_t = jnp.where((rank < SLC_TOPN) & (v_t > -jnp.inf), 0.0, NEG)
        neg = jnp.concatenate([neg_t, jnp.zeros((HEAD_DIM - n_s, tq), F32)], axis=0)
        q4a.append(jnp.concatenate([q4, tile(neg).astype(BF16)], axis=0))
        sw = _dot(kw_ref[0, g, start:start + span, :], q4) + band
        pw = jnp.exp(sw - jnp.max(sw, axis=0, keepdims=True)).astype(BF16)
        aw = _dot(vwt_ref[0, g, :, start:start + span], pw)
        o_win.append(aw[:HEAD_DIM] / jnp.maximum(aw[HEAD_DIM:HEAD_DIM + 1], 1e-30))

    def slc_step(k0, k1, carries, diag):
        new = []
        for g in groups:
            sc = _dot(ks_ref[0, g, k0:k1, :], q4a[g])
            if diag:
                sc = sc + tile(jnp.where(k0 + _iota((k1 - k0, 1), 0) <= t_q, 0.0, NEG))
            new.append(_chunk_update_t(carries[g], sc, vst_ref[0, g, :, k0:k1]))
        return tuple(new)

    end = (i + 1) * tq
    last = (end - 1) // tk * tk
    carries = slc_step(last, end, tuple(_chunk_init_t(rep * tq) for _ in groups), True)
    for k0 in range(0, last, tk):
        carries = slc_step(k0, k0 + tk, carries, False)

    heads = []
    for g in groups:
        o_slc = _chunk_finish_t(carries[g])
        gates = gt_ref[0, g]
        for r in range(rep):
            cols = slice(r * tq, (r + 1) * tq)
            heads.append(gates[3 * r:3 * r + 1] * o_cmp[g][:, cols]
                         + gates[3 * r + 1:3 * r + 2] * o_slc[:, cols]
                         + gates[3 * r + 2:3 * r + 3] * o_win[g][:, cols])
    o_ref[0] = _heads_to_rows(heads).astype(o_ref.dtype)


def _nsa(qt, cmp_kv, cmp_t, nks, nkw, nvt, gt, ovt, *, tq=256, tk=512):
    b, _, _, seq = qt.shape
    n_c = cmp_kv.shape[2]
    ng = NSA_GROUPS
    return pl.pallas_call(
        functools.partial(_nsa_kernel, tq=tq, tk=tk),
        grid=(b,),
        in_specs=[pl.BlockSpec((1, NSA_HEADS, HEAD_DIM, seq), lambda bi: (bi, 0, 0, 0)),
                  pl.BlockSpec((1, ng, n_c, HEAD_DIM), lambda bi: (bi, 0, 0, 0)),
                  pl.BlockSpec((1, ng, HEAD_DIM, n_c), lambda bi: (bi, 1, 0, 0)),
                  pl.BlockSpec((1, ng, seq, LANES), lambda bi: (bi, 0, 0, 0)),
                  pl.BlockSpec((1, ng, LANES, seq), lambda bi: (bi, 0, 0, 0)),
                  pl.BlockSpec((1, ng, seq, HEAD_DIM), lambda bi: (bi, 0, 0, 0)),
                  pl.BlockSpec((1, ng, LANES, seq), lambda bi: (bi, 1, 0, 0)),
                  pl.BlockSpec((1, ng, GATE_ROWS, seq), lambda bi: (bi, 0, 0, 0)),
                  pl.BlockSpec((LANES, n_c), lambda bi: (0, 0))],
        out_specs=pl.BlockSpec((1, seq, NSA_HEADS * HEAD_DIM), lambda bi: (bi, 0, 0)),
        out_shape=jax.ShapeDtypeStruct((b, seq, NSA_HEADS * HEAD_DIM), BF16),
        compiler_params=_params("parallel"),
        name="nsa",
    )(qt, cmp_kv, cmp_t, nks, nvt, nkw, nvt, gt, ovt)


def _fox_kernel(qt_ref, k_ref, vt_ref, lf_ref, cum_ref, tri_ref, o_ref, fcol_ref, frow_ref, *, tq):
    heads = range(FOX_HEADS)
    seq = lf_ref.shape[1]
    blk = cum_ref.shape[0]
    carry = jnp.zeros((1, LANES), F32)
    for c in range(seq // blk):
        rows = slice(c * blk, (c + 1) * blk)
        cs = jnp.dot(cum_ref[...], lf_ref[0, rows, :], precision=HIGHEST,
                     preferred_element_type=F32) + carry
        fcol_ref[rows, :] = cs
        carry = cs[blk - 1:blk, :]
    frow_ref[...] = fcol_ref[...].T[_F_SLAB:_F_SLAB + SUBLANES, :]

    tk = tri_ref.shape[1]
    per_chunk = tk // tq
    for i in range(seq // tq):
        q_cols = slice(i * tq, (i + 1) * tq)
        qs = [qt_ref[0, h, :, q_cols] for h in heads]
        fqs = [frow_ref[_F_ROW0 + h:_F_ROW0 + h + 1, q_cols] for h in heads]

        def step(c, carries, diag, qs=qs, fqs=fqs, i=i):
            keys = slice(c * tk, (c + 1) * tk)
            new = []
            for h in heads:
                fk = fcol_ref[keys, GATE_LANES + h:GATE_LANES + h + 1]
                sc = _dot(k_ref[0, h, keys, :], qs[h]) + (fqs[h] - fk)
                if diag:
                    sc = sc + tri_ref[i % per_chunk]
                new.append(_chunk_update_t(carries[h], sc, vt_ref[0, h, :, keys]))
            return tuple(new)

        carries = step(i // per_chunk, tuple(_chunk_init_t(tq) for _ in heads), True)
        for c in range(i // per_chunk):
            carries = step(c, carries, False)
        o_ref[0, q_cols, :] = _heads_to_rows([_chunk_finish_t(cr) for cr in carries]).astype(o_ref.dtype)


def _fox(fqt, fk, fvt, gm, cum, tri_t, *, tq=FOX_TILE):
    b, nh, _, seq = fqt.shape
    blk = cum.shape[0]
    return pl.pallas_call(
        functools.partial(_fox_kernel, tq=tq),
        grid=(b,),
        in_specs=[pl.BlockSpec((1, nh, HEAD_DIM, seq), lambda bi: (bi, 0, 0, 0)),
                  pl.BlockSpec((1, nh, seq, HEAD_DIM), lambda bi: (bi, 0, 0, 0)),
                  pl.BlockSpec((1, nh, LANES, seq), lambda bi: (bi, 0, 0, 0)),
                  pl.BlockSpec((1, seq, LANES), lambda bi: (bi, 0, 0)),
                  pl.BlockSpec((blk, blk), lambda bi: (0, 0)),
                  pl.BlockSpec(tri_t.shape, lambda bi: (0, 0, 0))],
        out_specs=pl.BlockSpec((1, seq, nh * HEAD_DIM), lambda bi: (bi, 0, 0)),
        out_shape=jax.ShapeDtypeStruct((b, seq, nh * HEAD_DIM), BF16),
        scratch_shapes=[pltpu.VMEM((seq, LANES), F32), pltpu.VMEM((8, seq), F32)],
        compiler_params=_params("parallel"),
        name="fox",
    )(fqt, fk, fvt, gm, cum, tri_t)


def _moba_kernel(qt_ref, k_ref, vt_ref, avg_ref, tri_ref, o_ref, *, tq):
    heads = range(MOBA_HEADS)
    nb = avg_ref.shape[0]
    seq = k_ref.shape[2]
    kms = [_dot(avg_ref[...], k_ref[0, h])[:, :HEAD_DIM] for h in heads]
    blk = _iota((nb, tq), 0)
    in_tile = lax.shift_right_logical(_iota((nb, tq), 1), MOBA_BLOCK.bit_length() - 1)

    for i in range(seq // tq):
        q_cols = slice(i * tq, (i + 1) * tq)
        own = i * (tq // MOBA_BLOCK) + in_tile
        qas = []
        for h in heads:
            q = qt_ref[0, h, :, q_cols]
            gate = jnp.dot(kms[h], q.astype(F32), precision=HIGHEST, preferred_element_type=F32)
            val = jnp.where(blk < own, gate, -jnp.inf)
            rank = jnp.zeros((nb, tq), jnp.int32)
            for jp in range(nb):
                row = val[jp:jp + 1, :]
                ahead = (row > val) | ((row == val) & (blk > jp))
                rank = rank + ahead.astype(jnp.int32)
            keep = ((rank < MOBA_TOPK) & (blk < own)) | (blk == own)
            neg = jnp.concatenate([jnp.where(keep, 0.0, NEG), jnp.zeros((HEAD_DIM - nb, tq), F32)], axis=0)
            qas.append(jnp.concatenate([q, neg.astype(BF16)], axis=0))

        def step(c, carries, diag, qas=qas):
            keys = slice(c * tq, (c + 1) * tq)
            new = []
            for h in heads:
                sc = _dot(k_ref[0, h, keys, :], qas[h])
                if diag:
                    sc = sc + tri_ref[...]
                new.append(_chunk_update_t(carries[h], sc, vt_ref[0, h, :, keys]))
            return tuple(new)

        carries = step(i, tuple(_chunk_init_t(tq) for _ in heads), True)
        for c in range(i):
            carries = step(c, carries, False)
        o_ref[0, q_cols, :] = _heads_to_rows([_chunk_finish_t(cr) for cr in carries]).astype(o_ref.dtype)


def _moba(mqt, mk, mvt, avg, tri_t, *, tq=MOBA_TILE):
    b, nh, _, seq = mqt.shape
    nb = avg.shape[0]
    return pl.pallas_call(
        functools.partial(_moba_kernel, tq=tq),
        grid=(b,),
        in_specs=[pl.BlockSpec((1, nh, HEAD_DIM, seq), lambda bi: (bi, 0, 0, 0)),
                  pl.BlockSpec((1, nh, seq, LANES), lambda bi: (bi, 0, 0, 0)),
                  pl.BlockSpec((1, nh, LANES, seq), lambda bi: (bi, 0, 0, 0)),
                  pl.BlockSpec((nb, seq), lambda bi: (0, 0)),
                  pl.BlockSpec((tq, tq), lambda bi: (0, 0))],
        out_specs=pl.BlockSpec((1, seq, nh * HEAD_DIM), lambda bi: (bi, 0, 0)),
        out_shape=jax.ShapeDtypeStruct((b, seq, nh * HEAD_DIM), BF16),
        compiler_params=_params("parallel"),
        name="moba",
    )(mqt, mk, mvt, avg, tri_t)


def _pack_w_in(w_in):
    depth, d, _ = w_in.shape

    def cols(a, b):
        return w_in[:, :, a:b]

    zeros = lambda n: jnp.zeros((depth, d, n), w_in.dtype)
    g_half = 3 * NSA_REP
    token_major = [cols(_OFF_KC, _OFF_KS),
                   cols(_OFF_KS, _OFF_VS), cols(_OFF_KW, _OFF_VW),
                   cols(_OFF_FOX_K, _OFF_FOX_V),
                   cols(_OFF_MOBA_K, _OFF_MOBA_V),
                   zeros(GATE_LANES), cols(_OFF_FOX_F, _OFF_FOX_F + FOX_HEADS),
                   zeros(LANES - GATE_LANES - FOX_HEADS)]
    transposed = [cols(_OFF_NSA_Q, _OFF_KC),
                  cols(_OFF_VS, _OFF_KW), cols(_OFF_VW, _OFF_GATE),
                  cols(_OFF_GATE, _OFF_GATE + g_half), zeros(GATE_ROWS - g_half),
                  cols(_OFF_GATE + g_half, _OFF_GATE + 2 * g_half), zeros(LANES - GATE_ROWS - g_half),
                  cols(_OFF_FOX_Q, _OFF_FOX_K), cols(_OFF_FOX_V, _OFF_FOX_F),
                  cols(_OFF_MOBA_Q, _OFF_MOBA_K), cols(_OFF_MOBA_V, IN_COLS)]
    w_p = jnp.concatenate(token_major, axis=-1).astype(BF16)
    w_t = jnp.swapaxes(jnp.concatenate(transposed, axis=-1), 1, 2).astype(BF16)
    return w_p, w_t


def _constants(seq):
    n_c = seq // CMP_STRIDE
    n_s = seq // SLC_LEN
    c0 = np.arange(n_c)[None, :] * CMP_STRIDE
    j0 = np.arange(LANES)[:, None] * SLC_LEN
    real = (np.arange(n_c)[None, :] < (seq - CMP_LEN) // CMP_STRIDE + 1) & (np.arange(LANES)[:, None] < n_s)
    ov = ((c0 < j0 + SLC_LEN) & (c0 + CMP_LEN > j0) & real).astype(np.float32)
    nb = seq // MOBA_BLOCK
    avg = (np.arange(nb)[:, None] == (np.arange(seq)[None, :] // MOBA_BLOCK)).astype(np.float32) / MOBA_BLOCK
    tri = np.tril(np.ones((256, 256), np.float32))

    kk = np.arange(FOX_CHUNK)[None, :, None]
    qq = np.arange(FOX_TILE)[None, None, :] + FOX_TILE * np.arange(FOX_CHUNK // FOX_TILE)[:, None, None]
    causal_fox = np.where(kk <= qq, 0.0, NEG).astype(np.float32)
    r = np.arange(MOBA_TILE)
    same = (r[:, None] // MOBA_BLOCK) == (r[None, :] // MOBA_BLOCK)
    allowed = np.where(same, r[:, None] <= r[None, :], r[:, None] < r[None, :])
    causal_moba = np.where(allowed, 0.0, NEG).astype(np.float32)
    return dict(ov=jnp.asarray(ov), avg=jnp.asarray(avg, dtype=BF16), tri=jnp.asarray(tri),
                causal_fox=jnp.asarray(causal_fox), causal_moba=jnp.asarray(causal_moba))


def kernel(x, c, positions, norm_g, w_ada, b_ada, w_in, fox_fbias, cmp_pos, cmp_w1, cmp_w2, w_out,
           ffn_w13, ffn_w2, final_g):
    batch, seq, d = x.shape
    depth = w_ada.shape[0]
    assert d == D_MODEL and seq % MOBA_BLOCK == 0 and seq % 1024 == 0

    w_in_p, w_in_t = _pack_w_in(w_in)
    w13 = ffn_w13.astype(BF16)
    w2 = ffn_w2.astype(BF16)
    w_o = w_out.astype(BF16)
    half = CMP_LEN * HEAD_DIM // 2
    cw1 = cmp_w1.astype(BF16).reshape(depth, 2, 2, half, CMP_HIDDEN)
    cw2 = cmp_w2.astype(BF16)
    cw2t = jnp.swapaxes(cw2, 2, 3)
    cpos = cmp_pos.reshape(depth, 2, 2, 1, half)
    cst = _constants(seq)

    inv = ROPE_THETA ** (-jnp.arange(0, ROPE_DIM, 2, dtype=F32) / ROPE_DIM)
    inv_col = inv.reshape(ROPE_DIM // 2, 1)
    fb_lane = jnp.zeros((depth, 1, LANES), F32).at[:, 0, GATE_LANES:GATE_LANES + FOX_HEADS].set(fox_fbias)

    mod_all = _ada_mod(c, w_ada, b_ada).reshape(depth, batch, 9, d)
    x2 = x.reshape(batch * seq, d)
    fg = final_g.reshape(1, d)
    for l in range(depth):
        mod = mod_all[l]
        x2 = _ffn(x2, mod, norm_g[l, 0:1], fg, w13[l, 0], w2[l, 0], sub=0, seq=seq, final=False)
        qt, nvt, gt, fqt, fvt, mqt, mvt, cmp_in, nks, nkw, fk, mk, gm = _inproj(
            x2, mod, norm_g[l, 1:2], positions, inv_col, fb_lane[l], w_in_p[l], w_in_t[l],
            batch=batch, seq=seq)
        cmp_kv, cmp_t = _compress(cmp_in.reshape(batch, 4, seq // CMP_STRIDE, CMP_STRIDE * HEAD_DIM),
                                  cpos[l], cw1[l], cw2[l], cw2t[l])
        o_nsa = _nsa(qt, cmp_kv, cmp_t, nks, nkw, nvt, gt, cst["ov"])
        o_fox = _fox(fqt, fk, fvt, gm, cst["tri"], cst["causal_fox"])
        o_moba = _moba(mqt, mk, mvt, cst["avg"], cst["causal_moba"])
        x2 = _mix_ffn(x2, mod, norm_g[l, 2:3], fg, o_nsa.reshape(batch * seq, -1),
                      o_fox.reshape(batch * seq, -1), o_moba.reshape(batch * seq, -1), w_o[l],
                      w13[l, 1], w2[l, 1], seq=seq, final=(l == depth - 1))
    return x2.reshape(batch, seq, d)
```

```python
import functools

import numpy as np
import jax
import jax.numpy as jnp
from jax import lax
from jax.experimental import pallas as pl
from jax.experimental.pallas import tpu as pltpu

D_MODEL = 1024
HEAD_DIM = 64
NSA_HEADS = 8
NSA_GROUPS = 2
NSA_REP = NSA_HEADS // NSA_GROUPS
FOX_HEADS = 4
MOBA_HEADS = 4
ROPE_DIM = HEAD_DIM // 4
ROPE_THETA = 500000.0
CMP_LEN = 32
CMP_STRIDE = 16
CMP_HIDDEN = 4 * HEAD_DIM
SLC_LEN = 64
SLC_TOPN = 8
WIN = 512
MOBA_BLOCK = 256
MOBA_TOPK = 3
D_FF = 2816
EPS = 1e-6
ATT_SCALE = HEAD_DIM ** -0.5

FOX_TILE = 512
FOX_CHUNK = 512
MOBA_TILE = 2 * MOBA_BLOCK
LANES = 128
MXU_TILE = 256
NEG = -1e30
VMEM_LIMIT = 56 * 1024 * 1024

F32 = jnp.float32
BF16 = jnp.bfloat16
HIGHEST = lax.Precision.HIGHEST

_KV = NSA_GROUPS * HEAD_DIM
_OFF_NSA_Q = 0
_OFF_KC = _OFF_NSA_Q + NSA_HEADS * HEAD_DIM
_OFF_VC = _OFF_KC + _KV
_OFF_KS = _OFF_VC + _KV
_OFF_VS = _OFF_KS + _KV
_OFF_KW = _OFF_VS + _KV
_OFF_VW = _OFF_KW + _KV
_OFF_GATE = _OFF_VW + _KV
_OFF_FOX_Q = _OFF_GATE + 3 * NSA_HEADS
_OFF_FOX_K = _OFF_FOX_Q + FOX_HEADS * HEAD_DIM
_OFF_FOX_V = _OFF_FOX_K + FOX_HEADS * HEAD_DIM
_OFF_FOX_F = _OFF_FOX_V + FOX_HEADS * HEAD_DIM
_OFF_MOBA_Q = _OFF_FOX_F + FOX_HEADS
_OFF_MOBA_K = _OFF_MOBA_Q + MOBA_HEADS * HEAD_DIM
_OFF_MOBA_V = _OFF_MOBA_K + MOBA_HEADS * HEAD_DIM
IN_COLS = _OFF_MOBA_V + MOBA_HEADS * HEAD_DIM

N_HEAD_COLS = 4 * _KV + (FOX_HEADS + MOBA_HEADS) * HEAD_DIM
GATE_LANES = 3 * NSA_REP
PROJ_COLS = N_HEAD_COLS + LANES
GATE_ROWS = 16
SUBLANES = 8
_F_SLAB = GATE_LANES // SUBLANES * SUBLANES
_F_ROW0 = GATE_LANES - _F_SLAB
_T_NSA_Q = 0
_T_NSA_V = _T_NSA_Q + NSA_HEADS * HEAD_DIM
_T_GATE = _T_NSA_V + 4 * HEAD_DIM
_T_FOX_Q = _T_GATE + LANES
_T_FOX_V = _T_FOX_Q + FOX_HEADS * HEAD_DIM
_T_MOBA_Q = _T_FOX_V + FOX_HEADS * HEAD_DIM
_T_MOBA_V = _T_MOBA_Q + MOBA_HEADS * HEAD_DIM
T_ROWS = _T_MOBA_V + MOBA_HEADS * HEAD_DIM


def _dot(a, b):
    return jnp.dot(a, b, preferred_element_type=F32)


def _dot_nt(a, b, precision=None):
    return lax.dot_general(a, b, (((1,), (1,)), ((), ())), precision=precision,
                           preferred_element_type=F32)


def _iota(shape, dim):
    return lax.broadcasted_iota(jnp.int32, shape, dim)


def _params(*sem):
    return pltpu.CompilerParams(dimension_semantics=sem, vmem_limit_bytes=VMEM_LIMIT)


def _ada_norm(x, g, shift, scale):
    ms = jnp.mean(x * x, axis=-1, keepdims=True)
    y = x * lax.rsqrt(ms + EPS) * g
    return y * (1.0 + scale) + shift


def _ada_kernel(c_ref, w_ref, b_ref, o_ref):
    c = c_ref[...]
    ca = c * jax.nn.sigmoid(c)
    o_ref[0] = jnp.dot(ca, w_ref[0], precision=HIGHEST, preferred_element_type=F32) + b_ref[0]


def _ada_mod(c, w_ada, b_ada):
    depth, d, n = w_ada.shape
    b = c.shape[0]
    tn = n // 4
    return pl.pallas_call(
        _ada_kernel,
        grid=(depth, n // tn),
        in_specs=[pl.BlockSpec((b, d), lambda l, j: (0, 0)),
                  pl.BlockSpec((1, d, tn), lambda l, j: (l, 0, j)),
                  pl.BlockSpec((1, 1, tn), lambda l, j: (l, 0, j))],
        out_specs=pl.BlockSpec((1, b, tn), lambda l, j: (l, 0, j)),
        out_shape=jax.ShapeDtypeStruct((depth, b, n), F32),
        compiler_params=_params("parallel", "parallel"),
        name="ada_mod",
    )(c, w_ada, b_ada.reshape(depth, 1, n))


def _swiglu_residual(x, mod_ref, g_ref, fg_ref, w13_ref, w2_ref, *, sub, coef, final, splits):
    h = _ada_norm(x, g_ref[...], mod_ref[0, 3 * sub:3 * sub + 1, :],
                  mod_ref[0, 3 * sub + 1:3 * sub + 2, :]).astype(BF16)
    acc = None
    for lo, hi in splits:
        a = _dot(h, w13_ref[:, lo:hi])
        b = _dot(h, w13_ref[:, D_FF + lo:D_FF + hi])
        u = (a * jax.nn.sigmoid(a) * b).astype(BF16)
        part = _dot(u, w2_ref[lo:hi, :])
        acc = part if acc is None else acc + part
    y = x + (coef * mod_ref[0, 3 * sub + 2:3 * sub + 3, :]) * acc
    if final:
        y = y * lax.rsqrt(jnp.mean(y * y, axis=-1, keepdims=True) + EPS) * fg_ref[...]
    return y


def _ffn_kernel(x_ref, mod_ref, g_ref, fg_ref, w13_ref, w2_ref, o_ref, **kw):
    o_ref[...] = _swiglu_residual(x_ref[...], mod_ref, g_ref, fg_ref, w13_ref, w2_ref, **kw)


def _mix_ffn_kernel(x_ref, mod_ref, g_ref, fg_ref, on_ref, of_ref, om_ref, wo_ref, w13_ref, w2_ref, o_ref, **kw):
    n_w = NSA_HEADS * HEAD_DIM
    f_w = FOX_HEADS * HEAD_DIM
    mix = _dot(on_ref[...], wo_ref[0:n_w, :])
    mix += _dot(of_ref[...], wo_ref[n_w:n_w + f_w, :])
    mix += _dot(om_ref[...], wo_ref[n_w + f_w:, :])
    x = x_ref[...] + mod_ref[0, 5:6, :] * mix
    o_ref[...] = _swiglu_residual(x, mod_ref, g_ref, fg_ref, w13_ref, w2_ref, **kw)


def _resident(shape):
    return pl.BlockSpec(shape, lambda i: (0,) * len(shape), pipeline_mode=pl.Buffered(1))


def _hidden_splits(chunk):
    return tuple((lo, min(lo + chunk, D_FF)) for lo in range(0, D_FF, chunk))


def _ffn(x2, mod, g, final_g, w13, w2, *, sub, seq, final, tm=1024, chunk=3 * MXU_TILE):
    n, d = x2.shape
    per_b = seq // tm
    return pl.pallas_call(
        functools.partial(_ffn_kernel, sub=sub, coef=0.5, final=final, splits=_hidden_splits(chunk)),
        grid=(n // tm,),
        in_specs=[pl.BlockSpec((tm, d), lambda i: (i, 0)),
                  pl.BlockSpec((1, 9, d), lambda i: (i // per_b, 0, 0)),
                  pl.BlockSpec((1, d), lambda i: (0, 0)),
                  pl.BlockSpec((1, d), lambda i: (0, 0)),
                  _resident((d, 2 * D_FF)), _resident((D_FF, d))],
        out_specs=pl.BlockSpec((tm, d), lambda i: (i, 0)),
        out_shape=jax.ShapeDtypeStruct((n, d), F32),
        compiler_params=_params("parallel"),
        name="ffn",
    )(x2, mod, g, final_g, w13, w2)


def _mix_ffn(x2, mod, g, final_g, o_nsa, o_fox, o_moba, w_out, w13, w2, *, seq, final, tm=512,
             chunk=4 * MXU_TILE):
    n, d = x2.shape
    per_b = seq // tm
    row = lambda width: pl.BlockSpec((tm, width), lambda i: (i, 0))
    return pl.pallas_call(
        functools.partial(_mix_ffn_kernel, sub=2, coef=0.5, final=final, splits=_hidden_splits(chunk)),
        grid=(n // tm,),
        in_specs=[row(d),
                  pl.BlockSpec((1, 9, d), lambda i: (i // per_b, 0, 0)),
                  pl.BlockSpec((1, d), lambda i: (0, 0)),
                  pl.BlockSpec((1, d), lambda i: (0, 0)),
                  row(o_nsa.shape[1]), row(o_fox.shape[1]), row(o_moba.shape[1]),
                  _resident((d, d)), _resident((d, 2 * D_FF)), _resident((D_FF, d))],
        out_specs=row(d),
        out_shape=jax.ShapeDtypeStruct((n, d), F32),
        compiler_params=_params("parallel"),
        name="mix_ffn",
    )(x2, mod, g, final_g, o_nsa, o_fox, o_moba, w_out, w13, w2)


def _inproj_kernel(x_ref, mod_ref, g_ref, posr_ref, invc_ref, fb_ref, w_ref, wt_ref,
                   qt_ref, nvt_ref, gt_ref, fqt_ref, fvt_ref, mqt_ref, mvt_ref,
                   cmp_ref, nks_ref, nkw_ref, fk_ref, mk_ref, gm_ref, *, per_b):
    h = _ada_norm(x_ref[...], g_ref[...], mod_ref[0, 3:4, :], mod_ref[0, 4:5, :]).astype(BF16)
    tm = h.shape[0]
    seq_pos = (pl.program_id(0) % per_b) * tm + _iota((tm, LANES), 0)

    half = ROPE_DIM // 2
    ang_t = invc_ref[...] * posr_ref[0].astype(F32)
    cos_t = jnp.cos(ang_t)
    sin_t = jnp.sin(ang_t)
    ones_row = jnp.where(_iota((HEAD_DIM, tm), 0) == 0, 1.0, 0.0)
    slab_rows = 4 * HEAD_DIM

    def queries_t(row0, n_heads, out_ref, rotate):
        for s_idx in range(n_heads // 4):
            y_t = _dot_nt(wt_ref[row0 + slab_rows * s_idx:row0 + slab_rows * (s_idx + 1), :], h)
            for hh in range(4):
                blk = y_t[HEAD_DIM * hh:HEAD_DIM * (hh + 1)]
                if rotate:
                    r1, r2 = blk[:half], blk[half:2 * half]
                    blk = jnp.concatenate([r1 * cos_t - r2 * sin_t, r2 * cos_t + r1 * sin_t, blk[2 * half:]],
                                          axis=0)
                out_ref[0, 4 * s_idx + hh] = (blk * ATT_SCALE).astype(BF16)

    def values_t(row0, out_ref):
        y_t = _dot_nt(wt_ref[row0:row0 + slab_rows, :], h)
        for idx in range(4):
            out_ref[0, idx] = jnp.concatenate([y_t[HEAD_DIM * idx:HEAD_DIM * (idx + 1)], ones_row],
                                              axis=0).astype(BF16)

    queries_t(_T_NSA_Q, NSA_HEADS, qt_ref, True)
    values_t(_T_NSA_V, nvt_ref)
    sig_t = jax.nn.sigmoid(_dot_nt(wt_ref[_T_GATE:_T_GATE + LANES, :], h))
    for g in range(NSA_GROUPS):
        gt_ref[0, g] = sig_t[GATE_ROWS * g:GATE_ROWS * (g + 1)]
    queries_t(_T_FOX_Q, FOX_HEADS, fqt_ref, False)
    values_t(_T_FOX_V, fvt_ref)
    queries_t(_T_MOBA_Q, MOBA_HEADS, mqt_ref, True)
    values_t(_T_MOBA_V, mvt_ref)

    zeros_t = jnp.zeros((half, tm), F32)

    def lane_table(first, second, fill):
        head = jnp.concatenate([first, second, jnp.full((HEAD_DIM - ROPE_DIM, tm), fill, F32)], axis=0)
        return jnp.concatenate([head, head], axis=0).T

    cosl = lane_table(cos_t, cos_t, 1.0)
    sin_lo = lane_table(-sin_t, zeros_t, 0.0)
    sin_hi = lane_table(zeros_t, sin_t, 0.0)
    lane = _iota((tm, LANES), 1)

    def rope(y):
        return (y * cosl + pltpu.roll(y, LANES - ROPE_DIM // 2, 1) * sin_lo
                + pltpu.roll(y, ROPE_DIM // 2, 1) * sin_hi)

    def put_heads(z, out_ref, head0, dtype):
        z = z.astype(dtype)
        out_ref[0, head0] = z[:, :HEAD_DIM]
        out_ref[0, head0 + 1] = z[:, HEAD_DIM:]

    def put_wide(z, out_ref, head0, tail):
        out_ref[0, head0] = jnp.where(lane < HEAD_DIM, z, tail).astype(BF16)
        out_ref[0, head0 + 1] = jnp.where(lane < HEAD_DIM, pltpu.roll(z, HEAD_DIM, 1), tail).astype(BF16)

    def put_keys(z, out_ref, head0, block_len):
        block = lax.shift_right_logical(seq_pos, block_len.bit_length() - 1)
        put_wide(z, out_ref, head0, jnp.where(lane - HEAD_DIM == block, 1.0, 0.0))

    def slab(idx):
        y = _dot(h, w_ref[:, idx * 2 * LANES:(idx + 1) * 2 * LANES])
        return y[:, :LANES], y[:, LANES:]

    a, b = slab(0)
    put_heads(rope(a), cmp_ref, 0, F32)
    put_heads(b, cmp_ref, 2, F32)
    a, b = slab(1)
    put_keys(rope(a), nks_ref, 0, SLC_LEN)
    put_heads(rope(b), nkw_ref, 0, BF16)
    a, b = slab(2)
    put_heads(a, fk_ref, 0, BF16)
    put_heads(b, fk_ref, 2, BF16)
    a, b = slab(3)
    put_keys(rope(a), mk_ref, 0, MOBA_BLOCK)
    put_keys(rope(b), mk_ref, 2, MOBA_BLOCK)
    f = _dot(h, w_ref[:, N_HEAD_COLS:])
    gm_ref[0] = jax.nn.log_sigmoid(f + fb_ref[...])


def _inproj(x2, mod, g, positions, inv_col, fb_lane, w_p, w_t, *, batch, seq, tm=512):
    n, d = x2.shape
    per_b = seq // tm

    def hm(nh, dtype, width=HEAD_DIM):
        return (pl.BlockSpec((1, nh, tm, width), lambda i: (i // per_b, 0, i % per_b, 0)),
                jax.ShapeDtypeStruct((batch, nh, seq, width), dtype))

    def tr(nh, rows, dtype):
        return (pl.BlockSpec((1, nh, rows, tm), lambda i: (i // per_b, 0, 0, i % per_b)),
                jax.ShapeDtypeStruct((batch, nh, rows, seq), dtype))

    specs = [tr(NSA_HEADS, HEAD_DIM, BF16), tr(4, LANES, BF16), tr(NSA_GROUPS, GATE_ROWS, F32),
             tr(FOX_HEADS, HEAD_DIM, BF16), tr(FOX_HEADS, LANES, BF16),
             tr(MOBA_HEADS, HEAD_DIM, BF16), tr(MOBA_HEADS, LANES, BF16),
             hm(4, F32), hm(2, BF16, LANES), hm(2, BF16), hm(FOX_HEADS, BF16), hm(MOBA_HEADS, BF16, LANES),
             (pl.BlockSpec((1, tm, LANES), lambda i: (i // per_b, i % per_b, 0)),
              jax.ShapeDtypeStruct((batch, seq, LANES), F32))]
    const = lambda shape: pl.BlockSpec(shape, lambda i: (0,) * len(shape))
    return pl.pallas_call(
        functools.partial(_inproj_kernel, per_b=per_b),
        grid=(n // tm,),
        in_specs=[pl.BlockSpec((tm, d), lambda i: (i, 0)),
                  pl.BlockSpec((1, 9, d), lambda i: (i // per_b, 0, 0)),
                  const((1, d)),
                  pl.BlockSpec((1, 1, tm), lambda i: (i, 0, 0)),
                  const((ROPE_DIM // 2, 1)), const((1, LANES)),
                  _resident(w_p.shape), _resident(w_t.shape)],
        out_specs=[s for s, _ in specs],
        out_shape=[s for _, s in specs],
        compiler_params=_params("parallel"),
        name="inproj",
    )(x2, mod, g, positions.reshape(n // tm, 1, tm), inv_col, fb_lane, w_p, w_t)


def _compress_kernel(z_ref, pos_ref, w1_ref, w2_ref, w2t_ref, o_ref, ot_ref):
    z = z_ref[0, 0]
    top = _dot((z + pos_ref[0, 0]).astype(BF16), w1_ref[0, 0])
    bot = _dot((z + pos_ref[0, 1]).astype(BF16), w1_ref[0, 1])
    nch = z.shape[0]
    hid = top + pltpu.roll(bot, nch - 1, 0)
    act = jax.nn.gelu(hid).astype(BF16)
    o_ref[0, 0] = _dot(act, w2_ref[0]).astype(o_ref.dtype)
    ot_ref[0, 0] = _dot_nt(w2t_ref[0], act).astype(ot_ref.dtype)


def _compress(cmp_in, pos_l, w1_l, w2_l, w2t_l):
    b, four, nch, width = cmp_in.shape
    return pl.pallas_call(
        _compress_kernel,
        grid=(b, four),
        in_specs=[pl.BlockSpec((1, 1, nch, width), lambda i, n: (i, n, 0, 0)),
                  pl.BlockSpec((1, 2, 1, width), lambda i, n: (n // 2, 0, 0, 0)),
                  pl.BlockSpec((1, 2, width, CMP_HIDDEN), lambda i, n: (n // 2, 0, 0, 0)),
                  pl.BlockSpec((1, CMP_HIDDEN, HEAD_DIM), lambda i, n: (n // 2, 0, 0)),
                  pl.BlockSpec((1, HEAD_DIM, CMP_HIDDEN), lambda i, n: (n // 2, 0, 0))],
        out_specs=[pl.BlockSpec((1, 1, nch, HEAD_DIM), lambda i, n: (i, n, 0, 0)),
                   pl.BlockSpec((1, 1, HEAD_DIM, nch), lambda i, n: (i, n, 0, 0))],
        out_shape=[jax.ShapeDtypeStruct((b, four, nch, HEAD_DIM), BF16),
                   jax.ShapeDtypeStruct((b, four, HEAD_DIM, nch), BF16)],
        compiler_params=_params("parallel", "parallel"),
        name="nsa_compress",
    )(cmp_in, pos_l, w1_l, w2_l, w2t_l)


def _chunk_update_t(carry, s, v_t):
    m, acc = carry
    m_new = jnp.maximum(m, jnp.max(s, axis=0, keepdims=True))
    p = jnp.exp(s - m_new).astype(BF16)
    return m_new, jnp.exp(m - m_new) * acc + _dot(v_t, p)


def _chunk_init_t(n):
    return jnp.full((1, n), NEG, F32), jnp.zeros((LANES, n), F32)


def _chunk_finish_t(carry):
    _, acc = carry
    return acc[:HEAD_DIM] / jnp.maximum(acc[HEAD_DIM:HEAD_DIM + 1], 1e-30)


def _heads_to_rows(heads_t):
    pairs = [jnp.concatenate(heads_t[a:a + 2], axis=0).T for a in range(0, len(heads_t), 2)]
    return jnp.concatenate(pairs, axis=1)


def _nsa_kernel(qt_ref, kc_ref, vct_ref, ks_ref, vst_ref, kw_ref, vwt_ref, gt_ref, ovt_ref, o_ref, *, tq, tk):
    seq = ks_ref.shape[2]
    for i in range(seq // tq):
        q_cols = slice(i * tq, (i + 1) * tq)
        _nsa_tile(i, qt_ref.at[:, :, :, q_cols], kc_ref, vct_ref, ks_ref, vst_ref, kw_ref, vwt_ref,
                  gt_ref.at[:, :, :, q_cols], ovt_ref, o_ref.at[:, q_cols, :], tq=tq, tk=tk)


def _nsa_tile(i, qt_ref, kc_ref, vct_ref, ks_ref, vst_ref, kw_ref, vwt_ref, gt_ref, ovt_ref, o_ref, *, tq, tk):
    rep = NSA_REP
    groups = range(qt_ref.shape[1] // rep)
    n_c = kc_ref.shape[2]
    n_s = ks_ref.shape[2] // SLC_LEN
    tile = lambda a: jnp.concatenate([a] * rep, axis=1)
    t_q = i * tq + _iota((1, tq), 1)
    cend = _iota((n_c, 1), 0) * CMP_STRIDE + (CMP_LEN - 1)
    cm = cend <= tile(t_q)
    j_t = _iota((n_s, tq), 0)
    tb = lax.shift_right_logical(t_q, SLC_LEN.bit_length() - 1)
    valid = j_t <= tb
    forced = (j_t == 0) | (j_t == tb) | (j_t == tb - 1)
    start = max(i * tq - WIN, 0)
    span = (i + 1) * tq - start
    dist = t_q - (start + _iota((span, 1), 0))
    band = tile(jnp.where((dist >= 0) & (dist < WIN), 0.0, NEG))

    o_cmp, o_win, q4a = [], [], []
    for g in groups:
        q4 = jnp.concatenate([qt_ref[0, rep * g + r] for r in range(rep)], axis=1)
        s = jnp.where(cm, _dot(kc_ref[0, g], q4), NEG)
        e = jnp.where(cm, jnp.exp(s - jnp.max(s, axis=0, keepdims=True)), 0.0)
        p = e / jnp.maximum(jnp.sum(e, axis=0, keepdims=True), 1e-30)
        o_cmp.append(_dot(vct_ref[0, g], p.astype(BF16)))
        psum = p[:, 0:tq] + p[:, tq:2 * tq] + p[:, 2 * tq:3 * tq] + p[:, 3 * tq:4 * tq]
        imp = jnp.dot(ovt_ref[...], psum, precision=HIGHEST, preferred_element_type=F32)[:n_s]
        v_t = jnp.where(valid, jnp.where(forced, jnp.inf, imp), -jnp.inf)
        rank = jnp.zeros((n_s, tq), jnp.int32)
        for jp in range(n_s):
            row = v_t[jp:jp + 1, :]
            ahead = (row > v_t) | ((row == v_t) & (j_t > jp))
            rank = rank + ahead.astype(jnp.int32)
        neg_t = jnp.where((rank < SLC_TOPN) & (v_t > -jnp.inf), 0.0, NEG)
        neg = jnp.concatenate([neg_t, jnp.zeros((HEAD_DIM - n_s, tq), F32)], axis=0)
        q4a.append(jnp.concatenate([q4, tile(neg).astype(BF16)], axis=0))
        sw = _dot(kw_ref[0, g, start:start + span, :], q4) + band
        pw = jnp.exp(sw - jnp.max(sw, axis=0, keepdims=True)).astype(BF16)
        aw = _dot(vwt_ref[0, g, :, start:start + span], pw)
        o_win.append(aw[:HEAD_DIM] / jnp.maximum(aw[HEAD_DIM:HEAD_DIM + 1], 1e-30))

    def slc_step(k0, k1, carries, diag):
        new = []
        for g in groups:
            sc = _dot(ks_ref[0, g, k0:k1, :], q4a[g])
            if diag:
                sc = sc + tile(jnp.where(k0 + _iota((k1 - k0, 1), 0) <= t_q, 0.0, NEG))
            new.append(_chunk_update_t(carries[g], sc, vst_ref[0, g, :, k0:k1]))
        return tuple(new)

    end = (i + 1) * tq
    last = (end - 1) // tk * tk
    carries = slc_step(last, end, tuple(_chunk_init_t(rep * tq) for _ in groups), True)
    for k0 in range(0, last, tk):
        carries = slc_step(k0, k0 + tk, carries, False)

    heads = []
    for g in groups:
        o_slc = _chunk_finish_t(carries[g])
        gates = gt_ref[0, g]
        for r in range(rep):
            cols = slice(r * tq, (r + 1) * tq)
            heads.append(gates[3 * r:3 * r + 1] * o_cmp[g][:, cols]
                         + gates[3 * r + 1:3 * r + 2] * o_slc[:, cols]
                         + gates[3 * r + 2:3 * r + 3] * o_win[g][:, cols])
    o_ref[0] = _heads_to_rows(heads).astype(o_ref.dtype)


def _nsa(qt, cmp_kv, cmp_t, nks, nkw, nvt, gt, ovt, *, tq=256, tk=512):
    b, _, _, seq = qt.shape
    n_c = cmp_kv.shape[2]
    ng = NSA_GROUPS
    return pl.pallas_call(
        functools.partial(_nsa_kernel, tq=tq, tk=tk),
        grid=(b, ng),
        in_specs=[pl.BlockSpec((1, NSA_REP, HEAD_DIM, seq), lambda bi, g: (bi, g, 0, 0)),
                  pl.BlockSpec((1, 1, n_c, HEAD_DIM), lambda bi, g: (bi, g, 0, 0)),
                  pl.BlockSpec((1, 1, HEAD_DIM, n_c), lambda bi, g: (bi, ng + g, 0, 0)),
                  pl.BlockSpec((1, 1, seq, LANES), lambda bi, g: (bi, g, 0, 0)),
                  pl.BlockSpec((1, 1, LANES, seq), lambda bi, g: (bi, g, 0, 0)),
                  pl.BlockSpec((1, 1, seq, HEAD_DIM), lambda bi, g: (bi, g, 0, 0)),
                  pl.BlockSpec((1, 1, LANES, seq), lambda bi, g: (bi, ng + g, 0, 0)),
                  pl.BlockSpec((1, 1, GATE_ROWS, seq), lambda bi, g: (bi, g, 0, 0)),
                  pl.BlockSpec((LANES, n_c), lambda bi, g: (0, 0))],
        out_specs=pl.BlockSpec((1, seq, NSA_REP * HEAD_DIM), lambda bi, g: (bi, 0, g)),
        out_shape=jax.ShapeDtypeStruct((b, seq, NSA_HEADS * HEAD_DIM), BF16),
        compiler_params=_params("parallel", "parallel"),
        name="nsa",
    )(qt, cmp_kv, cmp_t, nks, nvt, nkw, nvt, gt, ovt)


def _fox_kernel(qt_ref, k_ref, vt_ref, lf_ref, cum_ref, tri_ref, o_ref, fcol_ref, frow_ref, *, tq):
    heads = range(FOX_HEADS)
    seq = lf_ref.shape[1]
    blk = cum_ref.shape[0]
    carry = jnp.zeros((1, LANES), F32)
    for c in range(seq // blk):
        rows = slice(c * blk, (c + 1) * blk)
        cs = jnp.dot(cum_ref[...], lf_ref[0, rows, :], precision=HIGHEST,
                     preferred_element_type=F32) + carry
        fcol_ref[rows, :] = cs
        carry = cs[blk - 1:blk, :]
    frow_ref[...] = fcol_ref[...].T[_F_SLAB:_F_SLAB + SUBLANES, :]

    tk = tri_ref.shape[1]
    per_chunk = tk // tq
    for i in range(seq // tq):
        q_cols = slice(i * tq, (i + 1) * tq)
        qs = [qt_ref[0, h, :, q_cols] for h in heads]
        fqs = [frow_ref[_F_ROW0 + h:_F_ROW0 + h + 1, q_cols] for h in heads]

        def step(c, carries, diag, qs=qs, fqs=fqs, i=i):
            keys = slice(c * tk, (c + 1) * tk)
            new = []
            for h in heads:
                fk = fcol_ref[keys, GATE_LANES + h:GATE_LANES + h + 1]
                sc = _dot(k_ref[0, h, keys, :], qs[h]) + (fqs[h] - fk)
                if diag:
                    sc = sc + tri_ref[i % per_chunk]
                new.append(_chunk_update_t(carries[h], sc, vt_ref[0, h, :, keys]))
            return tuple(new)

        carries = step(i // per_chunk, tuple(_chunk_init_t(tq) for _ in heads), True)
        for c in range(i // per_chunk):
            carries = step(c, carries, False)
        o_ref[0, q_cols, :] = _heads_to_rows([_chunk_finish_t(cr) for cr in carries]).astype(o_ref.dtype)


def _fox(fqt, fk, fvt, gm, cum, tri_t, *, tq=FOX_TILE):
    b, nh, _, seq = fqt.shape
    blk = cum.shape[0]
    return pl.pallas_call(
        functools.partial(_fox_kernel, tq=tq),
        grid=(b,),
        in_specs=[pl.BlockSpec((1, nh, HEAD_DIM, seq), lambda bi: (bi, 0, 0, 0)),
                  pl.BlockSpec((1, nh, seq, HEAD_DIM), lambda bi: (bi, 0, 0, 0)),
                  pl.BlockSpec((1, nh, LANES, seq), lambda bi: (bi, 0, 0, 0)),
                  pl.BlockSpec((1, seq, LANES), lambda bi: (bi, 0, 0)),
                  pl.BlockSpec((blk, blk), lambda bi: (0, 0)),
                  pl.BlockSpec(tri_t.shape, lambda bi: (0, 0, 0))],
        out_specs=pl.BlockSpec((1, seq, nh * HEAD_DIM), lambda bi: (bi, 0, 0)),
        out_shape=jax.ShapeDtypeStruct((b, seq, nh * HEAD_DIM), BF16),
        scratch_shapes=[pltpu.VMEM((seq, LANES), F32), pltpu.VMEM((8, seq), F32)],
        compiler_params=_params("parallel"),
        name="fox",
    )(fqt, fk, fvt, gm, cum, tri_t)


def _moba_kernel(qt_ref, k_ref, vt_ref, avg_ref, tri_ref, o_ref, *, tq):
    heads = range(MOBA_HEADS)
    nb = avg_ref.shape[0]
    seq = k_ref.shape[2]
    kms = [_dot(avg_ref[...], k_ref[0, h])[:, :HEAD_DIM] for h in heads]
    blk = _iota((nb, tq), 0)
    in_tile = lax.shift_right_logical(_iota((nb, tq), 1), MOBA_BLOCK.bit_length() - 1)

    for i in range(seq // tq):
        q_cols = slice(i * tq, (i + 1) * tq)
        own = i * (tq // MOBA_BLOCK) + in_tile
        qas = []
        for h in heads:
            q = qt_ref[0, h, :, q_cols]
            gate = jnp.dot(kms[h], q.astype(F32), precision=HIGHEST, preferred_element_type=F32)
            val = jnp.where(blk < own, gate, -jnp.inf)
            rank = jnp.zeros((nb, tq), jnp.int32)
            for jp in range(nb):
                row = val[jp:jp + 1, :]
                ahead = (row > val) | ((row == val) & (blk > jp))
                rank = rank + ahead.astype(jnp.int32)
            keep = ((rank < MOBA_TOPK) & (blk < own)) | (blk == own)
            neg = jnp.concatenate([jnp.where(keep, 0.0, NEG), jnp.zeros((HEAD_DIM - nb, tq), F32)], axis=0)
            qas.append(jnp.concatenate([q, neg.astype(BF16)], axis=0))

        def step(c, carries, diag, qas=qas):
            keys = slice(c * tq, (c + 1) * tq)
            new = []
            for h in heads:
                sc = _dot(k_ref[0, h, keys, :], qas[h])
                if diag:
                    sc = sc + tri_ref[...]
                new.append(_chunk_update_t(carries[h], sc, vt_ref[0, h, :, keys]))
            return tuple(new)

        carries = step(i, tuple(_chunk_init_t(tq) for _ in heads), True)
        for c in range(i):
            carries = step(c, carries, False)
        o_ref[0, q_cols, :] = _heads_to_rows([_chunk_finish_t(cr) for cr in carries]).astype(o_ref.dtype)


def _moba(mqt, mk, mvt, avg, tri_t, *, tq=MOBA_TILE):
    b, nh, _, seq = mqt.shape
    nb = avg.shape[0]
    return pl.pallas_call(
        functools.partial(_moba_kernel, tq=tq),
        grid=(b,),
        in_specs=[pl.BlockSpec((1, nh, HEAD_DIM, seq), lambda bi: (bi, 0, 0, 0)),
                  pl.BlockSpec((1, nh, seq, LANES), lambda bi: (bi, 0, 0, 0)),
                  pl.BlockSpec((1, nh, LANES, seq), lambda bi: (bi, 0, 0, 0)),
                  pl.BlockSpec((nb, seq), lambda bi: (0, 0)),
                  pl.BlockSpec((tq, tq), lambda bi: (0, 0))],
        out_specs=pl.BlockSpec((1, seq, nh * HEAD_DIM), lambda bi: (bi, 0, 0)),
        out_shape=jax.ShapeDtypeStruct((b, seq, nh * HEAD_DIM), BF16),
        compiler_params=_params("parallel"),
        name="moba",
    )(mqt, mk, mvt, avg, tri_t)


def _pack_w_in(w_in):
    depth, d, _ = w_in.shape

    def cols(a, b):
        return w_in[:, :, a:b]

    zeros = lambda n: jnp.zeros((depth, d, n), w_in.dtype)
    g_half = 3 * NSA_REP
    token_major = [cols(_OFF_KC, _OFF_KS),
                   cols(_OFF_KS, _OFF_VS), cols(_OFF_KW, _OFF_VW),
                   cols(_OFF_FOX_K, _OFF_FOX_V),
                   cols(_OFF_MOBA_K, _OFF_MOBA_V),
                   zeros(GATE_LANES), cols(_OFF_FOX_F, _OFF_FOX_F + FOX_HEADS),
                   zeros(LANES - GATE_LANES - FOX_HEADS)]
    transposed = [cols(_OFF_NSA_Q, _OFF_KC),
                  cols(_OFF_VS, _OFF_KW), cols(_OFF_VW, _OFF_GATE),
                  cols(_OFF_GATE, _OFF_GATE + g_half), zeros(GATE_ROWS - g_half),
                  cols(_OFF_GATE + g_half, _OFF_GATE + 2 * g_half), zeros(LANES - GATE_ROWS - g_half),
                  cols(_OFF_FOX_Q, _OFF_FOX_K), cols(_OFF_FOX_V, _OFF_FOX_F),
                  cols(_OFF_MOBA_Q, _OFF_MOBA_K), cols(_OFF_MOBA_V, IN_COLS)]
    w_p = jnp.concatenate(token_major, axis=-1).astype(BF16)
    w_t = jnp.swapaxes(jnp.concatenate(transposed, axis=-1), 1, 2).astype(BF16)
    return w_p, w_t


def _constants(seq):
    n_c = seq // CMP_STRIDE
    n_s = seq // SLC_LEN
    c0 = np.arange(n_c)[None, :] * CMP_STRIDE
    j0 = np.arange(LANES)[:, None] * SLC_LEN
    real = (np.arange(n_c)[None, :] < (seq - CMP_LEN) // CMP_STRIDE + 1) & (np.arange(LANES)[:, None] < n_s)
    ov = ((c0 < j0 + SLC_LEN) & (c0 + CMP_LEN > j0) & real).astype(np.float32)
    nb = seq // MOBA_BLOCK
    avg = (np.arange(nb)[:, None] == (np.arange(seq)[None, :] // MOBA_BLOCK)).astype(np.float32) / MOBA_BLOCK
    tri = np.tril(np.ones((256, 256), np.float32))

    kk = np.arange(FOX_CHUNK)[None, :, None]
    qq = np.arange(FOX_TILE)[None, None, :] + FOX_TILE * np.arange(FOX_CHUNK // FOX_TILE)[:, None, None]
    causal_fox = np.where(kk <= qq, 0.0, NEG).astype(np.float32)
    r = np.arange(MOBA_TILE)
    same = (r[:, None] // MOBA_BLOCK) == (r[None, :] // MOBA_BLOCK)
    allowed = np.where(same, r[:, None] <= r[None, :], r[:, None] < r[None, :])
    causal_moba = np.where(allowed, 0.0, NEG).astype(np.float32)
    return dict(ov=jnp.asarray(ov), avg=jnp.asarray(avg, dtype=BF16), tri=jnp.asarray(tri),
                causal_fox=jnp.asarray(causal_fox), causal_moba=jnp.asarray(causal_moba))


def kernel(x, c, positions, norm_g, w_ada, b_ada, w_in, fox_fbias, cmp_pos, cmp_w1, cmp_w2, w_out,
           ffn_w13, ffn_w2, final_g):
    batch, seq, d = x.shape
    depth = w_ada.shape[0]
    assert d == D_MODEL and seq % MOBA_BLOCK == 0 and seq % 1024 == 0

    w_in_p, w_in_t = _pack_w_in(w_in)
    w13 = ffn_w13.astype(BF16)
    w2 = ffn_w2.astype(BF16)
    w_o = w_out.astype(BF16)
    half = CMP_LEN * HEAD_DIM // 2
    cw1 = cmp_w1.astype(BF16).reshape(depth, 2, 2, half, CMP_HIDDEN)
    cw2 = cmp_w2.astype(BF16)
    cw2t = jnp.swapaxes(cw2, 2, 3)
    cpos = cmp_pos.reshape(depth, 2, 2, 1, half)
    cst = _constants(seq)

    inv = ROPE_THETA ** (-jnp.arange(0, ROPE_DIM, 2, dtype=F32) / ROPE_DIM)
    inv_col = inv.reshape(ROPE_DIM // 2, 1)
    fb_lane = jnp.zeros((depth, 1, LANES), F32).at[:, 0, GATE_LANES:GATE_LANES + FOX_HEADS].set(fox_fbias)

    mod_all = _ada_mod(c, w_ada, b_ada).reshape(depth, batch, 9, d)
    x2 = x.reshape(batch * seq, d)
    fg = final_g.reshape(1, d)
    for l in range(depth):
        mod = mod_all[l]
        x2 = _ffn(x2, mod, norm_g[l, 0:1], fg, w13[l, 0], w2[l, 0], sub=0, seq=seq, final=False)
        qt, nvt, gt, fqt, fvt, mqt, mvt, cmp_in, nks, nkw, fk, mk, gm = _inproj(
            x2, mod, norm_g[l, 1:2], positions, inv_col, fb_lane[l], w_in_p[l], w_in_t[l],
            batch=batch, seq=seq)
        cmp_kv, cmp_t = _compress(cmp_in.reshape(batch, 4, seq // CMP_STRIDE, CMP_STRIDE * HEAD_DIM),
                                  cpos[l], cw1[l], cw2[l], cw2t[l])
        o_nsa = _nsa(qt, cmp_kv, cmp_t, nks, nkw, nvt, gt, cst["ov"])
        o_fox = _fox(fqt, fk, fvt, gm, cst["tri"], cst["causal_fox"])
        o_moba = _moba(mqt, mk, mvt, cst["avg"], cst["causal_moba"])
        x2 = _mix_ffn(x2, mod, norm_g[l, 2:3], fg, o_nsa.reshape(batch * seq, -1),
                      o_fox.reshape(batch * seq, -1), o_moba.reshape(batch * seq, -1), w_o[l],
                      w13[l, 1], w2[l, 1], seq=seq, final=(l == depth - 1))
    return x2.reshape(batch, seq, d)
```

```python
import functools

import numpy as np
import jax
import jax.numpy as jnp
from jax import lax
from jax.experimental import pallas as pl
from jax.experimental.pallas import tpu as pltpu

D_MODEL = 1024
HEAD_DIM = 64
NSA_HEADS = 8
NSA_GROUPS = 2
NSA_REP = NSA_HEADS // NSA_GROUPS
FOX_HEADS = 4
MOBA_HEADS = 4
ROPE_DIM = HEAD_DIM // 4
ROPE_THETA = 500000.0
CMP_LEN = 32
CMP_STRIDE = 16
CMP_HIDDEN = 4 * HEAD_DIM
SLC_LEN = 64
SLC_TOPN = 8
WIN = 512
MOBA_BLOCK = 256
MOBA_TOPK = 3
D_FF = 2816
EPS = 1e-6
ATT_SCALE = HEAD_DIM ** -0.5

FOX_TILE = 512
FOX_CHUNK = 512
MOBA_TILE = 2 * MOBA_BLOCK
LANES = 128
MXU_TILE = 256
NEG = -1e30
VMEM_LIMIT = 56 * 1024 * 1024

F32 = jnp.float32
BF16 = jnp.bfloat16
HIGHEST = lax.Precision.HIGHEST

_KV = NSA_GROUPS * HEAD_DIM
_OFF_NSA_Q = 0
_OFF_KC = _OFF_NSA_Q + NSA_HEADS * HEAD_DIM
_OFF_VC = _OFF_KC + _KV
_OFF_KS = _OFF_VC + _KV
_OFF_VS = _OFF_KS + _KV
_OFF_KW = _OFF_VS + _KV
_OFF_VW = _OFF_KW + _KV
_OFF_GATE = _OFF_VW + _KV
_OFF_FOX_Q = _OFF_GATE + 3 * NSA_HEADS
_OFF_FOX_K = _OFF_FOX_Q + FOX_HEADS * HEAD_DIM
_OFF_FOX_V = _OFF_FOX_K + FOX_HEADS * HEAD_DIM
_OFF_FOX_F = _OFF_FOX_V + FOX_HEADS * HEAD_DIM
_OFF_MOBA_Q = _OFF_FOX_F + FOX_HEADS
_OFF_MOBA_K = _OFF_MOBA_Q + MOBA_HEADS * HEAD_DIM
_OFF_MOBA_V = _OFF_MOBA_K + MOBA_HEADS * HEAD_DIM
IN_COLS = _OFF_MOBA_V + MOBA_HEADS * HEAD_DIM

N_HEAD_COLS = 4 * _KV + (FOX_HEADS + MOBA_HEADS) * HEAD_DIM
GATE_LANES = 3 * NSA_REP
PROJ_COLS = N_HEAD_COLS + LANES
GATE_ROWS = 16
SUBLANES = 8
_F_SLAB = GATE_LANES // SUBLANES * SUBLANES
_F_ROW0 = GATE_LANES - _F_SLAB
_T_NSA_Q = 0
_T_NSA_V = _T_NSA_Q + NSA_HEADS * HEAD_DIM
_T_GATE = _T_NSA_V + 4 * HEAD_DIM
_T_FOX_Q = _T_GATE + LANES
_T_FOX_V = _T_FOX_Q + FOX_HEADS * HEAD_DIM
_T_MOBA_Q = _T_FOX_V + FOX_HEADS * HEAD_DIM
_T_MOBA_V = _T_MOBA_Q + MOBA_HEADS * HEAD_DIM
T_ROWS = _T_MOBA_V + MOBA_HEADS * HEAD_DIM


def _dot(a, b):
    return jnp.dot(a, b, preferred_element_type=F32)


def _dot_nt(a, b, precision=None):
    return lax.dot_general(a, b, (((1,), (1,)), ((), ())), precision=precision,
                           preferred_element_type=F32)


def _iota(shape, dim):
    return lax.broadcasted_iota(jnp.int32, shape, dim)


def _params(*sem):
    return pltpu.CompilerParams(dimension_semantics=sem, vmem_limit_bytes=VMEM_LIMIT)


def _ada_norm(x, g, shift, scale):
    ms = jnp.mean(x * x, axis=-1, keepdims=True)
    y = x * lax.rsqrt(ms + EPS) * g
    return y * (1.0 + scale) + shift


def _ada_kernel(c_ref, w_ref, b_ref, o_ref):
    c = c_ref[...]
    ca = c * jax.nn.sigmoid(c)
    o_ref[0] = jnp.dot(ca, w_ref[0], precision=HIGHEST, preferred_element_type=F32) + b_ref[0]


def _ada_mod(c, w_ada, b_ada):
    depth, d, n = w_ada.shape
    b = c.shape[0]
    tn = n // 4
    return pl.pallas_call(
        _ada_kernel,
        grid=(depth, n // tn),
        in_specs=[pl.BlockSpec((b, d), lambda l, j: (0, 0)),
                  pl.BlockSpec((1, d, tn), lambda l, j: (l, 0, j)),
                  pl.BlockSpec((1, 1, tn), lambda l, j: (l, 0, j))],
        out_specs=pl.BlockSpec((1, b, tn), lambda l, j: (l, 0, j)),
        out_shape=jax.ShapeDtypeStruct((depth, b, n), F32),
        compiler_params=_params("parallel", "parallel"),
        name="ada_mod",
    )(c, w_ada, b_ada.reshape(depth, 1, n))


def _swiglu_residual(x, mod_ref, g_ref, fg_ref, w13_ref, w2_ref, *, sub, coef, final, splits):
    h = _ada_norm(x, g_ref[...], mod_ref[0, 3 * sub:3 * sub + 1, :],
                  mod_ref[0, 3 * sub + 1:3 * sub + 2, :]).astype(BF16)
    acc = None
    for lo, hi in splits:
        a = _dot(h, w13_ref[:, lo:hi])
        b = _dot(h, w13_ref[:, D_FF + lo:D_FF + hi])
        u = (a * jax.nn.sigmoid(a) * b).astype(BF16)
        part = _dot(u, w2_ref[lo:hi, :])
        acc = part if acc is None else acc + part
    y = x + (coef * mod_ref[0, 3 * sub + 2:3 * sub + 3, :]) * acc
    if final:
        y = y * lax.rsqrt(jnp.mean(y * y, axis=-1, keepdims=True) + EPS) * fg_ref[...]
    return y


def _ffn_kernel(x_ref, mod_ref, g_ref, fg_ref, w13_ref, w2_ref, o_ref, **kw):
    o_ref[...] = _swiglu_residual(x_ref[...], mod_ref, g_ref, fg_ref, w13_ref, w2_ref, **kw)


def _mix_ffn_kernel(x_ref, mod_ref, g_ref, fg_ref, on_ref, of_ref, om_ref, wo_ref, w13_ref, w2_ref, o_ref, **kw):
    n_w = NSA_HEADS * HEAD_DIM
    f_w = FOX_HEADS * HEAD_DIM
    mix = _dot(on_ref[...], wo_ref[0:n_w, :])
    mix += _dot(of_ref[...], wo_ref[n_w:n_w + f_w, :])
    mix += _dot(om_ref[...], wo_ref[n_w + f_w:, :])
    x = x_ref[...] + mod_ref[0, 5:6, :] * mix
    o_ref[...] = _swiglu_residual(x, mod_ref, g_ref, fg_ref, w13_ref, w2_ref, **kw)


def _resident(shape):
    return pl.BlockSpec(shape, lambda i: (0,) * len(shape), pipeline_mode=pl.Buffered(1))


def _hidden_splits(chunk):
    return tuple((lo, min(lo + chunk, D_FF)) for lo in range(0, D_FF, chunk))


def _ffn(x2, mod, g, final_g, w13, w2, *, sub, seq, final, tm=1024, chunk=3 * MXU_TILE):
    n, d = x2.shape
    per_b = seq // tm
    return pl.pallas_call(
        functools.partial(_ffn_kernel, sub=sub, coef=0.5, final=final, splits=_hidden_splits(chunk)),
        grid=(n // tm,),
        in_specs=[pl.BlockSpec((tm, d), lambda i: (i, 0)),
                  pl.BlockSpec((1, 9, d), lambda i: (i // per_b, 0, 0)),
                  pl.BlockSpec((1, d), lambda i: (0, 0)),
                  pl.BlockSpec((1, d), lambda i: (0, 0)),
                  _resident((d, 2 * D_FF)), _resident((D_FF, d))],
        out_specs=pl.BlockSpec((tm, d), lambda i: (i, 0)),
        out_shape=jax.ShapeDtypeStruct((n, d), F32),
        compiler_params=_params("parallel"),
        name="ffn",
    )(x2, mod, g, final_g, w13, w2)


def _mix_ffn(x2, mod, g, final_g, o_nsa, o_fox, o_moba, w_out, w13, w2, *, seq, final, tm=512,
             chunk=4 * MXU_TILE):
    n, d = x2.shape
    per_b = seq // tm
    row = lambda width: pl.BlockSpec((tm, width), lambda i: (i, 0))
    return pl.pallas_call(
        functools.partial(_mix_ffn_kernel, sub=2, coef=0.5, final=final, splits=_hidden_splits(chunk)),
        grid=(n // tm,),
        in_specs=[row(d),
                  pl.BlockSpec((1, 9, d), lambda i: (i // per_b, 0, 0)),
                  pl.BlockSpec((1, d), lambda i: (0, 0)),
                  pl.BlockSpec((1, d), lambda i: (0, 0)),
                  row(o_nsa.shape[1]), row(o_fox.shape[1]), row(o_moba.shape[1]),
                  _resident((d, d)), _resident((d, 2 * D_FF)), _resident((D_FF, d))],
        out_specs=row(d),
        out_shape=jax.ShapeDtypeStruct((n, d), F32),
        compiler_params=_params("parallel"),
        name="mix_ffn",
    )(x2, mod, g, final_g, o_nsa, o_fox, o_moba, w_out, w13, w2)


def _inproj_kernel(x_ref, mod_ref, g_ref, posr_ref, invc_ref, fb_ref, w_ref, wt_ref,
                   qt_ref, nvt_ref, gt_ref, fqt_ref, fvt_ref, mqt_ref, mvt_ref,
                   cmp_ref, nks_ref, nkw_ref, fk_ref, mk_ref, gm_ref, *, per_b):
    h = _ada_norm(x_ref[...], g_ref[...], mod_ref[0, 3:4, :], mod_ref[0, 4:5, :]).astype(BF16)
    tm = h.shape[0]
    seq_pos = (pl.program_id(0) % per_b) * tm + _iota((tm, LANES), 0)

    half = ROPE_DIM // 2
    ang_t = invc_ref[...] * posr_ref[0].astype(F32)
    cos_t = jnp.cos(ang_t)
    sin_t = jnp.sin(ang_t)
    ones_row = jnp.where(_iota((HEAD_DIM, tm), 0) == 0, 1.0, 0.0)
    slab_rows = 4 * HEAD_DIM

    def queries_t(row0, n_heads, out_ref, rotate):
        for s_idx in range(n_heads // 4):
            y_t = _dot_nt(wt_ref[row0 + slab_rows * s_idx:row0 + slab_rows * (s_idx + 1), :], h)
            for hh in range(4):
                blk = y_t[HEAD_DIM * hh:HEAD_DIM * (hh + 1)]
                if rotate:
                    r1, r2 = blk[:half], blk[half:2 * half]
                    blk = jnp.concatenate([r1 * cos_t - r2 * sin_t, r2 * cos_t + r1 * sin_t, blk[2 * half:]],
                                          axis=0)
                out_ref[0, 4 * s_idx + hh] = (blk * ATT_SCALE).astype(BF16)

    def values_t(row0, out_ref):
        y_t = _dot_nt(wt_ref[row0:row0 + slab_rows, :], h)
        for idx in range(4):
            out_ref[0, idx] = jnp.concatenate([y_t[HEAD_DIM * idx:HEAD_DIM * (idx + 1)], ones_row],
                                              axis=0).astype(BF16)

    queries_t(_T_NSA_Q, NSA_HEADS, qt_ref, True)
    values_t(_T_NSA_V, nvt_ref)
    sig_t = jax.nn.sigmoid(_dot_nt(wt_ref[_T_GATE:_T_GATE + LANES, :], h))
    for g in range(NSA_GROUPS):
        gt_ref[0, g] = sig_t[GATE_ROWS * g:GATE_ROWS * (g + 1)]
    queries_t(_T_FOX_Q, FOX_HEADS, fqt_ref, False)
    values_t(_T_FOX_V, fvt_ref)
    queries_t(_T_MOBA_Q, MOBA_HEADS, mqt_ref, True)
    values_t(_T_MOBA_V, mvt_ref)

    zeros_t = jnp.zeros((half, tm), F32)

    def lane_table(first, second, fill):
        head = jnp.concatenate([first, second, jnp.full((HEAD_DIM - ROPE_DIM, tm), fill, F32)], axis=0)
        return jnp.concatenate([head, head], axis=0).T

    cosl = lane_table(cos_t, cos_t, 1.0)
    sin_lo = lane_table(-sin_t, zeros_t, 0.0)
    sin_hi = lane_table(zeros_t, sin_t, 0.0)
    lane = _iota((tm, LANES), 1)

    def rope(y):
        return (y * cosl + pltpu.roll(y, LANES - ROPE_DIM // 2, 1) * sin_lo
                + pltpu.roll(y, ROPE_DIM // 2, 1) * sin_hi)

    def put_heads(z, out_ref, head0, dtype):
        z = z.astype(dtype)
        out_ref[0, head0] = z[:, :HEAD_DIM]
        out_ref[0, head0 + 1] = z[:, HEAD_DIM:]

    def put_wide(z, out_ref, head0, tail):
        out_ref[0, head0] = jnp.where(lane < HEAD_DIM, z, tail).astype(BF16)
        out_ref[0, head0 + 1] = jnp.where(lane < HEAD_DIM, pltpu.roll(z, HEAD_DIM, 1), tail).astype(BF16)

    def put_keys(z, out_ref, head0, block_len):
        block = lax.shift_right_logical(seq_pos, block_len.bit_length() - 1)
        put_wide(z, out_ref, head0, jnp.where(lane - HEAD_DIM == block, 1.0, 0.0))

    def slab(idx):
        y = _dot(h, w_ref[:, idx * 2 * LANES:(idx + 1) * 2 * LANES])
        return y[:, :LANES], y[:, LANES:]

    a, b = slab(0)
    put_heads(rope(a), cmp_ref, 0, F32)
    put_heads(b, cmp_ref, 2, F32)
    a, b = slab(1)
    put_keys(rope(a), nks_ref, 0, SLC_LEN)
    put_heads(rope(b), nkw_ref, 0, BF16)
    a, b = slab(2)
    put_heads(a, fk_ref, 0, BF16)
    put_heads(b, fk_ref, 2, BF16)
    a, b = slab(3)
    put_keys(rope(a), mk_ref, 0, MOBA_BLOCK)
    put_keys(rope(b), mk_ref, 2, MOBA_BLOCK)
    f = _dot(h, w_ref[:, N_HEAD_COLS:])
    gm_ref[0] = jax.nn.log_sigmoid(f + fb_ref[...])


def _inproj(x2, mod, g, positions, inv_col, fb_lane, w_p, w_t, *, batch, seq, tm=512):
    n, d = x2.shape
    per_b = seq // tm

    def hm(nh, dtype, width=HEAD_DIM):
        return (pl.BlockSpec((1, nh, tm, width), lambda i: (i // per_b, 0, i % per_b, 0)),
                jax.ShapeDtypeStruct((batch, nh, seq, width), dtype))

    def tr(nh, rows, dtype):
        return (pl.BlockSpec((1, nh, rows, tm), lambda i: (i // per_b, 0, 0, i % per_b)),
                jax.ShapeDtypeStruct((batch, nh, rows, seq), dtype))

    specs = [tr(NSA_HEADS, HEAD_DIM, BF16), tr(4, LANES, BF16), tr(NSA_GROUPS, GATE_ROWS, F32),
             tr(FOX_HEADS, HEAD_DIM, BF16), tr(FOX_HEADS, LANES, BF16),
             tr(MOBA_HEADS, HEAD_DIM, BF16), tr(MOBA_HEADS, LANES, BF16),
             hm(4, F32), hm(2, BF16, LANES), hm(2, BF16), hm(FOX_HEADS, BF16), hm(MOBA_HEADS, BF16, LANES),
             (pl.BlockSpec((1, tm, LANES), lambda i: (i // per_b, i % per_b, 0)),
              jax.ShapeDtypeStruct((batch, seq, LANES), F32))]
    const = lambda shape: pl.BlockSpec(shape, lambda i: (0,) * len(shape))
    return pl.pallas_call(
        functools.partial(_inproj_kernel, per_b=per_b),
        grid=(n // tm,),
        in_specs=[pl.BlockSpec((tm, d), lambda i: (i, 0)),
                  pl.BlockSpec((1, 9, d), lambda i: (i // per_b, 0, 0)),
                  const((1, d)),
                  pl.BlockSpec((1, 1, tm), lambda i: (i, 0, 0)),
                  const((ROPE_DIM // 2, 1)), const((1, LANES)),
                  _resident(w_p.shape), _resident(w_t.shape)],
        out_specs=[s for s, _ in specs],
        out_shape=[s for _, s in specs],
        compiler_params=_params("parallel"),
        name="inproj",
    )(x2, mod, g, positions.reshape(n // tm, 1, tm), inv_col, fb_lane, w_p, w_t)


def _compress_kernel(z_ref, pos_ref, w1_ref, w2_ref, w2t_ref, o_ref, ot_ref):
    z = z_ref[0, 0]
    top = _dot((z + pos_ref[0, 0]).astype(BF16), w1_ref[0, 0])
    bot = _dot((z + pos_ref[0, 1]).astype(BF16), w1_ref[0, 1])
    nch = z.shape[0]
    hid = top + pltpu.roll(bot, nch - 1, 0)
    act = jax.nn.gelu(hid).astype(BF16)
    o_ref[0, 0] = _dot(act, w2_ref[0]).astype(o_ref.dtype)
    ot_ref[0, 0] = _dot_nt(w2t_ref[0], act).astype(ot_ref.dtype)


def _compress(cmp_in, pos_l, w1_l, w2_l, w2t_l):
    b, four, nch, width = cmp_in.shape
    return pl.pallas_call(
        _compress_kernel,
        grid=(b, four),
        in_specs=[pl.BlockSpec((1, 1, nch, width), lambda i, n: (i, n, 0, 0)),
                  pl.BlockSpec((1, 2, 1, width), lambda i, n: (n // 2, 0, 0, 0)),
                  pl.BlockSpec((1, 2, width, CMP_HIDDEN), lambda i, n: (n // 2, 0, 0, 0)),
                  pl.BlockSpec((1, CMP_HIDDEN, HEAD_DIM), lambda i, n: (n // 2, 0, 0)),
                  pl.BlockSpec((1, HEAD_DIM, CMP_HIDDEN), lambda i, n: (n // 2, 0, 0))],
        out_specs=[pl.BlockSpec((1, 1, nch, HEAD_DIM), lambda i, n: (i, n, 0, 0)),
                   pl.BlockSpec((1, 1, HEAD_DIM, nch), lambda i, n: (i, n, 0, 0))],
        out_shape=[jax.ShapeDtypeStruct((b, four, nch, HEAD_DIM), BF16),
                   jax.ShapeDtypeStruct((b, four, HEAD_DIM, nch), BF16)],
        compiler_params=_params("parallel", "parallel"),
        name="nsa_compress",
    )(cmp_in, pos_l, w1_l, w2_l, w2t_l)


def _chunk_update_t(carry, s, v_t):
    m, acc = carry
    m_new = jnp.maximum(m, jnp.max(s, axis=0, keepdims=True))
    p = jnp.exp(s - m_new).astype(BF16)
    return m_new, jnp.exp(m - m_new) * acc + _dot(v_t, p)


def _chunk_init_t(n):
    return jnp.full((1, n), NEG, F32), jnp.zeros((LANES, n), F32)


def _chunk_finish_t(carry):
    _, acc = carry
    return acc[:HEAD_DIM] / jnp.maximum(acc[HEAD_DIM:HEAD_DIM + 1], 1e-30)


def _heads_to_rows(heads_t):
    pairs = [jnp.concatenate(heads_t[a:a + 2], axis=0).T for a in range(0, len(heads_t), 2)]
    return jnp.concatenate(pairs, axis=1)


def _nsa_kernel(qt_ref, kc_ref, vct_ref, ks_ref, vst_ref, kw_ref, vwt_ref, gt_ref, ovt_ref, o_ref, *, tq, tk):
    seq = ks_ref.shape[2]
    for i in range(seq // tq):
        q_cols = slice(i * tq, (i + 1) * tq)
        _nsa_tile(i, qt_ref.at[:, :, :, q_cols], kc_ref, vct_ref, ks_ref, vst_ref, kw_ref, vwt_ref,
                  gt_ref.at[:, :, :, q_cols], ovt_ref, o_ref.at[:, q_cols, :], tq=tq, tk=tk)


def _nsa_tile(i, qt_ref, kc_ref, vct_ref, ks_ref, vst_ref, kw_ref, vwt_ref, gt_ref, ovt_ref, o_ref, *, tq, tk):
    rep = NSA_REP
    groups = range(qt_ref.shape[1] // rep)
    n_c = kc_ref.shape[2]
    n_s = ks_ref.shape[2] // SLC_LEN
    tile = lambda a: jnp.concatenate([a] * rep, axis=1)
    t_q = i * tq + _iota((1, tq), 1)
    cend = _iota((n_c, 1), 0) * CMP_STRIDE + (CMP_LEN - 1)
    cm = cend <= tile(t_q)
    j_t = _iota((n_s, tq), 0)
    tb = lax.shift_right_logical(t_q, SLC_LEN.bit_length() - 1)
    valid = j_t <= tb
    forced = (j_t == 0) | (j_t == tb) | (j_t == tb - 1)
    start = max(i * tq - WIN, 0)
    span = (i + 1) * tq - start
    dist = t_q - (start + _iota((span, 1), 0))
    band = tile(jnp.where((dist >= 0) & (dist < WIN), 0.0, NEG))

    o_cmp, o_win, q4a = [], [], []
    for g in groups:
        q4 = jnp.concatenate([qt_ref[0, rep * g + r] for r in range(rep)], axis=1)
        s = jnp.where(cm, _dot(kc_ref[0, g], q4), NEG)
        e = jnp.where(cm, jnp.exp(s - jnp.max(s, axis=0, keepdims=True)), 0.0)
        p = e / jnp.maximum(jnp.sum(e, axis=0, keepdims=True), 1e-30)
        o_cmp.append(_dot(vct_ref[0, g], p.astype(BF16)))
        psum = p[:, 0:tq] + p[:, tq:2 * tq] + p[:, 2 * tq:3 * tq] + p[:, 3 * tq:4 * tq]
        imp = jnp.dot(ovt_ref[...], psum, precision=HIGHEST, preferred_element_type=F32)[:n_s]
        v_t = jnp.where(valid, jnp.where(forced, jnp.inf, imp), -jnp.inf)
        rank = jnp.zeros((n_s, tq), jnp.int32)
        for jp in range(n_s):
            row = v_t[jp:jp + 1, :]
            ahead = (row > v_t) | ((row == v_t) & (j_t > jp))
            rank = rank + ahead.astype(jnp.int32)
        neg_t = jnp.where((rank < SLC_TOPN) & (v_t > -jnp.inf), 0.0, NEG)
        neg = jnp.concatenate([neg_t, jnp.zeros((HEAD_DIM - n_s, tq), F32)], axis=0)
        q4a.append(jnp.concatenate([q4, tile(neg).astype(BF16)], axis=0))
        sw = _dot(kw_ref[0, g, start:start + span, :], q4) + band
        pw = jnp.exp(sw - jnp.max(sw, axis=0, keepdims=True)).astype(BF16)
        aw = _dot(vwt_ref[0, g, :, start:start + span], pw)
        o_win.append(aw[:HEAD_DIM] / jnp.maximum(aw[HEAD_DIM:HEAD_DIM + 1], 1e-30))

    def slc_step(k0, k1, carries, diag):
        new = []
        for g in groups:
            sc = _dot(ks_ref[0, g, k0:k1, :], q4a[g])
            if diag:
                sc = sc + tile(jnp.where(k0 + _iota((k1 - k0, 1), 0) <= t_q, 0.0, NEG))
            new.append(_chunk_update_t(carries[g], sc, vst_ref[0, g, :, k0:k1]))
        return tuple(new)

    end = (i + 1) * tq
    last = (end - 1) // tk * tk
    carries = slc_step(last, end, tuple(_chunk_init_t(rep * tq) for _ in groups), True)
    for k0 in range(0, last, tk):
        carries = slc_step(k0, k0 + tk, carries, False)

    heads = []
    for g in groups:
        o_slc = _chunk_finish_t(carries[g])
        gates = gt_ref[0, g]
        for r in range(rep):
            cols = slice(r * tq, (r + 1) * tq)
            heads.append(gates[3 * r:3 * r + 1] * o_cmp[g][:, cols]
                         + gates[3 * r + 1:3 * r + 2] * o_slc[:, cols]
                         + gates[3 * r + 2:3 * r + 3] * o_win[g][:, cols])
    o_ref[0] = _heads_to_rows(heads).astype(o_ref.dtype)


def _nsa(qt, cmp_kv, cmp_t, nks, nkw, nvt, gt, ovt, *, tq=256, tk=512):
    b, _, _, seq = qt.shape
    n_c = cmp_kv.shape[2]
    ng = NSA_GROUPS
    return pl.pallas_call(
        functools.partial(_nsa_kernel, tq=tq, tk=tk),
        grid=(b, ng),
        in_specs=[pl.BlockSpec((1, NSA_REP, HEAD_DIM, seq), lambda bi, g: (bi, g, 0, 0)),
                  pl.BlockSpec((1, 1, n_c, HEAD_DIM), lambda bi, g: (bi, g, 0, 0)),
                  pl.BlockSpec((1, 1, HEAD_DIM, n_c), lambda bi, g: (bi, ng + g, 0, 0)),
                  pl.BlockSpec((1, 1, seq, LANES), lambda bi, g: (bi, g, 0, 0)),
                  pl.BlockSpec((1, 1, LANES, seq), lambda bi, g: (bi, g, 0, 0)),
                  pl.BlockSpec((1, 1, seq, HEAD_DIM), lambda bi, g: (bi, g, 0, 0)),
                  pl.BlockSpec((1, 1, LANES, seq), lambda bi, g: (bi, ng + g, 0, 0)),
                  pl.BlockSpec((1, 1, GATE_ROWS, seq), lambda bi, g: (bi, g, 0, 0)),
                  pl.BlockSpec((LANES, n_c), lambda bi, g: (0, 0))],
        out_specs=pl.BlockSpec((1, seq, NSA_REP * HEAD_DIM), lambda bi, g: (bi, 0, g)),
        out_shape=jax.ShapeDtypeStruct((b, seq, NSA_HEADS * HEAD_DIM), BF16),
        compiler_params=_params("parallel", "parallel"),
        name="nsa",
    )(qt, cmp_kv, cmp_t, nks, nvt, nkw, nvt, gt, ovt)


def _fox_kernel(qt_ref, k_ref, vt_ref, lf_ref, cum_ref, tri_ref, o_ref, fcol_ref, frow_ref, *, tq):
    heads = range(FOX_HEADS)
    seq = lf_ref.shape[1]
    blk = cum_ref.shape[0]
    carry = jnp.zeros((1, LANES), F32)
    for c in range(seq // blk):
        rows = slice(c * blk, (c + 1) * blk)
        cs = jnp.dot(cum_ref[...], lf_ref[0, rows, :], precision=HIGHEST,
                     preferred_element_type=F32) + carry
        fcol_ref[rows, :] = cs
        carry = cs[blk - 1:blk, :]
    frow_ref[...] = fcol_ref[...].T[_F_SLAB:_F_SLAB + SUBLANES, :]

    tk = tri_ref.shape[1]
    per_chunk = tk // tq
    for i in range(seq // tq):
        q_cols = slice(i * tq, (i + 1) * tq)
        qs = [qt_ref[0, h, :, q_cols] for h in heads]
        fqs = [frow_ref[_F_ROW0 + h:_F_ROW0 + h + 1, q_cols] for h in heads]

        def step(c, carries, diag, qs=qs, fqs=fqs, i=i):
            keys = slice(c * tk, (c + 1) * tk)
            new = []
            for h in heads:
                fk = fcol_ref[keys, GATE_LANES + h:GATE_LANES + h + 1]
                sc = _dot(k_ref[0, h, keys, :], qs[h]) + (fqs[h] - fk)
                if diag:
                    sc = sc + tri_ref[i % per_chunk]
                new.append(_chunk_update_t(carries[h], sc, vt_ref[0, h, :, keys]))
            return tuple(new)

        carries = step(i // per_chunk, tuple(_chunk_init_t(tq) for _ in heads), True)
        for c in range(i // per_chunk):
            carries = step(c, carries, False)
        o_ref[0, q_cols, :] = _heads_to_rows([_chunk_finish_t(cr) for cr in carries]).astype(o_ref.dtype)


def _fox(fqt, fk, fvt, gm, cum, tri_t, *, tq=FOX_TILE):
    b, nh, _, seq = fqt.shape
    blk = cum.shape[0]
    return pl.pallas_call(
        functools.partial(_fox_kernel, tq=tq),
        grid=(b,),
        in_specs=[pl.BlockSpec((1, nh, HEAD_DIM, seq), lambda bi: (bi, 0, 0, 0)),
                  pl.BlockSpec((1, nh, seq, HEAD_DIM), lambda bi: (bi, 0, 0, 0)),
                  pl.BlockSpec((1, nh, LANES, seq), lambda bi: (bi, 0, 0, 0)),
                  pl.BlockSpec((1, seq, LANES), lambda bi: (bi, 0, 0)),
                  pl.BlockSpec((blk, blk), lambda bi: (0, 0)),
                  pl.BlockSpec(tri_t.shape, lambda bi: (0, 0, 0))],
        out_specs=pl.BlockSpec((1, seq, nh * HEAD_DIM), lambda bi: (bi, 0, 0)),
        out_shape=jax.ShapeDtypeStruct((b, seq, nh * HEAD_DIM), BF16),
        scratch_shapes=[pltpu.VMEM((seq, LANES), F32), pltpu.VMEM((8, seq), F32)],
        compiler_params=_params("parallel"),
        name="fox",
    )(fqt, fk, fvt, gm, cum, tri_t)


def _moba_kernel(qt_ref, k_ref, vt_ref, avg_ref, tri_ref, o_ref, *, tq):
    heads = range(qt_ref.shape[1])
    nb = avg_ref.shape[0]
    seq = k_ref.shape[2]
    kms = [_dot(avg_ref[...], k_ref[0, h])[:, :HEAD_DIM] for h in heads]
    blk = _iota((nb, tq), 0)
    in_tile = lax.shift_right_logical(_iota((nb, tq), 1), MOBA_BLOCK.bit_length() - 1)

    for i in range(seq // tq):
        q_cols = slice(i * tq, (i + 1) * tq)
        own = i * (tq // MOBA_BLOCK) + in_tile
        qas = []
        for h in heads:
            q = qt_ref[0, h, :, q_cols]
            gate = jnp.dot(kms[h], q.astype(F32), precision=HIGHEST, preferred_element_type=F32)
            val = jnp.where(blk < own, gate, -jnp.inf)
            rank = jnp.zeros((nb, tq), jnp.int32)
            for jp in range(nb):
                row = val[jp:jp + 1, :]
                ahead = (row > val) | ((row == val) & (blk > jp))
                rank = rank + ahead.astype(jnp.int32)
            keep = ((rank < MOBA_TOPK) & (blk < own)) | (blk == own)
            neg = jnp.concatenate([jnp.where(keep, 0.0, NEG), jnp.zeros((HEAD_DIM - nb, tq), F32)], axis=0)
            qas.append(jnp.concatenate([q, neg.astype(BF16)], axis=0))

        def step(c, carries, diag, qas=qas):
            keys = slice(c * tq, (c + 1) * tq)
            new = []
            for h in heads:
                sc = _dot(k_ref[0, h, keys, :], qas[h])
                if diag:
                    sc = sc + tri_ref[...]
                new.append(_chunk_update_t(carries[h], sc, vt_ref[0, h, :, keys]))
            return tuple(new)

        carries = step(i, tuple(_chunk_init_t(tq) for _ in heads), True)
        for c in range(i):
            carries = step(c, carries, False)
        o_ref[0, q_cols, :] = _heads_to_rows([_chunk_finish_t(cr) for cr in carries]).astype(o_ref.dtype)


def _moba(mqt, mk, mvt, avg, tri_t, *, tq=MOBA_TILE):
    b, nh, _, seq = mqt.shape
    nb = avg.shape[0]
    return pl.pallas_call(
        functools.partial(_moba_kernel, tq=tq),
        grid=(b, nh // 2),
        in_specs=[pl.BlockSpec((1, 2, HEAD_DIM, seq), lambda bi, hp: (bi, hp, 0, 0)),
                  pl.BlockSpec((1, 2, seq, LANES), lambda bi, hp: (bi, hp, 0, 0)),
                  pl.BlockSpec((1, 2, LANES, seq), lambda bi, hp: (bi, hp, 0, 0)),
                  pl.BlockSpec((nb, seq), lambda bi, hp: (0, 0)),
                  pl.BlockSpec((tq, tq), lambda bi, hp: (0, 0))],
        out_specs=pl.BlockSpec((1, seq, 2 * HEAD_DIM), lambda bi, hp: (bi, 0, hp)),
        out_shape=jax.ShapeDtypeStruct((b, seq, nh * HEAD_DIM), BF16),
        compiler_params=_params("parallel", "parallel"),
        name="moba",
    )(mqt, mk, mvt, avg, tri_t)


def _pack_w_in(w_in):
    depth, d, _ = w_in.shape

    def cols(a, b):
        return w_in[:, :, a:b]

    zeros = lambda n: jnp.zeros((depth, d, n), w_in.dtype)
    g_half = 3 * NSA_REP
    token_major = [cols(_OFF_KC, _OFF_KS),
                   cols(_OFF_KS, _OFF_VS), cols(_OFF_KW, _OFF_VW),
                   cols(_OFF_FOX_K, _OFF_FOX_V),
                   cols(_OFF_MOBA_K, _OFF_MOBA_V),
                   zeros(GATE_LANES), cols(_OFF_FOX_F, _OFF_FOX_F + FOX_HEADS),
                   zeros(LANES - GATE_LANES - FOX_HEADS)]
    transposed = [cols(_OFF_NSA_Q, _OFF_KC),
                  cols(_OFF_VS, _OFF_KW), cols(_OFF_VW, _OFF_GATE),
                  cols(_OFF_GATE, _OFF_GATE + g_half), zeros(GATE_ROWS - g_half),
                  cols(_OFF_GATE + g_half, _OFF_GATE + 2 * g_half), zeros(LANES - GATE_ROWS - g_half),
                  cols(_OFF_FOX_Q, _OFF_FOX_K), cols(_OFF_FOX_V, _OFF_FOX_F),
                  cols(_OFF_MOBA_Q, _OFF_MOBA_K), cols(_OFF_MOBA_V, IN_COLS)]
    w_p = jnp.concatenate(token_major, axis=-1).astype(BF16)
    w_t = jnp.swapaxes(jnp.concatenate(transposed, axis=-1), 1, 2).astype(BF16)
    return w_p, w_t


def _constants(seq):
    n_c = seq // CMP_STRIDE
    n_s = seq // SLC_LEN
    c0 = np.arange(n_c)[None, :] * CMP_STRIDE
    j0 = np.arange(LANES)[:, None] * SLC_LEN
    real = (np.arange(n_c)[None, :] < (seq - CMP_LEN) // CMP_STRIDE + 1) & (np.arange(LANES)[:, None] < n_s)
    ov = ((c0 < j0 + SLC_LEN) & (c0 + CMP_LEN > j0) & real).astype(np.float32)
    nb = seq // MOBA_BLOCK
    avg = (np.arange(nb)[:, None] == (np.arange(seq)[None, :] // MOBA_BLOCK)).astype(np.float32) / MOBA_BLOCK
    tri = np.tril(np.ones((256, 256), np.float32))

    kk = np.arange(FOX_CHUNK)[None, :, None]
    qq = np.arange(FOX_TILE)[None, None, :] + FOX_TILE * np.arange(FOX_CHUNK // FOX_TILE)[:, None, None]
    causal_fox = np.where(kk <= qq, 0.0, NEG).astype(np.float32)
    r = np.arange(MOBA_TILE)
    same = (r[:, None] // MOBA_BLOCK) == (r[None, :] // MOBA_BLOCK)
    allowed = np.where(same, r[:, None] <= r[None, :], r[:, None] < r[None, :])
    causal_moba = np.where(allowed, 0.0, NEG).astype(np.float32)
    return dict(ov=jnp.asarray(ov), avg=jnp.asarray(avg, dtype=BF16), tri=jnp.asarray(tri),
                causal_fox=jnp.asarray(causal_fox), causal_moba=jnp.asarray(causal_moba))


def kernel(x, c, positions, norm_g, w_ada, b_ada, w_in, fox_fbias, cmp_pos, cmp_w1, cmp_w2, w_out,
           ffn_w13, ffn_w2, final_g):
    batch, seq, d = x.shape
    depth = w_ada.shape[0]
    assert d == D_MODEL and seq % MOBA_BLOCK == 0 and seq % 1024 == 0

    w_in_p, w_in_t = _pack_w_in(w_in)
    w13 = ffn_w13.astype(BF16)
    w2 = ffn_w2.astype(BF16)
    w_o = w_out.astype(BF16)
    half = CMP_LEN * HEAD_DIM // 2
    cw1 = cmp_w1.astype(BF16).reshape(depth, 2, 2, half, CMP_HIDDEN)
    cw2 = cmp_w2.astype(BF16)
    cw2t = jnp.swapaxes(cw2, 2, 3)
    cpos = cmp_pos.reshape(depth, 2, 2, 1, half)
    cst = _constants(seq)

    inv = ROPE_THETA ** (-jnp.arange(0, ROPE_DIM, 2, dtype=F32) / ROPE_DIM)
    inv_col = inv.reshape(ROPE_DIM // 2, 1)
    fb_lane = jnp.zeros((depth, 1, LANES), F32).at[:, 0, GATE_LANES:GATE_LANES + FOX_HEADS].set(fox_fbias)

    mod_all = _ada_mod(c, w_ada, b_ada).reshape(depth, batch, 9, d)
    x2 = x.reshape(batch * seq, d)
    fg = final_g.reshape(1, d)
    for l in range(depth):
        mod = mod_all[l]
        x2 = _ffn(x2, mod, norm_g[l, 0:1], fg, w13[l, 0], w2[l, 0], sub=0, seq=seq, final=False)
        qt, nvt, gt, fqt, fvt, mqt, mvt, cmp_in, nks, nkw, fk, mk, gm = _inproj(
            x2, mod, norm_g[l, 1:2], positions, inv_col, fb_lane[l], w_in_p[l], w_in_t[l],
            batch=batch, seq=seq)
        cmp_kv, cmp_t = _compress(cmp_in.reshape(batch, 4, seq // CMP_STRIDE, CMP_STRIDE * HEAD_DIM),
                                  cpos[l], cw1[l], cw2[l], cw2t[l])
        o_nsa = _nsa(qt, cmp_kv, cmp_t, nks, nkw, nvt, gt, cst["ov"])
        o_fox = _fox(fqt, fk, fvt, gm, cst["tri"], cst["causal_fox"])
        o_moba = _moba(mqt, mk, mvt, cst["avg"], cst["causal_moba"])
        x2 = _mix_ffn(x2, mod, norm_g[l, 2:3], fg, o_nsa.reshape(batch * seq, -1),
                      o_fox.reshape(batch * seq, -1), o_moba.reshape(batch * seq, -1), w_o[l],
                      w13[l, 1], w2[l, 1], seq=seq, final=(l == depth - 1))
    return x2.reshape(batch, seq, d)
```

```python
import functools

import numpy as np
import jax
import jax.numpy as jnp
from jax import lax
from jax.experimental import pallas as pl
from jax.experimental.pallas import tpu as pltpu

D_MODEL = 1024
HEAD_DIM = 64
NSA_HEADS = 8
NSA_GROUPS = 2
NSA_REP = NSA_HEADS // NSA_GROUPS
FOX_HEADS = 4
MOBA_HEADS = 4
ROPE_DIM = HEAD_DIM // 4
ROPE_THETA = 500000.0
CMP_LEN = 32
CMP_STRIDE = 16
CMP_HIDDEN = 4 * HEAD_DIM
SLC_LEN = 64
SLC_TOPN = 8
WIN = 512
MOBA_BLOCK = 256
MOBA_TOPK = 3
D_FF = 2816
EPS = 1e-6
ATT_SCALE = HEAD_DIM ** -0.5

FOX_TILE = 512
FOX_CHUNK = 512
MOBA_TILE = 2 * MOBA_BLOCK
LANES = 128
MXU_TILE = 256
NEG = -1e30
VMEM_LIMIT = 56 * 1024 * 1024

F32 = jnp.float32
BF16 = jnp.bfloat16
HIGHEST = lax.Precision.HIGHEST

_KV = NSA_GROUPS * HEAD_DIM
_OFF_NSA_Q = 0
_OFF_KC = _OFF_NSA_Q + NSA_HEADS * HEAD_DIM
_OFF_VC = _OFF_KC + _KV
_OFF_KS = _OFF_VC + _KV
_OFF_VS = _OFF_KS + _KV
_OFF_KW = _OFF_VS + _KV
_OFF_VW = _OFF_KW + _KV
_OFF_GATE = _OFF_VW + _KV
_OFF_FOX_Q = _OFF_GATE + 3 * NSA_HEADS
_OFF_FOX_K = _OFF_FOX_Q + FOX_HEADS * HEAD_DIM
_OFF_FOX_V = _OFF_FOX_K + FOX_HEADS * HEAD_DIM
_OFF_FOX_F = _OFF_FOX_V + FOX_HEADS * HEAD_DIM
_OFF_MOBA_Q = _OFF_FOX_F + FOX_HEADS
_OFF_MOBA_K = _OFF_MOBA_Q + MOBA_HEADS * HEAD_DIM
_OFF_MOBA_V = _OFF_MOBA_K + MOBA_HEADS * HEAD_DIM
IN_COLS = _OFF_MOBA_V + MOBA_HEADS * HEAD_DIM

N_HEAD_COLS = 4 * _KV + (FOX_HEADS + MOBA_HEADS) * HEAD_DIM
GATE_LANES = 3 * NSA_REP
PROJ_COLS = N_HEAD_COLS + LANES
GATE_ROWS = 16
SUBLANES = 8
_F_SLAB = GATE_LANES // SUBLANES * SUBLANES
_F_ROW0 = GATE_LANES - _F_SLAB
_T_NSA_Q = 0
_T_NSA_V = _T_NSA_Q + NSA_HEADS * HEAD_DIM
_T_GATE = _T_NSA_V + 4 * HEAD_DIM
_T_FOX_Q = _T_GATE + LANES
_T_FOX_V = _T_FOX_Q + FOX_HEADS * HEAD_DIM
_T_MOBA_Q = _T_FOX_V + FOX_HEADS * HEAD_DIM
_T_MOBA_V = _T_MOBA_Q + MOBA_HEADS * HEAD_DIM
T_ROWS = _T_MOBA_V + MOBA_HEADS * HEAD_DIM


def _dot(a, b):
    return jnp.dot(a, b, preferred_element_type=F32)


def _dot_nt(a, b, precision=None):
    return lax.dot_general(a, b, (((1,), (1,)), ((), ())), precision=precision,
                           preferred_element_type=F32)


def _iota(shape, dim):
    return lax.broadcasted_iota(jnp.int32, shape, dim)


def _params(*sem):
    return pltpu.CompilerParams(dimension_semantics=sem, vmem_limit_bytes=VMEM_LIMIT)


def _ada_norm(x, g, shift, scale):
    ms = jnp.mean(x * x, axis=-1, keepdims=True)
    y = x * lax.rsqrt(ms + EPS) * g
    return y * (1.0 + scale) + shift


def _ada_kernel(c_ref, w_ref, b_ref, o_ref):
    c = c_ref[...]
    ca = c * jax.nn.sigmoid(c)
    o_ref[0] = jnp.dot(ca, w_ref[0], precision=HIGHEST, preferred_element_type=F32) + b_ref[0]


def _ada_mod(c, w_ada, b_ada):
    depth, d, n = w_ada.shape
    b = c.shape[0]
    tn = n // 4
    return pl.pallas_call(
        _ada_kernel,
        grid=(depth, n // tn),
        in_specs=[pl.BlockSpec((b, d), lambda l, j: (0, 0)),
                  pl.BlockSpec((1, d, tn), lambda l, j: (l, 0, j)),
                  pl.BlockSpec((1, 1, tn), lambda l, j: (l, 0, j))],
        out_specs=pl.BlockSpec((1, b, tn), lambda l, j: (l, 0, j)),
        out_shape=jax.ShapeDtypeStruct((depth, b, n), F32),
        compiler_params=_params("parallel", "parallel"),
        name="ada_mod",
    )(c, w_ada, b_ada.reshape(depth, 1, n))


def _swiglu_residual(x, mod_ref, g_ref, fg_ref, w13_ref, w2_ref, *, sub, coef, final, splits):
    h = _ada_norm(x, g_ref[...], mod_ref[0, 3 * sub:3 * sub + 1, :],
                  mod_ref[0, 3 * sub + 1:3 * sub + 2, :]).astype(BF16)
    acc = None
    for lo, hi in splits:
        a = _dot(h, w13_ref[:, lo:hi])
        b = _dot(h, w13_ref[:, D_FF + lo:D_FF + hi])
        u = (a * jax.nn.sigmoid(a) * b).astype(BF16)
        part = _dot(u, w2_ref[lo:hi, :])
        acc = part if acc is None else acc + part
    y = x + (coef * mod_ref[0, 3 * sub + 2:3 * sub + 3, :]) * acc
    if final:
        y = y * lax.rsqrt(jnp.mean(y * y, axis=-1, keepdims=True) + EPS) * fg_ref[...]
    return y


def _ffn_kernel(x_ref, mod_ref, g_ref, fg_ref, w13_ref, w2_ref, o_ref, **kw):
    o_ref[...] = _swiglu_residual(x_ref[...], mod_ref, g_ref, fg_ref, w13_ref, w2_ref, **kw)


def _mix_ffn_kernel(x_ref, mod_ref, g_ref, fg_ref, on_ref, of_ref, om_ref, wo_ref, w13_ref, w2_ref, o_ref, **kw):
    n_w = NSA_HEADS * HEAD_DIM
    f_w = FOX_HEADS * HEAD_DIM
    mix = _dot(on_ref[...], wo_ref[0:n_w, :])
    mix += _dot(of_ref[...], wo_ref[n_w:n_w + f_w, :])
    mix += _dot(om_ref[...], wo_ref[n_w + f_w:, :])
    x = x_ref[...] + mod_ref[0, 5:6, :] * mix
    o_ref[...] = _swiglu_residual(x, mod_ref, g_ref, fg_ref, w13_ref, w2_ref, **kw)


def _resident(shape):
    return pl.BlockSpec(shape, lambda i: (0,) * len(shape), pipeline_mode=pl.Buffered(1))


def _hidden_splits(chunk):
    return tuple((lo, min(lo + chunk, D_FF)) for lo in range(0, D_FF, chunk))


def _ffn(x2, mod, g, final_g, w13, w2, *, sub, seq, final, tm=1024, chunk=3 * MXU_TILE):
    n, d = x2.shape
    per_b = seq // tm
    return pl.pallas_call(
        functools.partial(_ffn_kernel, sub=sub, coef=0.5, final=final, splits=_hidden_splits(chunk)),
        grid=(n // tm,),
        in_specs=[pl.BlockSpec((tm, d), lambda i: (i, 0)),
                  pl.BlockSpec((1, 9, d), lambda i: (i // per_b, 0, 0)),
                  pl.BlockSpec((1, d), lambda i: (0, 0)),
                  pl.BlockSpec((1, d), lambda i: (0, 0)),
                  _resident((d, 2 * D_FF)), _resident((D_FF, d))],
        out_specs=pl.BlockSpec((tm, d), lambda i: (i, 0)),
        out_shape=jax.ShapeDtypeStruct((n, d), F32),
        compiler_params=_params("parallel"),
        name="ffn",
    )(x2, mod, g, final_g, w13, w2)


def _mix_ffn(x2, mod, g, final_g, o_nsa, o_fox, o_moba, w_out, w13, w2, *, seq, final, tm=512,
             chunk=4 * MXU_TILE):
    n, d = x2.shape
    per_b = seq // tm
    row = lambda width: pl.BlockSpec((tm, width), lambda i: (i, 0))
    return pl.pallas_call(
        functools.partial(_mix_ffn_kernel, sub=2, coef=0.5, final=final, splits=_hidden_splits(chunk)),
        grid=(n // tm,),
        in_specs=[row(d),
                  pl.BlockSpec((1, 9, d), lambda i: (i // per_b, 0, 0)),
                  pl.BlockSpec((1, d), lambda i: (0, 0)),
                  pl.BlockSpec((1, d), lambda i: (0, 0)),
                  row(o_nsa.shape[1]), row(o_fox.shape[1]), row(o_moba.shape[1]),
                  _resident((d, d)), _resident((d, 2 * D_FF)), _resident((D_FF, d))],
        out_specs=row(d),
        out_shape=jax.ShapeDtypeStruct((n, d), F32),
        compiler_params=_params("parallel"),
        name="mix_ffn",
    )(x2, mod, g, final_g, o_nsa, o_fox, o_moba, w_out, w13, w2)


def _inproj_kernel(x_ref, mod_ref, g_ref, posr_ref, invc_ref, fb_ref, w_ref, wt_ref,
                   qt_ref, nvt_ref, gt_ref, fqt_ref, fvt_ref, mqt_ref, mvt_ref,
                   cmp_ref, nks_ref, nkw_ref, fk_ref, mk_ref, gm_ref, *, per_b):
    h = _ada_norm(x_ref[...], g_ref[...], mod_ref[0, 3:4, :], mod_ref[0, 4:5, :]).astype(BF16)
    tm = h.shape[0]
    seq_pos = (pl.program_id(0) % per_b) * tm + _iota((tm, LANES), 0)

    half = ROPE_DIM // 2
    ang_t = invc_ref[...] * posr_ref[0].astype(F32)
    cos_t = jnp.cos(ang_t)
    sin_t = jnp.sin(ang_t)
    ones_row = jnp.where(_iota((HEAD_DIM, tm), 0) == 0, 1.0, 0.0)
    slab_rows = 4 * HEAD_DIM

    def queries_t(row0, n_heads, out_ref, rotate):
        for s_idx in range(n_heads // 4):
            y_t = _dot_nt(wt_ref[row0 + slab_rows * s_idx:row0 + slab_rows * (s_idx + 1), :], h)
            for hh in range(4):
                blk = y_t[HEAD_DIM * hh:HEAD_DIM * (hh + 1)]
                if rotate:
                    r1, r2 = blk[:half], blk[half:2 * half]
                    blk = jnp.concatenate([r1 * cos_t - r2 * sin_t, r2 * cos_t + r1 * sin_t, blk[2 * half:]],
                                          axis=0)
                out_ref[0, 4 * s_idx + hh] = (blk * ATT_SCALE).astype(BF16)

    def values_t(row0, out_ref):
        y_t = _dot_nt(wt_ref[row0:row0 + slab_rows, :], h)
        for idx in range(4):
            out_ref[0, idx] = jnp.concatenate([y_t[HEAD_DIM * idx:HEAD_DIM * (idx + 1)], ones_row],
                                              axis=0).astype(BF16)

    queries_t(_T_NSA_Q, NSA_HEADS, qt_ref, True)
    values_t(_T_NSA_V, nvt_ref)
    sig_t = jax.nn.sigmoid(_dot_nt(wt_ref[_T_GATE:_T_GATE + LANES, :], h))
    for g in range(NSA_GROUPS):
        gt_ref[0, g] = sig_t[GATE_ROWS * g:GATE_ROWS * (g + 1)]
    queries_t(_T_FOX_Q, FOX_HEADS, fqt_ref, False)
    values_t(_T_FOX_V, fvt_ref)
    queries_t(_T_MOBA_Q, MOBA_HEADS, mqt_ref, True)
    values_t(_T_MOBA_V, mvt_ref)

    zeros_t = jnp.zeros((half, tm), F32)

    def lane_table(first, second, fill):
        head = jnp.concatenate([first, second, jnp.full((HEAD_DIM - ROPE_DIM, tm), fill, F32)], axis=0)
        return jnp.concatenate([head, head], axis=0).T

    cosl = lane_table(cos_t, cos_t, 1.0)
    sin_lo = lane_table(-sin_t, zeros_t, 0.0)
    sin_hi = lane_table(zeros_t, sin_t, 0.0)
    lane = _iota((tm, LANES), 1)

    def rope(y):
        return (y * cosl + pltpu.roll(y, LANES - ROPE_DIM // 2, 1) * sin_lo
                + pltpu.roll(y, ROPE_DIM // 2, 1) * sin_hi)

    def put_heads(z, out_ref, head0, dtype):
        z = z.astype(dtype)
        out_ref[0, head0] = z[:, :HEAD_DIM]
        out_ref[0, head0 + 1] = z[:, HEAD_DIM:]

    def put_wide(z, out_ref, head0, tail):
        out_ref[0, head0] = jnp.where(lane < HEAD_DIM, z, tail).astype(BF16)
        out_ref[0, head0 + 1] = jnp.where(lane < HEAD_DIM, pltpu.roll(z, HEAD_DIM, 1), tail).astype(BF16)

    def put_keys(z, out_ref, head0, block_len):
        block = lax.shift_right_logical(seq_pos, block_len.bit_length() - 1)
        put_wide(z, out_ref, head0, jnp.where(lane - HEAD_DIM == block, 1.0, 0.0))

    def slab(idx):
        y = _dot(h, w_ref[:, idx * 2 * LANES:(idx + 1) * 2 * LANES])
        return y[:, :LANES], y[:, LANES:]

    a, b = slab(0)
    put_heads(rope(a), cmp_ref, 0, F32)
    put_heads(b, cmp_ref, 2, F32)
    a, b = slab(1)
    put_keys(rope(a), nks_ref, 0, SLC_LEN)
    put_heads(rope(b), nkw_ref, 0, BF16)
    a, b = slab(2)
    put_heads(a, fk_ref, 0, BF16)
    put_heads(b, fk_ref, 2, BF16)
    a, b = slab(3)
    put_keys(rope(a), mk_ref, 0, MOBA_BLOCK)
    put_keys(rope(b), mk_ref, 2, MOBA_BLOCK)
    f = _dot(h, w_ref[:, N_HEAD_COLS:])
    gm_ref[0] = jax.nn.log_sigmoid(f + fb_ref[...])


def _inproj(x2, mod, g, positions, inv_col, fb_lane, w_p, w_t, *, batch, seq, tm=512):
    n, d = x2.shape
    per_b = seq // tm

    def hm(nh, dtype, width=HEAD_DIM):
        return (pl.BlockSpec((1, nh, tm, width), lambda i: (i // per_b, 0, i % per_b, 0)),
                jax.ShapeDtypeStruct((batch, nh, seq, width), dtype))

    def tr(nh, rows, dtype):
        return (pl.BlockSpec((1, nh, rows, tm), lambda i: (i // per_b, 0, 0, i % per_b)),
                jax.ShapeDtypeStruct((batch, nh, rows, seq), dtype))

    specs = [tr(NSA_HEADS, HEAD_DIM, BF16), tr(4, LANES, BF16), tr(NSA_GROUPS, GATE_ROWS, F32),
             tr(FOX_HEADS, HEAD_DIM, BF16), tr(FOX_HEADS, LANES, BF16),
             tr(MOBA_HEADS, HEAD_DIM, BF16), tr(MOBA_HEADS, LANES, BF16),
             hm(4, F32), hm(2, BF16, LANES), hm(2, BF16), hm(FOX_HEADS, BF16), hm(MOBA_HEADS, BF16, LANES),
             (pl.BlockSpec((1, tm, LANES), lambda i: (i // per_b, i % per_b, 0)),
              jax.ShapeDtypeStruct((batch, seq, LANES), F32))]
    const = lambda shape: pl.BlockSpec(shape, lambda i: (0,) * len(shape))
    return pl.pallas_call(
        functools.partial(_inproj_kernel, per_b=per_b),
        grid=(n // tm,),
        in_specs=[pl.BlockSpec((tm, d), lambda i: (i, 0)),
                  pl.BlockSpec((1, 9, d), lambda i: (i // per_b, 0, 0)),
                  const((1, d)),
                  pl.BlockSpec((1, 1, tm), lambda i: (i, 0, 0)),
                  const((ROPE_DIM // 2, 1)), const((1, LANES)),
                  _resident(w_p.shape), _resident(w_t.shape)],
        out_specs=[s for s, _ in specs],
        out_shape=[s for _, s in specs],
        compiler_params=_params("parallel"),
        name="inproj",
    )(x2, mod, g, positions.reshape(n // tm, 1, tm), inv_col, fb_lane, w_p, w_t)


def _compress_kernel(z_ref, pos_ref, w1_ref, w2_ref, w2t_ref, o_ref, ot_ref):
    z = z_ref[0, 0]
    top = _dot((z + pos_ref[0, 0]).astype(BF16), w1_ref[0, 0])
    bot = _dot((z + pos_ref[0, 1]).astype(BF16), w1_ref[0, 1])
    nch = z.shape[0]
    hid = top + pltpu.roll(bot, nch - 1, 0)
    act = jax.nn.gelu(hid).astype(BF16)
    o_ref[0, 0] = _dot(act, w2_ref[0]).astype(o_ref.dtype)
    ot_ref[0, 0] = _dot_nt(w2t_ref[0], act).astype(ot_ref.dtype)


def _compress(cmp_in, pos_l, w1_l, w2_l, w2t_l):
    b, four, nch, width = cmp_in.shape
    return pl.pallas_call(
        _compress_kernel,
        grid=(b, four),
        in_specs=[pl.BlockSpec((1, 1, nch, width), lambda i, n: (i, n, 0, 0)),
                  pl.BlockSpec((1, 2, 1, width), lambda i, n: (n // 2, 0, 0, 0)),
                  pl.BlockSpec((1, 2, width, CMP_HIDDEN), lambda i, n: (n // 2, 0, 0, 0)),
                  pl.BlockSpec((1, CMP_HIDDEN, HEAD_DIM), lambda i, n: (n // 2, 0, 0)),
                  pl.BlockSpec((1, HEAD_DIM, CMP_HIDDEN), lambda i, n: (n // 2, 0, 0))],
        out_specs=[pl.BlockSpec((1, 1, nch, HEAD_DIM), lambda i, n: (i, n, 0, 0)),
                   pl.BlockSpec((1, 1, HEAD_DIM, nch), lambda i, n: (i, n, 0, 0))],
        out_shape=[jax.ShapeDtypeStruct((b, four, nch, HEAD_DIM), BF16),
                   jax.ShapeDtypeStruct((b, four, HEAD_DIM, nch), BF16)],
        compiler_params=_params("parallel", "parallel"),
        name="nsa_compress",
    )(cmp_in, pos_l, w1_l, w2_l, w2t_l)


def _chunk_update_t(carry, s, v_t):
    m, acc = carry
    m_new = jnp.maximum(m, jnp.max(s, axis=0, keepdims=True))
    p = jnp.exp(s - m_new).astype(BF16)
    return m_new, jnp.exp(m - m_new) * acc + _dot(v_t, p)


def _chunk_init_t(n):
    return jnp.full((1, n), NEG, F32), jnp.zeros((LANES, n), F32)


def _chunk_finish_t(carry):
    _, acc = carry
    return acc[:HEAD_DIM] / jnp.maximum(acc[HEAD_DIM:HEAD_DIM + 1], 1e-30)


def _heads_to_rows(heads_t):
    pairs = [jnp.concatenate(heads_t[a:a + 2], axis=0).T for a in range(0, len(heads_t), 2)]
    return jnp.concatenate(pairs, axis=1)


def _nsa_kernel(qt_ref, zk_ref, zv_ref, cpos_ref, cw1_ref, cw2_ref, cw2t_ref, ks_ref, vst_ref, kw_ref, vwt_ref,
                gt_ref, ovt_ref, o_ref, *, tq, tk):
    seq = ks_ref.shape[2]

    def hidden(z_ref, kind):
        z = z_ref[0, 0]
        top = _dot((z + cpos_ref[kind, 0]).astype(BF16), cw1_ref[kind, 0])
        bot = _dot((z + cpos_ref[kind, 1]).astype(BF16), cw1_ref[kind, 1])
        return jax.nn.gelu(top + pltpu.roll(bot, z.shape[0] - 1, 0)).astype(BF16)

    kc = _dot(hidden(zk_ref, 0), cw2_ref[0]).astype(BF16)
    vct = _dot_nt(cw2t_ref[1], hidden(zv_ref, 1)).astype(BF16)
    for i in range(seq // tq):
        q_cols = slice(i * tq, (i + 1) * tq)
        _nsa_tile(i, qt_ref.at[:, :, :, q_cols], kc, vct, ks_ref, vst_ref, kw_ref, vwt_ref,
                  gt_ref.at[:, :, :, q_cols], ovt_ref, o_ref.at[:, q_cols, :], tq=tq, tk=tk)


def _nsa_tile(i, qt_ref, kc, vct, ks_ref, vst_ref, kw_ref, vwt_ref, gt_ref, ovt_ref, o_ref, *, tq, tk):
    rep = NSA_REP
    groups = range(qt_ref.shape[1] // rep)
    n_c = kc.shape[0]
    n_s = ks_ref.shape[2] // SLC_LEN
    tile = lambda a: jnp.concatenate([a] * rep, axis=1)
    t_q = i * tq + _iota((1, tq), 1)
    cend = _iota((n_c, 1), 0) * CMP_STRIDE + (CMP_LEN - 1)
    cm = cend <= tile(t_q)
    j_t = _iota((n_s, tq), 0)
    tb = lax.shift_right_logical(t_q, SLC_LEN.bit_length() - 1)
    valid = j_t <= tb
    forced = (j_t == 0) | (j_t == tb) | (j_t == tb - 1)
    start = max(i * tq - WIN, 0)
    span = (i + 1) * tq - start
    dist = t_q - (start + _iota((span, 1), 0))
    band = tile(jnp.where((dist >= 0) & (dist < WIN), 0.0, NEG))

    o_cmp, o_win, q4a = [], [], []
    for g in groups:
        q4 = jnp.concatenate([qt_ref[0, rep * g + r] for r in range(rep)], axis=1)
        s = jnp.where(cm, _dot(kc, q4), NEG)
        e = jnp.where(cm, jnp.exp(s - jnp.max(s, axis=0, keepdims=True)), 0.0)
        p = e / jnp.maximum(jnp.sum(e, axis=0, keepdims=True), 1e-30)
        o_cmp.append(_dot(vct, p.astype(BF16)))
        psum = p[:, 0:tq] + p[:, tq:2 * tq] + p[:, 2 * tq:3 * tq] + p[:, 3 * tq:4 * tq]
        imp = jnp.dot(ovt_ref[...], psum, precision=HIGHEST, preferred_element_type=F32)[:n_s]
        v_t = jnp.where(valid, jnp.where(forced, jnp.inf, imp), -jnp.inf)
        rank = jnp.zeros((n_s, tq), jnp.int32)
        for jp in range(n_s):
            row = v_t[jp:jp + 1, :]
            ahead = (row > v_t) | ((row == v_t) & (j_t > jp))
            rank = rank + ahead.astype(jnp.int32)
        neg_t = jnp.where((rank < SLC_TOPN) & (v_t > -jnp.inf), 0.0, NEG)
        neg = jnp.concatenate([neg_t, jnp.zeros((HEAD_DIM - n_s, tq), F32)], axis=0)
        q4a.append(jnp.concatenate([q4, tile(neg).astype(BF16)], axis=0))
        sw = _dot(kw_ref[0, g, start:start + span, :], q4) + band
        pw = jnp.exp(sw - jnp.max(sw, axis=0, keepdims=True)).astype(BF16)
        aw = _dot(vwt_ref[0, g, :, start:start + span], pw)
        o_win.append(aw[:HEAD_DIM] / jnp.maximum(aw[HEAD_DIM:HEAD_DIM + 1], 1e-30))

    def slc_step(k0, k1, carries, diag):
        new = []
        for g in groups:
            sc = _dot(ks_ref[0, g, k0:k1, :], q4a[g])
            if diag:
                sc = sc + tile(jnp.where(k0 + _iota((k1 - k0, 1), 0) <= t_q, 0.0, NEG))
            new.append(_chunk_update_t(carries[g], sc, vst_ref[0, g, :, k0:k1]))
        return tuple(new)

    end = (i + 1) * tq
    last = (end - 1) // tk * tk
    carries = slc_step(last, end, tuple(_chunk_init_t(rep * tq) for _ in groups), True)
    for k0 in range(0, last, tk):
        carries = slc_step(k0, k0 + tk, carries, False)

    heads = []
    for g in groups:
        o_slc = _chunk_finish_t(carries[g])
        gates = gt_ref[0, g]
        for r in range(rep):
            cols = slice(r * tq, (r + 1) * tq)
            heads.append(gates[3 * r:3 * r + 1] * o_cmp[g][:, cols]
                         + gates[3 * r + 1:3 * r + 2] * o_slc[:, cols]
                         + gates[3 * r + 2:3 * r + 3] * o_win[g][:, cols])
    o_ref[0] = _heads_to_rows(heads).astype(o_ref.dtype)


def _nsa(qt, cmp_in, cpos, cw1, cw2, cw2t, nks, nkw, nvt, gt, ovt, *, tq=256, tk=512):
    b, _, _, seq = qt.shape
    n_c, width = cmp_in.shape[2:]
    ng = NSA_GROUPS
    const = lambda a: pl.BlockSpec(a.shape, lambda bi, g: (0,) * a.ndim)
    return pl.pallas_call(
        functools.partial(_nsa_kernel, tq=tq, tk=tk),
        grid=(b, ng),
        in_specs=[pl.BlockSpec((1, NSA_REP, HEAD_DIM, seq), lambda bi, g: (bi, g, 0, 0)),
                  pl.BlockSpec((1, 1, n_c, width), lambda bi, g: (bi, g, 0, 0)),
                  pl.BlockSpec((1, 1, n_c, width), lambda bi, g: (bi, ng + g, 0, 0)),
                  const(cpos), const(cw1), const(cw2), const(cw2t),
                  pl.BlockSpec((1, 1, seq, LANES), lambda bi, g: (bi, g, 0, 0)),
                  pl.BlockSpec((1, 1, LANES, seq), lambda bi, g: (bi, g, 0, 0)),
                  pl.BlockSpec((1, 1, seq, HEAD_DIM), lambda bi, g: (bi, g, 0, 0)),
                  pl.BlockSpec((1, 1, LANES, seq), lambda bi, g: (bi, ng + g, 0, 0)),
                  pl.BlockSpec((1, 1, GATE_ROWS, seq), lambda bi, g: (bi, g, 0, 0)),
                  pl.BlockSpec((LANES, n_c), lambda bi, g: (0, 0))],
        out_specs=pl.BlockSpec((1, seq, NSA_REP * HEAD_DIM), lambda bi, g: (bi, 0, g)),
        out_shape=jax.ShapeDtypeStruct((b, seq, NSA_HEADS * HEAD_DIM), BF16),
        compiler_params=_params("parallel", "parallel"),
        name="nsa",
    )(qt, cmp_in, cmp_in, cpos, cw1, cw2, cw2t, nks, nvt, nkw, nvt, gt, ovt)


def _fox_kernel(qt_ref, k_ref, vt_ref, lf_ref, cum_ref, tri_ref, o_ref, fcol_ref, frow_ref, *, tq):
    heads = range(FOX_HEADS)
    seq = lf_ref.shape[1]
    blk = cum_ref.shape[0]
    carry = jnp.zeros((1, LANES), F32)
    for c in range(seq // blk):
        rows = slice(c * blk, (c + 1) * blk)
        cs = jnp.dot(cum_ref[...], lf_ref[0, rows, :], precision=HIGHEST,
                     preferred_element_type=F32) + carry
        fcol_ref[rows, :] = cs
        carry = cs[blk - 1:blk, :]
    frow_ref[...] = fcol_ref[...].T[_F_SLAB:_F_SLAB + SUBLANES, :]

    tk = tri_ref.shape[1]
    per_chunk = tk // tq
    for i in range(seq // tq):
        q_cols = slice(i * tq, (i + 1) * tq)
        qs = [qt_ref[0, h, :, q_cols] for h in heads]
        fqs = [frow_ref[_F_ROW0 + h:_F_ROW0 + h + 1, q_cols] for h in heads]

        def step(c, carries, diag, qs=qs, fqs=fqs, i=i):
            keys = slice(c * tk, (c + 1) * tk)
            new = []
            for h in heads:
                fk = fcol_ref[keys, GATE_LANES + h:GATE_LANES + h + 1]
                sc = _dot(k_ref[0, h, keys, :], qs[h]) + (fqs[h] - fk)
                if diag:
                    sc = sc + tri_ref[i % per_chunk]
                new.append(_chunk_update_t(carries[h], sc, vt_ref[0, h, :, keys]))
            return tuple(new)

        carries = step(i // per_chunk, tuple(_chunk_init_t(tq) for _ in heads), True)
        for c in range(i // per_chunk):
            carries = step(c, carries, False)
        o_ref[0, q_cols, :] = _heads_to_rows([_chunk_finish_t(cr) for cr in carries]).astype(o_ref.dtype)


def _fox(fqt, fk, fvt, gm, cum, tri_t, *, tq=FOX_TILE):
    b, nh, _, seq = fqt.shape
    blk = cum.shape[0]
    return pl.pallas_call(
        functools.partial(_fox_kernel, tq=tq),
        grid=(b,),
        in_specs=[pl.BlockSpec((1, nh, HEAD_DIM, seq), lambda bi: (bi, 0, 0, 0)),
                  pl.BlockSpec((1, nh, seq, HEAD_DIM), lambda bi: (bi, 0, 0, 0)),
                  pl.BlockSpec((1, nh, LANES, seq), lambda bi: (bi, 0, 0, 0)),
                  pl.BlockSpec((1, seq, LANES), lambda bi: (bi, 0, 0)),
                  pl.BlockSpec((blk, blk), lambda bi: (0, 0)),
                  pl.BlockSpec(tri_t.shape, lambda bi: (0, 0, 0))],
        out_specs=pl.BlockSpec((1, seq, nh * HEAD_DIM), lambda bi: (bi, 0, 0)),
        out_shape=jax.ShapeDtypeStruct((b, seq, nh * HEAD_DIM), BF16),
        scratch_shapes=[pltpu.VMEM((seq, LANES), F32), pltpu.VMEM((8, seq), F32)],
        compiler_params=_params("parallel"),
        name="fox",
    )(fqt, fk, fvt, gm, cum, tri_t)


def _moba_kernel(qt_ref, k_ref, vt_ref, avg_ref, tri_ref, o_ref, *, tq):
    heads = range(MOBA_HEADS)
    nb = avg_ref.shape[0]
    seq = k_ref.shape[2]
    kms = [_dot(avg_ref[...], k_ref[0, h])[:, :HEAD_DIM] for h in heads]
    blk = _iota((nb, tq), 0)
    in_tile = lax.shift_right_logical(_iota((nb, tq), 1), MOBA_BLOCK.bit_length() - 1)

    for i in range(seq // tq):
        q_cols = slice(i * tq, (i + 1) * tq)
        own = i * (tq // MOBA_BLOCK) + in_tile
        qas = []
        for h in heads:
            q = qt_ref[0, h, :, q_cols]
            gate = jnp.dot(kms[h], q.astype(F32), precision=HIGHEST, preferred_element_type=F32)
            val = jnp.where(blk < own, gate, -jnp.inf)
            rank = jnp.zeros((nb, tq), jnp.int32)
            for jp in range(nb):
                row = val[jp:jp + 1, :]
                ahead = (row > val) | ((row == val) & (blk > jp))
                rank = rank + ahead.astype(jnp.int32)
            keep = ((rank < MOBA_TOPK) & (blk < own)) | (blk == own)
            neg = jnp.concatenate([jnp.where(keep, 0.0, NEG), jnp.zeros((HEAD_DIM - nb, tq), F32)], axis=0)
            qas.append(jnp.concatenate([q, neg.astype(BF16)], axis=0))

        def step(c, carries, diag, qas=qas):
            keys = slice(c * tq, (c + 1) * tq)
            new = []
            for h in heads:
                sc = _dot(k_ref[0, h, keys, :], qas[h])
                if diag:
                    sc = sc + tri_ref[...]
                new.append(_chunk_update_t(carries[h], sc, vt_ref[0, h, :, keys]))
            return tuple(new)

        carries = step(i, tuple(_chunk_init_t(tq) for _ in heads), True)
        for c in range(i):
            carries = step(c, carries, False)
        o_ref[0, q_cols, :] = _heads_to_rows([_chunk_finish_t(cr) for cr in carries]).astype(o_ref.dtype)


def _moba(mqt, mk, mvt, avg, tri_t, *, tq=MOBA_TILE):
    b, nh, _, seq = mqt.shape
    nb = avg.shape[0]
    return pl.pallas_call(
        functools.partial(_moba_kernel, tq=tq),
        grid=(b,),
        in_specs=[pl.BlockSpec((1, nh, HEAD_DIM, seq), lambda bi: (bi, 0, 0, 0)),
                  pl.BlockSpec((1, nh, seq, LANES), lambda bi: (bi, 0, 0, 0)),
                  pl.BlockSpec((1, nh, LANES, seq), lambda bi: (bi, 0, 0, 0)),
                  pl.BlockSpec((nb, seq), lambda bi: (0, 0)),
                  pl.BlockSpec((tq, tq), lambda bi: (0, 0))],
        out_specs=pl.BlockSpec((1, seq, nh * HEAD_DIM), lambda bi: (bi, 0, 0)),
        out_shape=jax.ShapeDtypeStruct((b, seq, nh * HEAD_DIM), BF16),
        compiler_params=_params("parallel"),
        name="moba",
    )(mqt, mk, mvt, avg, tri_t)


def _pack_w_in(w_in):
    depth, d, _ = w_in.shape

    def cols(a, b):
        return w_in[:, :, a:b]

    zeros = lambda n: jnp.zeros((depth, d, n), w_in.dtype)
    g_half = 3 * NSA_REP
    token_major = [cols(_OFF_KC, _OFF_KS),
                   cols(_OFF_KS, _OFF_VS), cols(_OFF_KW, _OFF_VW),
                   cols(_OFF_FOX_K, _OFF_FOX_V),
                   cols(_OFF_MOBA_K, _OFF_MOBA_V),
                   zeros(GATE_LANES), cols(_OFF_FOX_F, _OFF_FOX_F + FOX_HEADS),
                   zeros(LANES - GATE_LANES - FOX_HEADS)]
    transposed = [cols(_OFF_NSA_Q, _OFF_KC),
                  cols(_OFF_VS, _OFF_KW), cols(_OFF_VW, _OFF_GATE),
                  cols(_OFF_GATE, _OFF_GATE + g_half), zeros(GATE_ROWS - g_half),
                  cols(_OFF_GATE + g_half, _OFF_GATE + 2 * g_half), zeros(LANES - GATE_ROWS - g_half),
                  cols(_OFF_FOX_Q, _OFF_FOX_K), cols(_OFF_FOX_V, _OFF_FOX_F),
                  cols(_OFF_MOBA_Q, _OFF_MOBA_K), cols(_OFF_MOBA_V, IN_COLS)]
    w_p = jnp.concatenate(token_major, axis=-1).astype(BF16)
    w_t = jnp.swapaxes(jnp.concatenate(transposed, axis=-1), 1, 2).astype(BF16)
    return w_p, w_t


def _constants(seq):
    n_c = seq // CMP_STRIDE
    n_s = seq // SLC_LEN
    c0 = np.arange(n_c)[None, :] * CMP_STRIDE
    j0 = np.arange(LANES)[:, None] * SLC_LEN
    real = (np.arange(n_c)[None, :] < (seq - CMP_LEN) // CMP_STRIDE + 1) & (np.arange(LANES)[:, None] < n_s)
    ov = ((c0 < j0 + SLC_LEN) & (c0 + CMP_LEN > j0) & real).astype(np.float32)
    nb = seq // MOBA_BLOCK
    avg = (np.arange(nb)[:, None] == (np.arange(seq)[None, :] // MOBA_BLOCK)).astype(np.float32) / MOBA_BLOCK
    tri = np.tril(np.ones((256, 256), np.float32))

    kk = np.arange(FOX_CHUNK)[None, :, None]
    qq = np.arange(FOX_TILE)[None, None, :] + FOX_TILE * np.arange(FOX_CHUNK // FOX_TILE)[:, None, None]
    causal_fox = np.where(kk <= qq, 0.0, NEG).astype(np.float32)
    r = np.arange(MOBA_TILE)
    same = (r[:, None] // MOBA_BLOCK) == (r[None, :] // MOBA_BLOCK)
    allowed = np.where(same, r[:, None] <= r[None, :], r[:, None] < r[None, :])
    causal_moba = np.where(allowed, 0.0, NEG).astype(np.float32)
    return dict(ov=jnp.asarray(ov), avg=jnp.asarray(avg, dtype=BF16), tri=jnp.asarray(tri),
                causal_fox=jnp.asarray(causal_fox), causal_moba=jnp.asarray(causal_moba))


def kernel(x, c, positions, norm_g, w_ada, b_ada, w_in, fox_fbias, cmp_pos, cmp_w1, cmp_w2, w_out,
           ffn_w13, ffn_w2, final_g):
    batch, seq, d = x.shape
    depth = w_ada.shape[0]
    assert d == D_MODEL and seq % MOBA_BLOCK == 0 and seq % 1024 == 0

    w_in_p, w_in_t = _pack_w_in(w_in)
    w13 = ffn_w13.astype(BF16)
    w2 = ffn_w2.astype(BF16)
    w_o = w_out.astype(BF16)
    half = CMP_LEN * HEAD_DIM // 2
    cw1 = cmp_w1.astype(BF16).reshape(depth, 2, 2, half, CMP_HIDDEN)
    cw2 = cmp_w2.astype(BF16)
    cw2t = jnp.swapaxes(cw2, 2, 3)
    cpos = cmp_pos.reshape(depth, 2, 2, 1, half)
    cst = _constants(seq)

    inv = ROPE_THETA ** (-jnp.arange(0, ROPE_DIM, 2, dtype=F32) / ROPE_DIM)
    inv_col = inv.reshape(ROPE_DIM // 2, 1)
    fb_lane = jnp.zeros((depth, 1, LANES), F32).at[:, 0, GATE_LANES:GATE_LANES + FOX_HEADS].set(fox_fbias)

    mod_all = _ada_mod(c, w_ada, b_ada).reshape(depth, batch, 9, d)
    x2 = x.reshape(batch * seq, d)
    fg = final_g.reshape(1, d)
    for l in range(depth):
        mod = mod_all[l]
        x2 = _ffn(x2, mod, norm_g[l, 0:1], fg, w13[l, 0], w2[l, 0], sub=0, seq=seq, final=False)
        qt, nvt, gt, fqt, fvt, mqt, mvt, cmp_in, nks, nkw, fk, mk, gm = _inproj(
            x2, mod, norm_g[l, 1:2], positions, inv_col, fb_lane[l], w_in_p[l], w_in_t[l],
            batch=batch, seq=seq)
        o_nsa = _nsa(qt, cmp_in.reshape(batch, 4, seq // CMP_STRIDE, CMP_STRIDE * HEAD_DIM),
                     cpos[l], cw1[l], cw2[l], cw2t[l], nks, nkw, nvt, gt, cst["ov"])
        o_fox = _fox(fqt, fk, fvt, gm, cst["tri"], cst["causal_fox"])
        o_moba = _moba(mqt, mk, mvt, cst["avg"], cst["causal_moba"])
        x2 = _mix_ffn(x2, mod, norm_g[l, 2:3], fg, o_nsa.reshape(batch * seq, -1),
                      o_fox.reshape(batch * seq, -1), o_moba.reshape(batch * seq, -1), w_o[l],
                      w13[l, 1], w2[l, 1], seq=seq, final=(l == depth - 1))
    return x2.reshape(batch, seq, d)
```

```python
import functools

import numpy as np
import jax
import jax.numpy as jnp
from jax import lax
from jax.experimental import pallas as pl
from jax.experimental.pallas import tpu as pltpu

D_MODEL = 1024
HEAD_DIM = 64
NSA_HEADS = 8
NSA_GROUPS = 2
NSA_REP = NSA_HEADS // NSA_GROUPS
FOX_HEADS = 4
MOBA_HEADS = 4
ROPE_DIM = HEAD_DIM // 4
ROPE_THETA = 500000.0
CMP_LEN = 32
CMP_STRIDE = 16
CMP_HIDDEN = 4 * HEAD_DIM
SLC_LEN = 64
SLC_TOPN = 8
WIN = 512
MOBA_BLOCK = 256
MOBA_TOPK = 3
D_FF = 2816
EPS = 1e-6
ATT_SCALE = HEAD_DIM ** -0.5

FOX_TILE = 512
FOX_CHUNK = 512
MOBA_TILE = 2 * MOBA_BLOCK
LANES = 128
MXU_TILE = 256
NEG = -1e30
VMEM_LIMIT = 56 * 1024 * 1024

F32 = jnp.float32
BF16 = jnp.bfloat16
HIGHEST = lax.Precision.HIGHEST

_KV = NSA_GROUPS * HEAD_DIM
_OFF_NSA_Q = 0
_OFF_KC = _OFF_NSA_Q + NSA_HEADS * HEAD_DIM
_OFF_VC = _OFF_KC + _KV
_OFF_KS = _OFF_VC + _KV
_OFF_VS = _OFF_KS + _KV
_OFF_KW = _OFF_VS + _KV
_OFF_VW = _OFF_KW + _KV
_OFF_GATE = _OFF_VW + _KV
_OFF_FOX_Q = _OFF_GATE + 3 * NSA_HEADS
_OFF_FOX_K = _OFF_FOX_Q + FOX_HEADS * HEAD_DIM
_OFF_FOX_V = _OFF_FOX_K + FOX_HEADS * HEAD_DIM
_OFF_FOX_F = _OFF_FOX_V + FOX_HEADS * HEAD_DIM
_OFF_MOBA_Q = _OFF_FOX_F + FOX_HEADS
_OFF_MOBA_K = _OFF_MOBA_Q + MOBA_HEADS * HEAD_DIM
_OFF_MOBA_V = _OFF_MOBA_K + MOBA_HEADS * HEAD_DIM
IN_COLS = _OFF_MOBA_V + MOBA_HEADS * HEAD_DIM

N_HEAD_COLS = 4 * _KV + (FOX_HEADS + MOBA_HEADS) * HEAD_DIM
GATE_LANES = 3 * NSA_REP
PROJ_COLS = N_HEAD_COLS + LANES
GATE_ROWS = 16
SUBLANES = 8
_F_SLAB = GATE_LANES // SUBLANES * SUBLANES
_F_ROW0 = GATE_LANES - _F_SLAB
_T_NSA_Q = 0
_T_NSA_V = _T_NSA_Q + NSA_HEADS * HEAD_DIM
_T_GATE = _T_NSA_V + 4 * HEAD_DIM
_T_FOX_Q = _T_GATE + LANES
_T_FOX_V = _T_FOX_Q + FOX_HEADS * HEAD_DIM
_T_MOBA_Q = _T_FOX_V + FOX_HEADS * HEAD_DIM
_T_MOBA_V = _T_MOBA_Q + MOBA_HEADS * HEAD_DIM
T_ROWS = _T_MOBA_V + MOBA_HEADS * HEAD_DIM


def _dot(a, b):
    return jnp.dot(a, b, preferred_element_type=F32)


def _dot_nt(a, b, precision=None):
    return lax.dot_general(a, b, (((1,), (1,)), ((), ())), precision=precision,
                           preferred_element_type=F32)


def _iota(shape, dim):
    return lax.broadcasted_iota(jnp.int32, shape, dim)


def _params(*sem):
    return pltpu.CompilerParams(dimension_semantics=sem, vmem_limit_bytes=VMEM_LIMIT)


def _ada_norm(x, g, shift, scale):
    ms = jnp.mean(x * x, axis=-1, keepdims=True)
    y = x * lax.rsqrt(ms + EPS) * g
    return y * (1.0 + scale) + shift


def _ada_kernel(c_ref, w_ref, b_ref, o_ref):
    c = c_ref[...]
    ca = c * jax.nn.sigmoid(c)
    o_ref[0] = jnp.dot(ca, w_ref[0], precision=HIGHEST, preferred_element_type=F32) + b_ref[0]


def _ada_mod(c, w_ada, b_ada):
    depth, d, n = w_ada.shape
    b = c.shape[0]
    tn = n // 4
    return pl.pallas_call(
        _ada_kernel,
        grid=(depth, n // tn),
        in_specs=[pl.BlockSpec((b, d), lambda l, j: (0, 0)),
                  pl.BlockSpec((1, d, tn), lambda l, j: (l, 0, j)),
                  pl.BlockSpec((1, 1, tn), lambda l, j: (l, 0, j))],
        out_specs=pl.BlockSpec((1, b, tn), lambda l, j: (l, 0, j)),
        out_shape=jax.ShapeDtypeStruct((depth, b, n), F32),
        compiler_params=_params("parallel", "parallel"),
        name="ada_mod",
    )(c, w_ada, b_ada.reshape(depth, 1, n))


def _swiglu_residual(x, mod_ref, g_ref, fg_ref, w13_ref, w2_ref, *, sub, coef, final, splits):
    h = _ada_norm(x, g_ref[...], mod_ref[0, 3 * sub:3 * sub + 1, :],
                  mod_ref[0, 3 * sub + 1:3 * sub + 2, :]).astype(BF16)
    acc = None
    for lo, hi in splits:
        a = _dot(h, w13_ref[:, lo:hi])
        b = _dot(h, w13_ref[:, D_FF + lo:D_FF + hi])
        u = (a * jax.nn.sigmoid(a) * b).astype(BF16)
        part = _dot(u, w2_ref[lo:hi, :])
        acc = part if acc is None else acc + part
    y = x + (coef * mod_ref[0, 3 * sub + 2:3 * sub + 3, :]) * acc
    if final:
        y = y * lax.rsqrt(jnp.mean(y * y, axis=-1, keepdims=True) + EPS) * fg_ref[...]
    return y


def _ffn_kernel(x_ref, mod_ref, g_ref, fg_ref, w13_ref, w2_ref, o_ref, **kw):
    o_ref[...] = _swiglu_residual(x_ref[...], mod_ref, g_ref, fg_ref, w13_ref, w2_ref, **kw)


def _mix_ffn_kernel(x_ref, mod_ref, g_ref, fg_ref, on_ref, of_ref, om_ref, wo_ref, w13_ref, w2_ref, o_ref, **kw):
    n_w = NSA_HEADS * HEAD_DIM
    f_w = FOX_HEADS * HEAD_DIM
    mix = _dot(on_ref[...], wo_ref[0:n_w, :])
    mix += _dot(of_ref[...], wo_ref[n_w:n_w + f_w, :])
    mix += _dot(om_ref[...], wo_ref[n_w + f_w:, :])
    x = x_ref[...] + mod_ref[0, 5:6, :] * mix
    o_ref[...] = _swiglu_residual(x, mod_ref, g_ref, fg_ref, w13_ref, w2_ref, **kw)


def _resident(shape):
    return pl.BlockSpec(shape, lambda i: (0,) * len(shape), pipeline_mode=pl.Buffered(1))


def _hidden_splits(chunk):
    return tuple((lo, min(lo + chunk, D_FF)) for lo in range(0, D_FF, chunk))


def _ffn(x2, mod, g, final_g, w13, w2, *, sub, seq, final, tm=1024, chunk=3 * MXU_TILE):
    n, d = x2.shape
    per_b = seq // tm
    return pl.pallas_call(
        functools.partial(_ffn_kernel, sub=sub, coef=0.5, final=final, splits=_hidden_splits(chunk)),
        grid=(n // tm,),
        in_specs=[pl.BlockSpec((tm, d), lambda i: (i, 0)),
                  pl.BlockSpec((1, 9, d), lambda i: (i // per_b, 0, 0)),
                  pl.BlockSpec((1, d), lambda i: (0, 0)),
                  pl.BlockSpec((1, d), lambda i: (0, 0)),
                  _resident((d, 2 * D_FF)), _resident((D_FF, d))],
        out_specs=pl.BlockSpec((tm, d), lambda i: (i, 0)),
        out_shape=jax.ShapeDtypeStruct((n, d), F32),
        compiler_params=_params("parallel"),
        name="ffn",
    )(x2, mod, g, final_g, w13, w2)


def _mix_ffn(x2, mod, g, final_g, o_nsa, o_fox, o_moba, w_out, w13, w2, *, seq, final, tm=512,
             chunk=4 * MXU_TILE):
    n, d = x2.shape
    per_b = seq // tm
    row = lambda width: pl.BlockSpec((tm, width), lambda i: (i, 0))
    return pl.pallas_call(
        functools.partial(_mix_ffn_kernel, sub=2, coef=0.5, final=final, splits=_hidden_splits(chunk)),
        grid=(n // tm,),
        in_specs=[row(d),
                  pl.BlockSpec((1, 9, d), lambda i: (i // per_b, 0, 0)),
                  pl.BlockSpec((1, d), lambda i: (0, 0)),
                  pl.BlockSpec((1, d), lambda i: (0, 0)),
                  row(o_nsa.shape[1]), row(o_fox.shape[1]), row(o_moba.shape[1]),
                  _resident((d, d)), _resident((d, 2 * D_FF)), _resident((D_FF, d))],
        out_specs=row(d),
        out_shape=jax.ShapeDtypeStruct((n, d), F32),
        compiler_params=_params("parallel"),
        name="mix_ffn",
    )(x2, mod, g, final_g, o_nsa, o_fox, o_moba, w_out, w13, w2)


def _inproj_kernel(x_ref, mod_ref, g_ref, posr_ref, invc_ref, fb_ref, w_ref, wt_ref,
                   qt_ref, nvt_ref, gt_ref, fqt_ref, fvt_ref, mqt_ref, mvt_ref,
                   cmp_ref, nks_ref, nkw_ref, fk_ref, mk_ref, gm_ref, *, per_b):
    h = _ada_norm(x_ref[...], g_ref[...], mod_ref[0, 3:4, :], mod_ref[0, 4:5, :]).astype(BF16)
    tm = h.shape[0]
    seq_pos = (pl.program_id(0) % per_b) * tm + _iota((tm, LANES), 0)

    half = ROPE_DIM // 2
    ang_t = invc_ref[...] * posr_ref[0].astype(F32)
    cos_t = jnp.cos(ang_t)
    sin_t = jnp.sin(ang_t)
    ones_row = jnp.where(_iota((HEAD_DIM, tm), 0) == 0, 1.0, 0.0)
    slab_rows = 4 * HEAD_DIM

    def queries_t(row0, n_heads, out_ref, rotate):
        for s_idx in range(n_heads // 4):
            y_t = _dot_nt(wt_ref[row0 + slab_rows * s_idx:row0 + slab_rows * (s_idx + 1), :], h)
            for hh in range(4):
                blk = y_t[HEAD_DIM * hh:HEAD_DIM * (hh + 1)]
                if rotate:
                    r1, r2 = blk[:half], blk[half:2 * half]
                    blk = jnp.concatenate([r1 * cos_t - r2 * sin_t, r2 * cos_t + r1 * sin_t, blk[2 * half:]],
                                          axis=0)
                out_ref[0, 4 * s_idx + hh] = (blk * ATT_SCALE).astype(BF16)

    def values_t(row0, out_ref):
        y_t = _dot_nt(wt_ref[row0:row0 + slab_rows, :], h)
        for idx in range(4):
            out_ref[0, idx] = jnp.concatenate([y_t[HEAD_DIM * idx:HEAD_DIM * (idx + 1)], ones_row],
                                              axis=0).astype(BF16)

    queries_t(_T_NSA_Q, NSA_HEADS, qt_ref, True)
    values_t(_T_NSA_V, nvt_ref)
    sig_t = jax.nn.sigmoid(_dot_nt(wt_ref[_T_GATE:_T_GATE + LANES, :], h))
    for g in range(NSA_GROUPS):
        gt_ref[0, g] = sig_t[GATE_ROWS * g:GATE_ROWS * (g + 1)]
    queries_t(_T_FOX_Q, FOX_HEADS, fqt_ref, False)
    values_t(_T_FOX_V, fvt_ref)
    queries_t(_T_MOBA_Q, MOBA_HEADS, mqt_ref, True)
    values_t(_T_MOBA_V, mvt_ref)

    zeros_t = jnp.zeros((half, tm), F32)

    def lane_table(first, second, fill):
        head = jnp.concatenate([first, second, jnp.full((HEAD_DIM - ROPE_DIM, tm), fill, F32)], axis=0)
        return jnp.concatenate([head, head], axis=0).T

    cosl = lane_table(cos_t, cos_t, 1.0)
    sin_lo = lane_table(-sin_t, zeros_t, 0.0)
    sin_hi = lane_table(zeros_t, sin_t, 0.0)
    lane = _iota((tm, LANES), 1)

    def rope(y):
        return (y * cosl + pltpu.roll(y, LANES - ROPE_DIM // 2, 1) * sin_lo
                + pltpu.roll(y, ROPE_DIM // 2, 1) * sin_hi)

    def put_heads(z, out_ref, head0, dtype):
        z = z.astype(dtype)
        out_ref[0, head0] = z[:, :HEAD_DIM]
        out_ref[0, head0 + 1] = z[:, HEAD_DIM:]

    def put_wide(z, out_ref, head0, tail):
        out_ref[0, head0] = jnp.where(lane < HEAD_DIM, z, tail).astype(BF16)
        out_ref[0, head0 + 1] = jnp.where(lane < HEAD_DIM, pltpu.roll(z, HEAD_DIM, 1), tail).astype(BF16)

    def put_keys(z, out_ref, head0, block_len):
        block = lax.shift_right_logical(seq_pos, block_len.bit_length() - 1)
        put_wide(z, out_ref, head0, jnp.where(lane - HEAD_DIM == block, 1.0, 0.0))

    def slab(idx):
        y = _dot(h, w_ref[:, idx * 2 * LANES:(idx + 1) * 2 * LANES])
        return y[:, :LANES], y[:, LANES:]

    a, b = slab(0)
    put_heads(rope(a), cmp_ref, 0, F32)
    put_heads(b, cmp_ref, 2, F32)
    a, b = slab(1)
    put_keys(rope(a), nks_ref, 0, SLC_LEN)
    put_heads(rope(b), nkw_ref, 0, BF16)
    a, b = slab(2)
    put_heads(a, fk_ref, 0, BF16)
    put_heads(b, fk_ref, 2, BF16)
    a, b = slab(3)
    put_keys(rope(a), mk_ref, 0, MOBA_BLOCK)
    put_keys(rope(b), mk_ref, 2, MOBA_BLOCK)
    f = _dot(h, w_ref[:, N_HEAD_COLS:])
    gm_ref[0] = jax.nn.log_sigmoid(f + fb_ref[...])


def _inproj(x2, mod, g, positions, inv_col, fb_lane, w_p, w_t, *, batch, seq, tm=512):
    n, d = x2.shape
    per_b = seq // tm

    def hm(nh, dtype, width=HEAD_DIM):
        return (pl.BlockSpec((1, nh, tm, width), lambda i: (i // per_b, 0, i % per_b, 0)),
                jax.ShapeDtypeStruct((batch, nh, seq, width), dtype))

    def tr(nh, rows, dtype):
        return (pl.BlockSpec((1, nh, rows, tm), lambda i: (i // per_b, 0, 0, i % per_b)),
                jax.ShapeDtypeStruct((batch, nh, rows, seq), dtype))

    specs = [tr(NSA_HEADS, HEAD_DIM, BF16), tr(4, LANES, BF16), tr(NSA_GROUPS, GATE_ROWS, F32),
             tr(FOX_HEADS, HEAD_DIM, BF16), tr(FOX_HEADS, LANES, BF16),
             tr(MOBA_HEADS, HEAD_DIM, BF16), tr(MOBA_HEADS, LANES, BF16),
             hm(4, F32), hm(2, BF16, LANES), hm(2, BF16), hm(FOX_HEADS, BF16), hm(MOBA_HEADS, BF16, LANES),
             (pl.BlockSpec((1, tm, LANES), lambda i: (i // per_b, i % per_b, 0)),
              jax.ShapeDtypeStruct((batch, seq, LANES), F32))]
    const = lambda shape: pl.BlockSpec(shape, lambda i: (0,) * len(shape))
    return pl.pallas_call(
        functools.partial(_inproj_kernel, per_b=per_b),
        grid=(n // tm,),
        in_specs=[pl.BlockSpec((tm, d), lambda i: (i, 0)),
                  pl.BlockSpec((1, 9, d), lambda i: (i // per_b, 0, 0)),
                  const((1, d)),
                  pl.BlockSpec((1, 1, tm), lambda i: (i, 0, 0)),
                  const((ROPE_DIM // 2, 1)), const((1, LANES)),
                  _resident(w_p.shape), _resident(w_t.shape)],
        out_specs=[s for s, _ in specs],
        out_shape=[s for _, s in specs],
        compiler_params=_params("parallel"),
        name="inproj",
    )(x2, mod, g, positions.reshape(n // tm, 1, tm), inv_col, fb_lane, w_p, w_t)


def _compress_kernel(z_ref, pos_ref, w1_ref, w2_ref, w2t_ref, o_ref, ot_ref):
    z = z_ref[0, 0]
    top = _dot((z + pos_ref[0, 0]).astype(BF16), w1_ref[0, 0])
    bot = _dot((z + pos_ref[0, 1]).astype(BF16), w1_ref[0, 1])
    nch = z.shape[0]
    hid = top + pltpu.roll(bot, nch - 1, 0)
    act = jax.nn.gelu(hid).astype(BF16)
    o_ref[0, 0] = _dot(act, w2_ref[0]).astype(o_ref.dtype)
    ot_ref[0, 0] = _dot_nt(w2t_ref[0], act).astype(ot_ref.dtype)


def _compress(cmp_in, pos_l, w1_l, w2_l, w2t_l):
    b, four, nch, width = cmp_in.shape
    return pl.pallas_call(
        _compress_kernel,
        grid=(b, four),
        in_specs=[pl.BlockSpec((1, 1, nch, width), lambda i, n: (i, n, 0, 0)),
                  pl.BlockSpec((1, 2, 1, width), lambda i, n: (n // 2, 0, 0, 0)),
                  pl.BlockSpec((1, 2, width, CMP_HIDDEN), lambda i, n: (n // 2, 0, 0, 0)),
                  pl.BlockSpec((1, CMP_HIDDEN, HEAD_DIM), lambda i, n: (n // 2, 0, 0)),
                  pl.BlockSpec((1, HEAD_DIM, CMP_HIDDEN), lambda i, n: (n // 2, 0, 0))],
        out_specs=[pl.BlockSpec((1, 1, nch, HEAD_DIM), lambda i, n: (i, n, 0, 0)),
                   pl.BlockSpec((1, 1, HEAD_DIM, nch), lambda i, n: (i, n, 0, 0))],
        out_shape=[jax.ShapeDtypeStruct((b, four, nch, HEAD_DIM), BF16),
                   jax.ShapeDtypeStruct((b, four, HEAD_DIM, nch), BF16)],
        compiler_params=_params("parallel", "parallel"),
        name="nsa_compress",
    )(cmp_in, pos_l, w1_l, w2_l, w2t_l)


def _chunk_update_t(carry, s, v_t):
    m, acc = carry
    m_new = jnp.maximum(m, jnp.max(s, axis=0, keepdims=True))
    p = jnp.exp(s - m_new).astype(BF16)
    return m_new, jnp.exp(m - m_new) * acc + _dot(v_t, p)


def _chunk_init_t(n):
    return jnp.full((1, n), NEG, F32), jnp.zeros((LANES, n), F32)


def _chunk_finish_t(carry):
    _, acc = carry
    return acc[:HEAD_DIM] / jnp.maximum(acc[HEAD_DIM:HEAD_DIM + 1], 1e-30)


def _heads_to_rows(heads_t):
    pairs = [jnp.concatenate(heads_t[a:a + 2], axis=0).T for a in range(0, len(heads_t), 2)]
    return jnp.concatenate(pairs, axis=1)


def _nsa_kernel(qt_ref, zk_ref, zv_ref, cpos_ref, cw1_ref, cw2_ref, cw2t_ref, ks_ref, vst_ref, kw_ref, vwt_ref,
                gt_ref, ovt_ref, o_ref, *, tq, tk):
    seq = ks_ref.shape[2]

    def hidden(z_ref, kind):
        z = z_ref[0, 0]
        top = _dot((z + cpos_ref[kind, 0]).astype(BF16), cw1_ref[kind, 0])
        bot = _dot((z + cpos_ref[kind, 1]).astype(BF16), cw1_ref[kind, 1])
        return jax.nn.gelu(top + pltpu.roll(bot, z.shape[0] - 1, 0)).astype(BF16)

    kc = _dot(hidden(zk_ref, 0), cw2_ref[0]).astype(BF16)
    vct = _dot_nt(cw2t_ref[1], hidden(zv_ref, 1)).astype(BF16)
    for i in range(seq // tq):
        q_cols = slice(i * tq, (i + 1) * tq)
        _nsa_tile(i, qt_ref.at[:, :, :, q_cols], kc, vct, ks_ref, vst_ref, kw_ref, vwt_ref,
                  gt_ref.at[:, :, :, q_cols], ovt_ref, o_ref.at[:, q_cols, :], tq=tq, tk=tk)


def _nsa_tile(i, qt_ref, kc, vct, ks_ref, vst_ref, kw_ref, vwt_ref, gt_ref, ovt_ref, o_ref, *, tq, tk):
    rep = NSA_REP
    groups = range(qt_ref.shape[1] // rep)
    n_c = kc.shape[0]
    n_s = ks_ref.shape[2] // SLC_LEN
    tile = lambda a: jnp.concatenate([a] * rep, axis=1)
    t_q = i * tq + _iota((1, tq), 1)
    cend = _iota((n_c, 1), 0) * CMP_STRIDE + (CMP_LEN - 1)
    cm = cend <= tile(t_q)
    j_t = _iota((n_s, tq), 0)
    tb = lax.shift_right_logical(t_q, SLC_LEN.bit_length() - 1)
    valid = j_t <= tb
    forced = (j_t == 0) | (j_t == tb) | (j_t == tb - 1)
    start = max(i * tq - WIN, 0)
    span = (i + 1) * tq - start
    dist = t_q - (start + _iota((span, 1), 0))
    band = tile(jnp.where((dist >= 0) & (dist < WIN), 0.0, NEG))

    o_cmp, o_win, q4a = [], [], []
    for g in groups:
        q4 = jnp.concatenate([qt_ref[0, rep * g + r] for r in range(rep)], axis=1)
        s = jnp.where(cm, _dot(kc, q4), NEG)
        e = jnp.where(cm, jnp.exp(s - jnp.max(s, axis=0, keepdims=True)), 0.0)
        p = e / jnp.maximum(jnp.sum(e, axis=0, keepdims=True), 1e-30)
        o_cmp.append(_dot(vct, p.astype(BF16)))
        psum = p[:, 0:tq] + p[:, tq:2 * tq] + p[:, 2 * tq:3 * tq] + p[:, 3 * tq:4 * tq]
        imp = jnp.dot(ovt_ref[...], psum, precision=HIGHEST, preferred_element_type=F32)[:n_s]
        v_t = jnp.where(valid, jnp.where(forced, jnp.inf, imp), -jnp.inf)
        rank = jnp.zeros((n_s, tq), jnp.int32)
        for jp in range(n_s):
            row = v_t[jp:jp + 1, :]
            ahead = (row > v_t) | ((row == v_t) & (j_t > jp))
            rank = rank + ahead.astype(jnp.int32)
        neg_t = jnp.where((rank < SLC_TOPN) & (v_t > -jnp.inf), 0.0, NEG)
        neg = jnp.concatenate([neg_t, jnp.zeros((HEAD_DIM - n_s, tq), F32)], axis=0)
        q4a.append(jnp.concatenate([q4, tile(neg).astype(BF16)], axis=0))
        sw = _dot(kw_ref[0, g, start:start + span, :], q4) + band
        pw = jnp.exp(sw - jnp.max(sw, axis=0, keepdims=True)).astype(BF16)
        aw = _dot(vwt_ref[0, g, :, start:start + span], pw)
        o_win.append(aw[:HEAD_DIM] / jnp.maximum(aw[HEAD_DIM:HEAD_DIM + 1], 1e-30))

    def slc_step(k0, k1, carries, diag):
        new = []
        for g in groups:
            sc = _dot(ks_ref[0, g, k0:k1, :], q4a[g])
            if diag:
                sc = sc + tile(jnp.where(k0 + _iota((k1 - k0, 1), 0) <= t_q, 0.0, NEG))
            new.append(_chunk_update_t(carries[g], sc, vst_ref[0, g, :, k0:k1]))
        return tuple(new)

    end = (i + 1) * tq
    last = (end - 1) // tk * tk
    carries = slc_step(last, end, tuple(_chunk_init_t(rep * tq) for _ in groups), True)
    for k0 in range(0, last, tk):
        carries = slc_step(k0, k0 + tk, carries, False)

    heads = []
    for g in groups:
        o_slc = _chunk_finish_t(carries[g])
        gates = gt_ref[0, g]
        for r in range(rep):
            cols = slice(r * tq, (r + 1) * tq)
            heads.append(gates[3 * r:3 * r + 1] * o_cmp[g][:, cols]
                         + gates[3 * r + 1:3 * r + 2] * o_slc[:, cols]
                         + gates[3 * r + 2:3 * r + 3] * o_win[g][:, cols])
    o_ref[0] = _heads_to_rows(heads).astype(o_ref.dtype)


def _nsa(qt, cmp_in, cpos, cw1, cw2, cw2t, nks, nkw, nvt, gt, ovt, *, tq=256, tk=512):
    b, _, _, seq = qt.shape
    n_c, width = cmp_in.shape[2:]
    ng = NSA_GROUPS
    const = lambda a: pl.BlockSpec(a.shape, lambda bi, g: (0,) * a.ndim, pipeline_mode=pl.Buffered(1))
    return pl.pallas_call(
        functools.partial(_nsa_kernel, tq=tq, tk=tk),
        grid=(b, ng),
        in_specs=[pl.BlockSpec((1, NSA_REP, HEAD_DIM, seq), lambda bi, g: (bi, g, 0, 0)),
                  pl.BlockSpec((1, 1, n_c, width), lambda bi, g: (bi, g, 0, 0)),
                  pl.BlockSpec((1, 1, n_c, width), lambda bi, g: (bi, ng + g, 0, 0)),
                  const(cpos), const(cw1), const(cw2), const(cw2t),
                  pl.BlockSpec((1, 1, seq, LANES), lambda bi, g: (bi, g, 0, 0)),
                  pl.BlockSpec((1, 1, LANES, seq), lambda bi, g: (bi, g, 0, 0)),
                  pl.BlockSpec((1, 1, seq, HEAD_DIM), lambda bi, g: (bi, g, 0, 0)),
                  pl.BlockSpec((1, 1, LANES, seq), lambda bi, g: (bi, ng + g, 0, 0)),
                  pl.BlockSpec((1, 1, GATE_ROWS, seq), lambda bi, g: (bi, g, 0, 0)),
                  const(ovt)],
        out_specs=pl.BlockSpec((1, seq, NSA_REP * HEAD_DIM), lambda bi, g: (bi, 0, g)),
        out_shape=jax.ShapeDtypeStruct((b, seq, NSA_HEADS * HEAD_DIM), BF16),
        compiler_params=_params("parallel", "parallel"),
        name="nsa",
    )(qt, cmp_in, cmp_in, cpos, cw1, cw2, cw2t, nks, nvt, nkw, nvt, gt, ovt)


def _fox_kernel(qt_ref, k_ref, vt_ref, lf_ref, cum_ref, tri_ref, o_ref, fcol_ref, frow_ref, *, tq):
    heads = range(FOX_HEADS)
    seq = lf_ref.shape[1]
    blk = cum_ref.shape[0]
    carry = jnp.zeros((1, LANES), F32)
    for c in range(seq // blk):
        rows = slice(c * blk, (c + 1) * blk)
        cs = jnp.dot(cum_ref[...], lf_ref[0, rows, :], precision=HIGHEST,
                     preferred_element_type=F32) + carry
        fcol_ref[rows, :] = cs
        carry = cs[blk - 1:blk, :]
    frow_ref[...] = fcol_ref[...].T[_F_SLAB:_F_SLAB + SUBLANES, :]

    tk = tri_ref.shape[1]
    per_chunk = tk // tq
    for i in range(seq // tq):
        q_cols = slice(i * tq, (i + 1) * tq)
        qs = [qt_ref[0, h, :, q_cols] for h in heads]
        fqs = [frow_ref[_F_ROW0 + h:_F_ROW0 + h + 1, q_cols] for h in heads]

        def step(c, carries, diag, qs=qs, fqs=fqs, i=i):
            keys = slice(c * tk, (c + 1) * tk)
            new = []
            for h in heads:
                fk = fcol_ref[keys, GATE_LANES + h:GATE_LANES + h + 1]
                sc = _dot(k_ref[0, h, keys, :], qs[h]) + (fqs[h] - fk)
                if diag:
                    sc = sc + tri_ref[i % per_chunk]
                new.append(_chunk_update_t(carries[h], sc, vt_ref[0, h, :, keys]))
            return tuple(new)

        carries = step(i // per_chunk, tuple(_chunk_init_t(tq) for _ in heads), True)
        for c in range(i // per_chunk):
            carries = step(c, carries, False)
        o_ref[0, q_cols, :] = _heads_to_rows([_chunk_finish_t(cr) for cr in carries]).astype(o_ref.dtype)


def _fox(fqt, fk, fvt, gm, cum, tri_t, *, tq=FOX_TILE):
    b, nh, _, seq = fqt.shape
    blk = cum.shape[0]
    return pl.pallas_call(
        functools.partial(_fox_kernel, tq=tq),
        grid=(b,),
        in_specs=[pl.BlockSpec((1, nh, HEAD_DIM, seq), lambda bi: (bi, 0, 0, 0)),
                  pl.BlockSpec((1, nh, seq, HEAD_DIM), lambda bi: (bi, 0, 0, 0)),
                  pl.BlockSpec((1, nh, LANES, seq), lambda bi: (bi, 0, 0, 0)),
                  pl.BlockSpec((1, seq, LANES), lambda bi: (bi, 0, 0)),
                  pl.BlockSpec((blk, blk), lambda bi: (0, 0)),
                  pl.BlockSpec(tri_t.shape, lambda bi: (0, 0, 0))],
        out_specs=pl.BlockSpec((1, seq, nh * HEAD_DIM), lambda bi: (bi, 0, 0)),
        out_shape=jax.ShapeDtypeStruct((b, seq, nh * HEAD_DIM), BF16),
        scratch_shapes=[pltpu.VMEM((seq, LANES), F32), pltpu.VMEM((8, seq), F32)],
        compiler_params=_params("parallel"),
        name="fox",
    )(fqt, fk, fvt, gm, cum, tri_t)


def _moba_kernel(qt_ref, k_ref, vt_ref, avg_ref, tri_ref, o_ref, *, tq):
    heads = range(MOBA_HEADS)
    nb = avg_ref.shape[0]
    seq = k_ref.shape[2]
    kms = [_dot(avg_ref[...], k_ref[0, h])[:, :HEAD_DIM] for h in heads]
    blk = _iota((nb, tq), 0)
    in_tile = lax.shift_right_logical(_iota((nb, tq), 1), MOBA_BLOCK.bit_length() - 1)

    for i in range(seq // tq):
        q_cols = slice(i * tq, (i + 1) * tq)
        own = i * (tq // MOBA_BLOCK) + in_tile
        qas = []
        for h in heads:
            q = qt_ref[0, h, :, q_cols]
            gate = jnp.dot(kms[h], q.astype(F32), precision=HIGHEST, preferred_element_type=F32)
            val = jnp.where(blk < own, gate, -jnp.inf)
            rank = jnp.zeros((nb, tq), jnp.int32)
            for jp in range(nb):
                row = val[jp:jp + 1, :]
                ahead = (row > val) | ((row == val) & (blk > jp))
                rank = rank + ahead.astype(jnp.int32)
            keep = ((rank < MOBA_TOPK) & (blk < own)) | (blk == own)
            neg = jnp.concatenate([jnp.where(keep, 0.0, NEG), jnp.zeros((HEAD_DIM - nb, tq), F32)], axis=0)
            qas.append(jnp.concatenate([q, neg.astype(BF16)], axis=0))

        def step(c, carries, diag, qas=qas):
            keys = slice(c * tq, (c + 1) * tq)
            new = []
            for h in heads:
                sc = _dot(k_ref[0, h, keys, :], qas[h])
                if diag:
                    sc = sc + tri_ref[...]
                new.append(_chunk_update_t(carries[h], sc, vt_ref[0, h, :, keys]))
            return tuple(new)

        carries = step(i, tuple(_chunk_init_t(tq) for _ in heads), True)
        for c in range(i):
            carries = step(c, carries, False)
        o_ref[0, q_cols, :] = _heads_to_rows([_chunk_finish_t(cr) for cr in carries]).astype(o_ref.dtype)


def _moba(mqt, mk, mvt, avg, tri_t, *, tq=MOBA_TILE):
    b, nh, _, seq = mqt.shape
    nb = avg.shape[0]
    return pl.pallas_call(
        functools.partial(_moba_kernel, tq=tq),
        grid=(b,),
        in_specs=[pl.BlockSpec((1, nh, HEAD_DIM, seq), lambda bi: (bi, 0, 0, 0)),
                  pl.BlockSpec((1, nh, seq, LANES), lambda bi: (bi, 0, 0, 0)),
                  pl.BlockSpec((1, nh, LANES, seq), lambda bi: (bi, 0, 0, 0)),
                  pl.BlockSpec((nb, seq), lambda bi: (0, 0)),
                  pl.BlockSpec((tq, tq), lambda bi: (0, 0))],
        out_specs=pl.BlockSpec((1, seq, nh * HEAD_DIM), lambda bi: (bi, 0, 0)),
        out_shape=jax.ShapeDtypeStruct((b, seq, nh * HEAD_DIM), BF16),
        compiler_params=_params("parallel"),
        name="moba",
    )(mqt, mk, mvt, avg, tri_t)


def _pack_w_in(w_in):
    depth, d, _ = w_in.shape

    def cols(a, b):
        return w_in[:, :, a:b]

    zeros = lambda n: jnp.zeros((depth, d, n), w_in.dtype)
    g_half = 3 * NSA_REP
    token_major = [cols(_OFF_KC, _OFF_KS),
                   cols(_OFF_KS, _OFF_VS), cols(_OFF_KW, _OFF_VW),
                   cols(_OFF_FOX_K, _OFF_FOX_V),
                   cols(_OFF_MOBA_K, _OFF_MOBA_V),
                   zeros(GATE_LANES), cols(_OFF_FOX_F, _OFF_FOX_F + FOX_HEADS),
                   zeros(LANES - GATE_LANES - FOX_HEADS)]
    transposed = [cols(_OFF_NSA_Q, _OFF_KC),
                  cols(_OFF_VS, _OFF_KW), cols(_OFF_VW, _OFF_GATE),
                  cols(_OFF_GATE, _OFF_GATE + g_half), zeros(GATE_ROWS - g_half),
                  cols(_OFF_GATE + g_half, _OFF_GATE + 2 * g_half), zeros(LANES - GATE_ROWS - g_half),
                  cols(_OFF_FOX_Q, _OFF_FOX_K), cols(_OFF_FOX_V, _OFF_FOX_F),
                  cols(_OFF_MOBA_Q, _OFF_MOBA_K), cols(_OFF_MOBA_V, IN_COLS)]
    w_p = jnp.concatenate(token_major, axis=-1).astype(BF16)
    w_t = jnp.swapaxes(jnp.concatenate(transposed, axis=-1), 1, 2).astype(BF16)
    return w_p, w_t


def _constants(seq):
    n_c = seq // CMP_STRIDE
    n_s = seq // SLC_LEN
    c0 = np.arange(n_c)[None, :] * CMP_STRIDE
    j0 = np.arange(LANES)[:, None] * SLC_LEN
    real = (np.arange(n_c)[None, :] < (seq - CMP_LEN) // CMP_STRIDE + 1) & (np.arange(LANES)[:, None] < n_s)
    ov = ((c0 < j0 + SLC_LEN) & (c0 + CMP_LEN > j0) & real).astype(np.float32)
    nb = seq // MOBA_BLOCK
    avg = (np.arange(nb)[:, None] == (np.arange(seq)[None, :] // MOBA_BLOCK)).astype(np.float32) / MOBA_BLOCK
    tri = np.tril(np.ones((256, 256), np.float32))

    kk = np.arange(FOX_CHUNK)[None, :, None]
    qq = np.arange(FOX_TILE)[None, None, :] + FOX_TILE * np.arange(FOX_CHUNK // FOX_TILE)[:, None, None]
    causal_fox = np.where(kk <= qq, 0.0, NEG).astype(np.float32)
    r = np.arange(MOBA_TILE)
    same = (r[:, None] // MOBA_BLOCK) == (r[None, :] // MOBA_BLOCK)
    allowed = np.where(same, r[:, None] <= r[None, :], r[:, None] < r[None, :])
    causal_moba = np.where(allowed, 0.0, NEG).astype(np.float32)
    return dict(ov=jnp.asarray(ov), avg=jnp.asarray(avg, dtype=BF16), tri=jnp.asarray(tri),
                causal_fox=jnp.asarray(causal_fox), causal_moba=jnp.asarray(causal_moba))


def kernel(x, c, positions, norm_g, w_ada, b_ada, w_in, fox_fbias, cmp_pos, cmp_w1, cmp_w2, w_out,
           ffn_w13, ffn_w2, final_g):
    batch, seq, d = x.shape
    depth = w_ada.shape[0]
    assert d == D_MODEL and seq % MOBA_BLOCK == 0 and seq % 1024 == 0

    w_in_p, w_in_t = _pack_w_in(w_in)
    w13 = ffn_w13.astype(BF16)
    w2 = ffn_w2.astype(BF16)
    w_o = w_out.astype(BF16)
    half = CMP_LEN * HEAD_DIM // 2
    cw1 = cmp_w1.astype(BF16).reshape(depth, 2, 2, half, CMP_HIDDEN)
    cw2 = cmp_w2.astype(BF16)
    cw2t = jnp.swapaxes(cw2, 2, 3)
    cpos = cmp_pos.reshape(depth, 2, 2, 1, half)
    cst = _constants(seq)

    inv = ROPE_THETA ** (-jnp.arange(0, ROPE_DIM, 2, dtype=F32) / ROPE_DIM)
    inv_col = inv.reshape(ROPE_DIM // 2, 1)
    fb_lane = jnp.zeros((depth, 1, LANES), F32).at[:, 0, GATE_LANES:GATE_LANES + FOX_HEADS].set(fox_fbias)

    mod_all = _ada_mod(c, w_ada, b_ada).reshape(depth, batch, 9, d)
    x2 = x.reshape(batch * seq, d)
    fg = final_g.reshape(1, d)
    for l in range(depth):
        mod = mod_all[l]
        x2 = _ffn(x2, mod, norm_g[l, 0:1], fg, w13[l, 0], w2[l, 0], sub=0, seq=seq, final=False)
        qt, nvt, gt, fqt, fvt, mqt, mvt, cmp_in, nks, nkw, fk, mk, gm = _inproj(
            x2, mod, norm_g[l, 1:2], positions, inv_col, fb_lane[l], w_in_p[l], w_in_t[l],
            batch=batch, seq=seq)
        o_nsa = _nsa(qt, cmp_in.reshape(batch, 4, seq // CMP_STRIDE, CMP_STRIDE * HEAD_DIM),
                     cpos[l], cw1[l], cw2[l], cw2t[l], nks, nkw, nvt, gt, cst["ov"])
        o_fox = _fox(fqt, fk, fvt, gm, cst["tri"], cst["causal_fox"])
        o_moba = _moba(mqt, mk, mvt, cst["avg"], cst["causal_moba"])
        x2 = _mix_ffn(x2, mod, norm_g[l, 2:3], fg, o_nsa.reshape(batch * seq, -1),
                      o_fox.reshape(batch * seq, -1), o_moba.reshape(batch * seq, -1), w_o[l],
                      w13[l, 1], w2[l, 1], seq=seq, final=(l == depth - 1))
    return x2.reshape(batch, seq, d)
```
